```python
import math
import jax, jax.numpy as jnp
from jax import lax
import numpy as np

D_MODEL = 1024
BATCH = 4
SEQ = 4096
DEPTH = 2
DEC_BATCH = 32
DEC_SEQ = 1
PAST_LEN = 8192
PAGE_SIZE = 128

N_HEADS = 16
HEAD_DIM = D_MODEL // N_HEADS
D_FF = 4 * D_MODEL
NORM_EPS = 1e-6
ATTN_SCALE = HEAD_DIM ** -0.5
REL_BUCKETS = 32
REL_MAX_DIST = 128
Q_CHUNK = 128
MOBA_BLOCK = 256
MOBA_TOPK = 3
NSA_KV_HEADS = 4
NSA_GROUP = N_HEADS // NSA_KV_HEADS
CMP_BLOCK = 32
CMP_STRIDE = 16
CMP_HIDDEN = 2 * HEAD_DIM
SEL_BLOCK = 64
SEL_TOPK = 16
N_LOCAL_SEL = 2
WINDOW = 512
N_BRANCH = 3
NSA_IN = N_HEADS * HEAD_DIM + N_BRANCH * 2 * NSA_KV_HEADS * HEAD_DIM + N_BRANCH * N_HEADS
N_MOBA_LAYERS = (DEPTH + 1) // 2
N_NSA_LAYERS = DEPTH // 2

kernel_name = "moba_nsa_hybrid_step"


def rms_norm(x, g):
    xf = x.astype(jnp.float32)
    y = xf * lax.rsqrt(jnp.mean(xf * xf, axis=-1, keepdims=True) + NORM_EPS)
    return (y * g.astype(jnp.float32)).astype(x.dtype)


def rel_bucket(dist):
    n = jnp.maximum(dist, 0)
    max_exact = REL_BUCKETS // 2
    nf = jnp.maximum(n, 1).astype(jnp.float32)
    large = max_exact + (jnp.log(nf / max_exact) / math.log(REL_MAX_DIST / max_exact)
                         * (REL_BUCKETS - max_exact)).astype(jnp.int32)
    large = jnp.minimum(large, REL_BUCKETS - 1)
    return jnp.where(n < max_exact, n, large)


def masked_softmax(s, mask):
    s = jnp.where(mask, s.astype(jnp.float32), -jnp.inf)
    m = jnp.max(s, axis=-1, keepdims=True)
    m = jnp.where(jnp.isfinite(m), m, 0.0)
    e = jnp.exp(s - m)
    return e / jnp.maximum(jnp.sum(e, axis=-1, keepdims=True), 1e-30)


def sqrelu_mlp(x, w_up, w_down):
    return jnp.square(jax.nn.relu(x @ w_up)) @ w_down


def moba_project(xn, w_qkv, q_gain, k_gain):
    lead = xn.shape[:-1]
    h = (xn @ w_qkv).reshape(*lead, 3, N_HEADS, HEAD_DIM)
    q = rms_norm(h[..., 0, :, :], q_gain)
    kv = jnp.stack([rms_norm(h[..., 1, :, :], k_gain), h[..., 2, :, :]], axis=-2)
    return q, kv


def block_means(kv, nb):
    b = kv.shape[0]
    k = kv[:, :nb * MOBA_BLOCK, :, 0, :].reshape(b, nb, MOBA_BLOCK, N_HEADS, HEAD_DIM)
    return jnp.mean(k.astype(jnp.float32), axis=2)


def moba_attend(q, q_pos, kv, k_mean, kv_loc, loc_pos, rel_bias):
    tq = q.shape[0]
    nb = k_mean.shape[0]
    cur = q_pos // MOBA_BLOCK
    gate = jnp.einsum('thd,nhd->thn', q.astype(jnp.float32), k_mean)
    past = jnp.arange(nb)[None, None, :] < cur[:, None, None]
    _, sel = lax.top_k(jnp.where(past, gate, -jnp.inf), MOBA_TOPK)
    sel_ok = jnp.arange(MOBA_TOPK)[None, :] < cur[:, None]
    hidx = jnp.arange(N_HEADS)[None, :, None, None]
    kpos = sel[..., None] * MOBA_BLOCK + jnp.arange(MOBA_BLOCK)
    kv_sel = kv[kpos, hidx]
    s_sel = jnp.einsum('thd,thkbd->thkb', q, kv_sel[..., 0, :]).astype(jnp.float32) * ATTN_SCALE
    s_sel = s_sel + rel_bias.T[hidx, rel_bucket(q_pos[:, None, None, None] - kpos)]
    s_loc = jnp.einsum('thd,lhd->thl', q, kv_loc[:, :, 0, :]).astype(jnp.float32) * ATTN_SCALE
    s_loc = s_loc + jnp.moveaxis(rel_bias[rel_bucket(q_pos[:, None] - loc_pos[None, :])], -1, 1)
    n_sel = MOBA_TOPK * MOBA_BLOCK
    m_sel = jnp.broadcast_to(sel_ok[:, None, :, None], s_sel.shape).reshape(tq, N_HEADS, n_sel)
    m_loc = (loc_pos[None, :] // MOBA_BLOCK == cur[:, None]) & (loc_pos[None, :] <= q_pos[:, None])
    m_loc = jnp.broadcast_to(m_loc[:, None, :], s_loc.shape)
    p = masked_softmax(jnp.concatenate([s_sel.reshape(tq, N_HEADS, n_sel), s_loc], axis=-1),
                       jnp.concatenate([m_sel, m_loc], axis=-1)).astype(q.dtype)
    o = jnp.einsum('thk,thkd->thd', p[..., :n_sel],
                   kv_sel[..., 1, :].reshape(tq, N_HEADS, n_sel, HEAD_DIM))
    return o + jnp.einsum('thl,lhd->thd', p[..., n_sel:], kv_loc[:, :, 1, :])


def moba_prompt(q, kv, rel_bias):
    b, s = q.shape[:2]
    nb = max(-(-s // MOBA_BLOCK), MOBA_TOPK)
    kv_p = jnp.pad(kv, ((0, 0), (0, nb * MOBA_BLOCK + Q_CHUNK - s), (0, 0), (0, 0), (0, 0)))
    k_mean = block_means(kv_p, nb)
    nch = s // Q_CHUNK
    starts = jnp.arange(nch, dtype=jnp.int32) * Q_CHUNK
    n_loc = MOBA_BLOCK + Q_CHUNK

    def per_seq(args):
        q_b, kv_b, km_b = args

        def per_chunk(cargs):
            q_c, p0 = cargs
            ls = (p0 // MOBA_BLOCK) * MOBA_BLOCK
            kv_loc = lax.dynamic_slice_in_dim(kv_b, ls, n_loc, 0)
            return moba_attend(q_c, p0 + jnp.arange(Q_CHUNK), kv_b, km_b, kv_loc,
                               ls + jnp.arange(n_loc), rel_bias)
        return lax.map(per_chunk, (q_b.reshape(nch, Q_CHUNK, N_HEADS, HEAD_DIM), starts))

    o = lax.map(per_seq, (q, kv_p, k_mean))
    return o.reshape(b, s, N_HEADS, HEAD_DIM)


def moba_sample(q, kv_new, cache, j, page_table, rel_bias):
    db, t = q.shape[:2]
    past = page_table.shape[1] * PAGE_SIZE
    nb = max(-(-(past + t) // MOBA_BLOCK), MOBA_TOPK)
    tf = nb * MOBA_BLOCK + t
    kv = jnp.concatenate([
        cache[j, page_table].reshape(db, past, N_HEADS, 2, HEAD_DIM),
        kv_new,
        jnp.zeros((db, tf - past - t, N_HEADS, 2, HEAD_DIM), kv_new.dtype)], axis=1)
    k_mean = block_means(kv, nb)
    ls = (past // MOBA_BLOCK) * MOBA_BLOCK
    n_loc = MOBA_BLOCK + t
    kv_loc = kv[:, ls:ls + n_loc]
    return jax.vmap(moba_attend, in_axes=(0, None, 0, 0, 0, None, None))(
        q, past + jnp.arange(t), kv, k_mean, kv_loc, ls + jnp.arange(n_loc), rel_bias)


def norm_kv(kv, g):
    return jnp.stack([rms_norm(kv[..., 0, :], g), kv[..., 1, :]], axis=-2)


def nsa_project(xn, w_in, gate_bias, q_gain, k_gain):
    lead = xn.shape[:-1]
    h = xn @ w_in
    qd = N_HEADS * HEAD_DIM
    kvd = N_BRANCH * 2 * NSA_KV_HEADS * HEAD_DIM
    q = rms_norm(h[..., :qd].reshape(*lead, N_HEADS, HEAD_DIM), q_gain)
    kv = h[..., qd:qd + kvd].reshape(*lead, N_BRANCH, NSA_KV_HEADS, 2, HEAD_DIM)
    kv_cmp = kv[..., 0, :, :, :]
    kv_sel = norm_kv(kv[..., 1, :, :, :], k_gain[1])
    kv_win = norm_kv(kv[..., 2, :, :, :], k_gain[2])
    gate = jax.nn.sigmoid(h[..., qd + kvd:] + gate_bias).reshape(*lead, N_HEADS, N_BRANCH)
    return q, gate, kv_cmp, kv_sel, kv_win


def nsa_compress(kv, cmp_pos, w1, b1, w2, k_gain):
    b, tp = kv.shape[:2]
    nseg = tp // CMP_STRIDE
    n_part = CMP_BLOCK // CMP_STRIDE
    nc = nseg - n_part + 1
    seg = kv.reshape(b, nseg, CMP_STRIDE, NSA_KV_HEADS, 2, HEAD_DIM)
    h = b1[None, None, None, :, :]
    for m in range(n_part):
        sl = slice(m * CMP_STRIDE, (m + 1) * CMP_STRIDE)
        part = seg[:, m:m + nc] + cmp_pos[sl][None, None, :, None, :, :]
        h = h + jnp.einsum('bnigsc,sice->bngse', part, w1[:, sl])
    out = jnp.einsum('bngse,sed->bngsd', jax.nn.gelu(h), w2)
    return rms_norm(out[..., 0, :], k_gain), out[..., 1, :]


def nsa_attend(q, q_pos, gate, kc, vc, kv_sel, kv_win, win_pos, rel_bias):
    tq = q.shape[0]
    qg = q.reshape(tq, NSA_KV_HEADS, NSA_GROUP, HEAD_DIM)
    rel_g = rel_bias.reshape(REL_BUCKETS, NSA_KV_HEADS, NSA_GROUP)

    def tok_bias(dist):
        return jnp.transpose(rel_g[rel_bucket(dist)], (0, 2, 3, 1))

    n_cmp = kc.shape[0]
    c_end = jnp.arange(n_cmp) * CMP_STRIDE + (CMP_BLOCK - 1)
    d_c = q_pos[:, None] - c_end[None, :]
    s_c = jnp.einsum('tgpd,ngd->tgpn', qg, kc).astype(jnp.float32) * ATTN_SCALE + tok_bias(d_c)
    p_c = masked_softmax(s_c, (d_c >= 0)[:, None, None, :])
    o_c = jnp.einsum('tgpn,ngd->tgpd', p_c.astype(q.dtype), vc)
    n_sel = kv_sel.shape[0] // SEL_BLOCK
    ratio = SEL_BLOCK // CMP_STRIDE
    lead = CMP_BLOCK // CMP_STRIDE - 1
    taps = ratio + lead
    imp = jnp.pad(jnp.sum(p_c, axis=2), ((0, 0), (0, 0), (lead, n_sel * ratio + taps - n_cmp - lead)))
    imp = sum(imp[..., u:u + n_sel * ratio:ratio] for u in range(taps))
    cur = q_pos // SEL_BLOCK
    blk = jnp.arange(n_sel)[None, :]
    valid = blk <= cur[:, None]
    forced = (blk == 0) | (valid & (blk >= cur[:, None] - (N_LOCAL_SEL - 1)))
    score = jnp.where(forced[:, None, :], jnp.inf, imp)
    score = jnp.where(valid[:, None, :], score, -jnp.inf)
    _, sel = lax.top_k(score, SEL_TOPK)
    sel_ok = jnp.arange(SEL_TOPK)[None, :] <= cur[:, None]
    gidx = jnp.arange(NSA_KV_HEADS)[None, :, None, None]
    kpos = sel[..., None] * SEL_BLOCK + jnp.arange(SEL_BLOCK)
    nk = SEL_TOPK * SEL_BLOCK
    kvs = kv_sel[kpos, gidx].reshape(tq, NSA_KV_HEADS, nk, 2, HEAD_DIM)
    kpos = kpos.reshape(tq, NSA_KV_HEADS, nk)
    s_s = jnp.einsum('tgpd,tgkd->tgpk', qg, kvs[..., 0, :]).astype(jnp.float32) * ATTN_SCALE
    rel_gt = jnp.transpose(rel_g, (1, 2, 0))
    s_s = s_s + rel_gt[gidx, jnp.arange(NSA_GROUP)[None, None, :, None],
                       rel_bucket(q_pos[:, None, None] - kpos)[:, :, None, :]]
    m_s = (jnp.repeat(sel_ok, SEL_BLOCK, axis=-1)[:, None, :] & (kpos <= q_pos[:, None, None]))[:, :, None, :]
    p_s = masked_softmax(s_s, m_s)
    o_s = jnp.einsum('tgpk,tgkd->tgpd', p_s.astype(q.dtype), kvs[..., 1, :])
    d_w = q_pos[:, None] - win_pos[None, :]
    s_w = jnp.einsum('tgpd,lgd->tgpl', qg, kv_win[:, :, 0, :]).astype(jnp.float32) * ATTN_SCALE + tok_bias(d_w)
    m_w = ((d_w >= 0) & (d_w < WINDOW) & (win_pos[None, :] >= 0))[:, None, None, :]
    p_w = masked_softmax(s_w, m_w)
    o_w = jnp.einsum('tgpl,lgd->tgpd', p_w.astype(q.dtype), kv_win[:, :, 1, :])
    g = gate.reshape(tq, NSA_KV_HEADS, NSA_GROUP, N_BRANCH)
    o = g[..., 0:1] * o_c + g[..., 1:2] * o_s + g[..., 2:3] * o_w
    return o.reshape(tq, N_HEADS, HEAD_DIM)


def nsa_prompt(q, gate, kv_cmp, kv_sel, kv_win, cmp_par, rel_bias):
    b, s = q.shape[:2]
    n_sel = max(-(-s // SEL_BLOCK), SEL_TOPK)
    tp = n_sel * SEL_BLOCK
    pad_t = ((0, 0), (0, tp - s), (0, 0), (0, 0), (0, 0))
    kc, vc = nsa_compress(jnp.pad(kv_cmp, pad_t), *cmp_par)
    kv_sel_p = jnp.pad(kv_sel, pad_t)
    kv_win_p = jnp.pad(kv_win, ((0, 0), (WINDOW, 0), (0, 0), (0, 0), (0, 0)))
    nch = s // Q_CHUNK
    n_win = WINDOW + Q_CHUNK
    starts = jnp.arange(nch, dtype=jnp.int32) * Q_CHUNK

    def per_seq(args):
        q_b, g_b, kc_b, vc_b, ks_b, kw_b = args

        def per_chunk(cargs):
            q_c, g_c, p0 = cargs
            kw = lax.dynamic_slice_in_dim(kw_b, p0, n_win, 0)
            return nsa_attend(q_c, p0 + jnp.arange(Q_CHUNK), g_c, kc_b, vc_b, ks_b, kw,
                              p0 - WINDOW + jnp.arange(n_win), rel_bias)
        return lax.map(per_chunk, (q_b.reshape(nch, Q_CHUNK, N_HEADS, HEAD_DIM),
                                   g_b.reshape(nch, Q_CHUNK, N_HEADS, N_BRANCH), starts))

    o = lax.map(per_seq, (q, gate, kc, vc, kv_sel_p, kv_win_p))
    return o.reshape(b, s, N_HEADS, HEAD_DIM)


def nsa_sample(q, gate, kv_cmp_new, kv_sel_new, kv_win_new, cache_cmp, cache_sel, win_state, j,
               page_table, cmp_par, rel_bias):
    db, t = q.shape[:2]
    past = page_table.shape[1] * PAGE_SIZE
    n_sel = max(-(-(past + t) // SEL_BLOCK), SEL_TOPK)
    tp = n_sel * SEL_BLOCK

    def full_rows(cache, new):
        return jnp.concatenate([
            cache[j, page_table].reshape(db, past, NSA_KV_HEADS, 2, HEAD_DIM),
            new,
            jnp.zeros((db, tp - past - t, NSA_KV_HEADS, 2, HEAD_DIM), new.dtype)], axis=1)

    kc, vc = nsa_compress(full_rows(cache_cmp, kv_cmp_new), *cmp_par)
    kv_sel_f = full_rows(cache_sel, kv_sel_new)
    ws = win_state[j]
    wb = ws.shape[1]
    kv_w = jnp.concatenate([ws, kv_win_new], axis=1)
    o = jax.vmap(nsa_attend, in_axes=(0, None, 0, 0, 0, 0, 0, None, None))(
        q, past + jnp.arange(t), gate, kc, vc, kv_sel_f, kv_w, past - wb + jnp.arange(wb + t), rel_bias)
    return o, kv_w[:, t:]


def setup_inputs(seed: int = 0) -> dict:
    key = jax.random.key(seed)
    ks = iter(jax.random.split(key, 40))
    n_pages = PAST_LEN // PAGE_SIZE
    n_used = DEC_BATCH * n_pages
    n_pool = n_used + max(n_used // 4, 1)
    win_buf = min(WINDOW, PAST_LEN)

    def nrm(shape, scale=1.0):
        return jax.random.normal(next(ks), shape, jnp.float32) * scale

    def gain(shape):
        return 1.0 + nrm(shape, 0.1)

    page_table = jax.random.permutation(next(ks), n_pool)[:n_used].reshape(DEC_BATCH, n_pages).astype(jnp.int32)
    d_in = D_MODEL ** -0.5
    return {
        "x_prompt": nrm((BATCH, SEQ, D_MODEL)),
        "x_sample": nrm((DEC_BATCH, DEC_SEQ, D_MODEL)),
        "cache_moba_kv": nrm((N_MOBA_LAYERS, n_pool, PAGE_SIZE, N_HEADS, 2, HEAD_DIM)),
        "cache_nsa_cmp_kv": nrm((N_NSA_LAYERS, n_pool, PAGE_SIZE, NSA_KV_HEADS, 2, HEAD_DIM)),
        "cache_nsa_sel_kv": nrm((N_NSA_LAYERS, n_pool, PAGE_SIZE, NSA_KV_HEADS, 2, HEAD_DIM)),
        "state_nsa_win_kv": nrm((N_NSA_LAYERS, DEC_BATCH, win_buf, NSA_KV_HEADS, 2, HEAD_DIM)),
        "page_table": page_table,
        "rel_bias": nrm((REL_BUCKETS, N_HEADS), 0.5),
        "attn_norm": gain((DEPTH, D_MODEL)),
        "ffn_norm": gain((DEPTH, D_MODEL)),
        "moba_w_qkv": nrm((N_MOBA_LAYERS, D_MODEL, 3 * D_MODEL), d_in),
        "moba_q_norm": gain((N_MOBA_LAYERS, HEAD_DIM)),
        "moba_k_norm": gain((N_MOBA_LAYERS, HEAD_DIM)),
        "moba_w_o": nrm((N_MOBA_LAYERS, D_MODEL, D_MODEL), d_in),
        "nsa_w_in": nrm((N_NSA_LAYERS, D_MODEL, NSA_IN), d_in),
        "nsa_gate_bias": nrm((N_NSA_LAYERS, N_BRANCH * N_HEADS), 0.1),
        "nsa_q_norm": gain((N_NSA_LAYERS, HEAD_DIM)),
        "nsa_k_norm": gain((N_NSA_LAYERS, N_BRANCH, HEAD_DIM)),
        "nsa_cmp_pos": nrm((N_NSA_LAYERS, CMP_BLOCK, 2, HEAD_DIM), 0.5),
        "nsa_cmp_w1": nrm((N_NSA_LAYERS, 2, CMP_BLOCK, HEAD_DIM, CMP_HIDDEN), (CMP_BLOCK * HEAD_DIM) ** -0.5),
        "nsa_cmp_b1": nrm((N_NSA_LAYERS, 2, CMP_HIDDEN), 0.1),
        "nsa_cmp_w2": nrm((N_NSA_LAYERS, 2, CMP_HIDDEN, HEAD_DIM), CMP_HIDDEN ** -0.5),
        "nsa_w_o": nrm((N_NSA_LAYERS, D_MODEL, D_MODEL), d_in),
        "ffn_w_up": nrm((DEPTH, D_MODEL, D_FF), d_in),
        "ffn_w_down": nrm((DEPTH, D_FF, D_MODEL), D_FF ** -0.5),
    }


def reference(x_prompt, x_sample, cache_moba_kv, cache_nsa_cmp_kv, cache_nsa_sel_kv, state_nsa_win_kv,
              page_table, rel_bias, attn_norm, ffn_norm,
              moba_w_qkv, moba_q_norm, moba_k_norm, moba_w_o,
              nsa_w_in, nsa_gate_bias, nsa_q_norm, nsa_k_norm, nsa_cmp_pos, nsa_cmp_w1, nsa_cmp_b1,
              nsa_cmp_w2, nsa_w_o, ffn_w_up, ffn_w_down):
    hp, hs = x_prompt, x_sample
    moba_p, moba_s, cmp_p, cmp_s, sel_p, sel_s, win_p, win_s = [], [], [], [], [], [], [], []
    for i in range(DEPTH):
        j = i // 2
        xp = rms_norm(hp, attn_norm[i])
        xs = rms_norm(hs, attn_norm[i])
        if i % 2 == 0:
            qp, kvp = moba_project(xp, moba_w_qkv[j], moba_q_norm[j], moba_k_norm[j])
            qs, kvs = moba_project(xs, moba_w_qkv[j], moba_q_norm[j], moba_k_norm[j])
            op = moba_prompt(qp, kvp, rel_bias)
            osm = moba_sample(qs, kvs, cache_moba_kv, j, page_table, rel_bias)
            moba_p.append(kvp)
            moba_s.append(kvs)
            w_o = moba_w_o[j]
        else:
            qp, gp, cp, sp, wp = nsa_project(xp, nsa_w_in[j], nsa_gate_bias[j], nsa_q_norm[j], nsa_k_norm[j])
            qs, gs, cs, ss, ws = nsa_project(xs, nsa_w_in[j], nsa_gate_bias[j], nsa_q_norm[j], nsa_k_norm[j])
            cmp_par = (nsa_cmp_pos[j], nsa_cmp_w1[j], nsa_cmp_b1[j], nsa_cmp_w2[j], nsa_k_norm[j, 0])
            op = nsa_prompt(qp, gp, cp, sp, wp, cmp_par, rel_bias)
            osm, ws_new = nsa_sample(qs, gs, cs, ss, ws, cache_nsa_cmp_kv, cache_nsa_sel_kv, state_nsa_win_kv,
                                     j, page_table, cmp_par, rel_bias)
            cmp_p.append(cp)
            cmp_s.append(cs)
            sel_p.append(sp)
            sel_s.append(ss)
            win_p.append(wp[:, wp.shape[1] - min(WINDOW, wp.shape[1]):])
            win_s.append(ws_new)
            w_o = nsa_w_o[j]
        hp = hp + op.reshape(hp.shape) @ w_o
        hs = hs + osm.reshape(hs.shape) @ w_o
        hp = hp + sqrelu_mlp(rms_norm(hp, ffn_norm[i]), ffn_w_up[i], ffn_w_down[i])
        hs = hs + sqrelu_mlp(rms_norm(hs, ffn_norm[i]), ffn_w_up[i], ffn_w_down[i])
    return (hp, hs, jnp.stack(moba_p), jnp.stack(moba_s), jnp.stack(cmp_p), jnp.stack(cmp_s),
            jnp.stack(sel_p), jnp.stack(sel_s), jnp.stack(win_p), jnp.stack(win_s))
```

```python
import functools
import math

import numpy as np
import jax
import jax.numpy as jnp
from jax import lax
from jax.experimental import pallas as pl
from jax.experimental.pallas import tpu as pltpu

F32 = jnp.float32
BF16 = jnp.bfloat16
I32 = jnp.int32

N_HEADS = 16
HEAD_DIM = 64
NORM_EPS = 1e-6
ATTN_SCALE = HEAD_DIM ** -0.5
REL_BUCKETS = 32
REL_MAX_DIST = 128
PAGE_SIZE = 128
MOBA_BLOCK = 256
MOBA_TOPK = 3
NSA_KV_HEADS = 4
NSA_GROUP = N_HEADS // NSA_KV_HEADS
CMP_BLOCK = 32
CMP_STRIDE = 16
CMP_HIDDEN = 2 * HEAD_DIM
SEL_BLOCK = 64
SEL_TOPK = 16
N_LOCAL_SEL = 2
WINDOW = 512
N_BRANCH = 3

LANES = 128
LANE_SHIFT = 7
SEL_SHIFT = 6
MASK_NEG = -(2.0 ** 100)
VMEM_LIMIT = 56 * 1024 * 1024
NSA_TQ = 128
NSA_TK = 256
WIN_KEYS = WINDOW + NSA_TQ
CMP_NEAR = 32


def _cparams(*sem):
    return pltpu.CompilerParams(dimension_semantics=sem, vmem_limit_bytes=VMEM_LIMIT)


def _nt(a, b):
    return lax.dot_general(a, b, (((1,), (1,)), ((), ())), preferred_element_type=F32)


def _mm(a, b):
    return jnp.dot(a, b, preferred_element_type=F32)


def _split2(x):
    hi = x.astype(BF16)
    lo = (x - hi.astype(F32)).astype(BF16)
    return hi, lo


def _split3(x):
    hi = x.astype(BF16)
    r = x - hi.astype(F32)
    mid = r.astype(BF16)
    lo = (r - mid.astype(F32)).astype(BF16)
    return hi, mid, lo


def _rms_rows(x, g):
    return x * lax.rsqrt(jnp.mean(x * x, axis=-1, keepdims=True) + NORM_EPS) * g


def _group_norm(h, gmat_ref, gain, k_lanes_only):
    hi, lo = _split2(h * h)
    gm = gmat_ref[...]
    ss = _mm(hi, gm) + _mm(lo, gm)
    y = h * lax.rsqrt(ss * (1.0 / HEAD_DIM) + NORM_EPS) * gain
    if k_lanes_only:
        lane = lax.broadcasted_iota(I32, h.shape, 1)
        y = jnp.where((lane & (LANES - 1)) < HEAD_DIM, y, h)
    return y


def _softmax_rows(s):
    m = jnp.max(s, axis=-1, keepdims=True)
    m = jnp.where(m == -jnp.inf, 0.0, m)
    e = jnp.exp(s - m)
    return e / jnp.maximum(jnp.sum(e, axis=-1, keepdims=True), 1e-30)


def _bucket_of_dist():
    n = np.arange(REL_MAX_DIST + 1)
    max_exact = REL_BUCKETS // 2
    nf = np.maximum(n, 1).astype(np.float32)
    large = max_exact + (np.log(nf / np.float32(max_exact)) / np.float32(math.log(REL_MAX_DIST / max_exact))
                         * np.float32(REL_BUCKETS - max_exact)).astype(np.int32)
    large = np.minimum(large, REL_BUCKETS - 1)
    return np.where(n < max_exact, n, large).astype(np.int32)


def _dist_tables(seq, past):
    c = lambda d: np.clip(d, 0, REL_MAX_DIST)
    ncp = seq // CMP_STRIDE
    i256 = np.arange(MOBA_BLOCK)[:, None]
    j256 = np.arange(MOBA_BLOCK)[None, :]
    i128 = np.arange(NSA_TQ)[:, None]
    t = {}
    t["moba"] = np.stack([c(i256 - j256), c(MOBA_BLOCK + i256 - j256),
                          np.full((MOBA_BLOCK, MOBA_BLOCK), REL_MAX_DIST)])
    jk = np.arange(NSA_TK)[None, :]
    t["sel"] = np.stack([c(NSA_TQ * v + i128 - jk) for v in range(4)])
    t["win"] = c(i128 + WINDOW - np.arange(WIN_KEYS)[None, :])
    mcol = np.arange(ncp)[None, :]
    near = i128 - CMP_STRIDE * (mcol - CMP_NEAR // 2) - (CMP_BLOCK - 1)
    t["cmp"] = np.where(mcol < CMP_NEAR, c(near), REL_MAX_DIST)
    r128 = np.arange(LANES)
    t["ms"] = np.stack([c(LANES - r128), np.full(LANES, REL_MAX_DIST), np.zeros(LANES, np.int64)])
    nrow = _cmp_rows(past)
    t["cs"] = c(past - CMP_STRIDE * np.arange(nrow) - (CMP_BLOCK - 1))
    i64 = np.arange(SEL_BLOCK)
    halves = [np.zeros(SEL_BLOCK, np.int64), c(SEL_BLOCK - i64), c(2 * SEL_BLOCK - i64),
              np.full(SEL_BLOCK, REL_MAX_DIST)]
    zero = np.zeros(SEL_BLOCK, np.int64)
    t["sse"] = np.stack([np.concatenate([h, zero]) for h in halves])
    t["sso"] = np.stack([np.concatenate([zero, h]) for h in halves])
    t["ws"] = c(WINDOW - np.arange(WINDOW))
    return t


def _cmp_rows(past):
    return -(-(past // CMP_STRIDE + 4) // 16) * 16


def _tab_kernel(idx_ref, rb_ref, o_ref):
    idx = idx_ref[...]
    b = lax.broadcasted_iota(I32, (REL_BUCKETS, idx.shape[1]), 0)
    oh = jnp.where(b == idx, 1.0, 0.0).astype(BF16)
    o_ref[...] = _mm(rb_ref[0], oh) + _mm(rb_ref[1], oh) + _mm(rb_ref[2], oh)


def _bias_tables(rel_bias, seq, past):
    pats = _dist_tables(seq, past)
    bucket = _bucket_of_dist()
    chunk = 16384
    flat, spans, off = [], {}, 0
    for name, d in pats.items():
        n = d.size
        pad = -n % LANES
        flat.append(bucket[d.reshape(-1)])
        flat.append(np.zeros(pad, np.int32))
        spans[name] = (off, n, d.shape)
        off += n + pad
    total = -(-off // chunk) * chunk
    flat.append(np.zeros(total - off, np.int32))
    idx = jnp.asarray(np.concatenate(flat).astype(np.int32)).reshape(1, total)
    rb3 = jnp.stack(_split3(rel_bias.T.astype(F32)))
    tab = pl.pallas_call(
        _tab_kernel,
        grid=(total // chunk,),
        in_specs=[pl.BlockSpec((1, chunk), lambda i: (0, i)),
                  pl.BlockSpec((3, N_HEADS, REL_BUCKETS), lambda i: (0, 0, 0))],
        out_specs=pl.BlockSpec((N_HEADS, chunk), lambda i: (0, i)),
        out_shape=jax.ShapeDtypeStruct((N_HEADS, total), F32),
        compiler_params=_cparams("parallel"),
        name="bias_tables",
    )(idx, rb3)
    return {name: tab[:, o:o + n].reshape((N_HEADS,) + shp) for name, (o, n, shp) in spans.items()}


def _group_mats():
    i = np.arange(2 * LANES)
    g64 = (i[:, None] // HEAD_DIM == i[None, :] // HEAD_DIM)
    g128 = (i[:, None] // LANES == i[None, :] // LANES) & ((i[:, None] % LANES) < HEAD_DIM)
    return jnp.asarray(g64, BF16), jnp.asarray(g128, BF16)


def _interleave_gain(g, n):
    return jnp.tile(jnp.concatenate([g.astype(F32), jnp.ones((HEAD_DIM,), F32)]), n).reshape(1, n * LANES)


def _proj_moba_kernel(x_ref, an_ref, wq_ref, wkv_ref, qg_ref, kg_ref, g64_ref, g128_ref,
                      q_ref, kv_ref, kvb_ref, *rest, q_wide, n_mean):
    xn = _rms_rows(x_ref[...], an_ref[...]).astype(BF16)
    cw = 2 * LANES
    for c in range(wq_ref.shape[1] // cw):
        sl = slice(c * cw, (c + 1) * cw)
        h = _mm(xn, wq_ref[:, sl])
        q = _group_norm(h, g128_ref if q_wide else g64_ref, qg_ref[:, sl], q_wide)
        q_ref[:, sl] = q.astype(q_ref.dtype)
    for c in range(wkv_ref.shape[1] // cw):
        sl = slice(c * cw, (c + 1) * cw)
        h = _mm(xn, wkv_ref[:, sl])
        kv = _group_norm(h, g128_ref, kg_ref[:, sl], True)
        kv_ref[:, sl] = kv
        kvb_ref[:, sl] = kv.astype(BF16)
        if n_mean:
            km_ref = rest[0]
            for r in range(n_mean):
                km_ref[0, r:r + 1, sl] = jnp.mean(kv[r * MOBA_BLOCK:(r + 1) * MOBA_BLOCK], axis=0, keepdims=True)


def _proj_moba(x, an, w_qkv, q_gain, k_gain, *, q_wide, with_mean):
    m, d = x.shape
    tm = min(m, 512)
    g64, g128 = _group_mats()
    w3 = w_qkv.reshape(d, 3, N_HEADS, HEAD_DIM)
    wkv = jnp.stack([w3[:, 1], w3[:, 2]], axis=2).reshape(d, 2 * d).astype(BF16)
    if q_wide:
        wq = jnp.stack([w3[:, 0], jnp.zeros_like(w3[:, 0])], axis=2).reshape(d, 2 * d).astype(BF16)
        qg = _interleave_gain(q_gain, N_HEADS)
        qdt = F32
    else:
        wq = w3[:, 0].reshape(d, d).astype(BF16)
        qg = jnp.tile(q_gain.astype(F32), N_HEADS).reshape(1, d)
        qdt = BF16
    kg = _interleave_gain(k_gain, N_HEADS)
    qn = wq.shape[1]
    n_mean = tm // MOBA_BLOCK if with_mean else 0
    const = lambda i: (0, 0)
    out_shape = [jax.ShapeDtypeStruct((m, qn), qdt), jax.ShapeDtypeStruct((m, 2 * d), F32),
                 jax.ShapeDtypeStruct((m, 2 * d), BF16)]
    out_specs = [pl.BlockSpec((tm, qn), lambda i: (i, 0)), pl.BlockSpec((tm, 2 * d), lambda i: (i, 0)),
                 pl.BlockSpec((tm, 2 * d), lambda i: (i, 0))]
    if n_mean:
        out_shape.append(jax.ShapeDtypeStruct((m // tm, n_mean, 2 * d), F32))
        out_specs.append(pl.BlockSpec((1, n_mean, 2 * d), lambda i: (i, 0, 0)))
    return pl.pallas_call(
        functools.partial(_proj_moba_kernel, q_wide=q_wide, n_mean=n_mean),
        grid=(m // tm,),
        in_specs=[pl.BlockSpec((tm, d), lambda i: (i, 0)), pl.BlockSpec((1, d), const),
                  pl.BlockSpec((d, qn), const), pl.BlockSpec((d, 2 * d), const),
                  pl.BlockSpec((1, qn), const), pl.BlockSpec((1, 2 * d), const),
                  pl.BlockSpec(g64.shape, const), pl.BlockSpec(g128.shape, const)],
        out_specs=out_specs, out_shape=out_shape,
        compiler_params=_cparams("parallel"),
        name="proj_moba",
    )(x, an.reshape(1, d).astype(F32), wq, wkv, qg, kg, g64, g128)


def _proj_nsa_kernel(x_ref, an_ref, wq_ref, wkv_ref, wg_ref, gb_ref, qg_ref, kg_ref, g64_ref, g128_ref,
                     q_ref, cmp_ref, sel_ref, win_ref, selb_ref, winb_ref, gate_ref, *, q_wide):
    xn = _rms_rows(x_ref[...], an_ref[...]).astype(BF16)
    cw = 2 * LANES
    for c in range(wq_ref.shape[1] // cw):
        sl = slice(c * cw, (c + 1) * cw)
        h = _mm(xn, wq_ref[:, sl])
        q = _group_norm(h, g128_ref if q_wide else g64_ref, qg_ref[:, sl], q_wide)
        q_ref[:, sl] = q.astype(q_ref.dtype)
    per_branch = NSA_KV_HEADS * LANES // cw
    for c in range(wkv_ref.shape[1] // cw):
        sl = slice(c * cw, (c + 1) * cw)
        br, cc = divmod(c, per_branch)
        osl = slice(cc * cw, (cc + 1) * cw)
        h = _mm(xn, wkv_ref[:, sl])
        if br == 0:
            cmp_ref[:, osl] = h
        else:
            kv = _group_norm(h, g128_ref, kg_ref[:, sl], True)
            o32, o16 = (sel_ref, selb_ref) if br == 1 else (win_ref, winb_ref)
            o32[:, osl] = kv
            o16[:, osl] = kv.astype(BF16)
    hg = _mm(xn, wg_ref[...]) + gb_ref[...]
    gate_ref[...] = 1.0 / (1.0 + jnp.exp(-hg))


def _proj_nsa(x, an, w_in, gate_bias, q_gain, k_gain, *, q_wide):
    m, d = x.shape
    tm = min(m, 512)
    g64, g128 = _group_mats()
    kvd = N_BRANCH * NSA_KV_HEADS * LANES
    gw = NSA_KV_HEADS * LANES
    ng = NSA_GROUP * N_BRANCH
    wq = w_in[:, :d]
    if q_wide:
        wq3 = wq.reshape(d, N_HEADS, HEAD_DIM)
        wq = jnp.stack([wq3, jnp.zeros_like(wq3)], axis=2).reshape(d, 2 * d)
        qg = _interleave_gain(q_gain, N_HEADS)
        qdt = F32
    else:
        qg = jnp.tile(q_gain.astype(F32), N_HEADS).reshape(1, d)
        qdt = BF16
    wq = wq.astype(BF16)
    wkv = w_in[:, d:d + kvd].astype(BF16)
    wg = jnp.zeros((d, NSA_KV_HEADS, LANES), F32).at[:, :, :ng].set(
        w_in[:, d + kvd:].reshape(d, NSA_KV_HEADS, ng)).reshape(d, gw).astype(BF16)
    gb = jnp.zeros((NSA_KV_HEADS, LANES), F32).at[:, :ng].set(
        gate_bias.astype(F32).reshape(NSA_KV_HEADS, ng)).reshape(1, gw)
    kg = jnp.concatenate([jnp.ones((1, gw), F32), _interleave_gain(k_gain[1], NSA_KV_HEADS),
                          _interleave_gain(k_gain[2], NSA_KV_HEADS)], axis=1)
    qn = wq.shape[1]
    const = lambda i: (0, 0)
    row = lambda n: pl.BlockSpec((tm, n), lambda i: (i, 0))
    return pl.pallas_call(
        functools.partial(_proj_nsa_kernel, q_wide=q_wide),
        grid=(m // tm,),
        in_specs=[row(d), pl.BlockSpec((1, d), const), pl.BlockSpec((d, qn), const),
                  pl.BlockSpec((d, kvd), const), pl.BlockSpec((d, gw), const), pl.BlockSpec((1, gw), const),
                  pl.BlockSpec((1, qn), const), pl.BlockSpec((1, kvd), const),
                  pl.BlockSpec(g64.shape, const), pl.BlockSpec(g128.shape, const)],
        out_specs=[row(qn), row(gw), row(gw), row(gw), row(gw), row(gw), row(gw)],
        out_shape=[jax.ShapeDtypeStruct((m, qn), qdt)] + [jax.ShapeDtypeStruct((m, gw), F32)] * 3
                  + [jax.ShapeDtypeStruct((m, gw), BF16)] * 2 + [jax.ShapeDtypeStruct((m, gw), F32)],
        compiler_params=_cparams("parallel"),
        name="proj_nsa",
    )(x, an.reshape(1, d).astype(F32), wq, wkv, wg, gb, qg, kg, g64, g128)


def _mlp_kernel(h_ref, o_ref, wo_ref, g_ref, wup_ref, wdn_ref, out_ref, h1_s, xn_s, acc_s):
    f = pl.program_id(1)

    @pl.when(f == 0)
    def _():
        h1 = h_ref[...] + _mm(o_ref[...], wo_ref[...])
        h1_s[...] = h1
        xn_s[...] = _rms_rows(h1, g_ref[...]).astype(BF16)
        acc_s[...] = jnp.zeros_like(acc_s)

    u = _mm(xn_s[...], wup_ref[...])
    u = jnp.square(jnp.maximum(u, 0.0)).astype(BF16)
    acc_s[...] += _mm(u, wdn_ref[...])

    @pl.when(f == pl.num_programs(1) - 1)
    def _():
        out_ref[...] = h1_s[...] + acc_s[...]


def _attn_out_mlp(h, o, w_o, fn, w_up, w_down):
    m, d = h.shape
    dff = w_up.shape[1]
    tm = min(m, 512)
    tf = 512
    return pl.pallas_call(
        _mlp_kernel,
        grid=(m // tm, dff // tf),
        in_specs=[pl.BlockSpec((tm, d), lambda i, f: (i, 0)), pl.BlockSpec((tm, d), lambda i, f: (i, 0)),
                  pl.BlockSpec((d, d), lambda i, f: (0, 0)), pl.BlockSpec((1, d), lambda i, f: (0, 0)),
                  pl.BlockSpec((d, tf), lambda i, f: (0, f)), pl.BlockSpec((tf, d), lambda i, f: (f, 0))],
        out_specs=pl.BlockSpec((tm, d), lambda i, f: (i, 0)),
        out_shape=jax.ShapeDtypeStruct((m, d), F32),
        scratch_shapes=[pltpu.VMEM((tm, d), F32), pltpu.VMEM((tm, d), BF16), pltpu.VMEM((tm, d), F32)],
        compiler_params=_cparams("parallel", "arbitrary"),
        name="attn_out_mlp",
    )(h, o.astype(BF16), w_o.astype(BF16), fn.reshape(1, d).astype(F32), w_up.astype(BF16), w_down.astype(BF16))


def _topk_rows(x, k, on_pick):
    n = x.shape[0]
    row = lax.broadcasted_iota(I32, x.shape, 0)
    for _ in range(k):
        m = jnp.max(x, axis=0, keepdims=True)
        idx = jnp.min(jnp.where(x == m, row, n), axis=0, keepdims=True)
        hit = row == idx
        on_pick(hit, m)
        x = jnp.where(hit, -jnp.inf, x)


def _moba_prompt_kernel(q_ref, kv_ref, km_ref, tb_ref, eye_ref, o_ref, *, nb):
    cur = pl.program_id(2)
    tq = MOBA_BLOCK
    lane = lax.broadcasted_iota(I32, (tq, LANES), 1)
    q2 = q_ref[...].astype(F32) * ATTN_SCALE
    causal = lax.broadcasted_iota(I32, (tq, tq), 1) <= lax.broadcasted_iota(I32, (tq, tq), 0)
    rown = lax.broadcasted_iota(I32, (nb, tq), 0)
    outs = []
    for hh in range(2):
        lanes = slice(hh * LANES, (hh + 1) * LANES)
        qz = jnp.where(lane < HEAD_DIM, q2 if hh == 0 else pltpu.roll(q2, HEAD_DIM, 1), 0.0).astype(BF16)
        km_hi, km_lo = _split2(km_ref[0, :, lanes])
        gate = _nt(km_hi, qz) + _nt(km_lo, qz)
        state = [jnp.full((nb, tq), MASK_NEG, F32)]

        def pick(hit, m, state=state):
            state[0] = jnp.where(hit, jnp.where(m > -jnp.inf, 0.0, state[0]), state[0])

        _topk_rows(jnp.where(rown < cur, gate, -jnp.inf), MOBA_TOPK, pick)
        negt = jnp.concatenate([state[0], jnp.zeros((LANES - nb, tq), F32)], axis=0).astype(BF16)
        negsel = _nt(eye_ref[...], negt).astype(BF16)
        qaug = jnp.concatenate([qz, negsel], axis=1)

        kv_own = kv_ref[0, pl.ds(pl.multiple_of(cur * tq, tq), tq), lanes]
        s = jnp.where(causal, _nt(qz, kv_own) + tb_ref[hh, 0], -jnp.inf)
        m0 = jnp.max(s, axis=1, keepdims=True)
        p = jnp.exp(s - m0)
        l0 = jnp.sum(p, axis=1, keepdims=True)
        acc0 = _mm(p.astype(BF16), kv_own)

        def body(n, carry, hh=hh, lanes=lanes, qaug=qaug):
            m, l, acc = carry
            kvn = kv_ref[0, pl.ds(pl.multiple_of(n * tq, tq), tq), lanes]
            onehot = jnp.where(lane == n, 1.0, 0.0).astype(BF16)
            s = _nt(qaug, jnp.concatenate([kvn, onehot], axis=1)) + tb_ref[hh, jnp.where(cur - n == 1, 1, 2)]
            mn = jnp.maximum(m, jnp.max(s, axis=1, keepdims=True))
            a = jnp.exp(m - mn)
            p = jnp.exp(s - mn)
            return mn, a * l + jnp.sum(p, axis=1, keepdims=True), a * acc + _mm(p.astype(BF16), kvn)

        _, l, acc = lax.fori_loop(0, cur, body, (m0, l0, acc0))
        outs.append(acc / jnp.maximum(l, 1e-30))
    o_ref[...] = jnp.where(lane < HEAD_DIM, pltpu.roll(outs[0], HEAD_DIM, 1), outs[1]).astype(o_ref.dtype)


def _moba_prompt(q, kvb, kmean, tb, b, s):
    d = q.shape[1]
    nb = s // MOBA_BLOCK
    assert s % MOBA_BLOCK == 0 and MOBA_TOPK <= nb <= LANES
    nq = s // MOBA_BLOCK
    eye = jnp.asarray(np.eye(MOBA_BLOCK), BF16)
    return pl.pallas_call(
        functools.partial(_moba_prompt_kernel, nb=nb),
        grid=(b, N_HEADS // 2, nq),
        in_specs=[pl.BlockSpec((MOBA_BLOCK, LANES), lambda i, h, t: (i * nq + t, h)),
                  pl.BlockSpec((1, s, 2 * LANES), lambda i, h, t: (i, 0, h)),
                  pl.BlockSpec((1, nb, 2 * LANES), lambda i, h, t: (i, 0, h)),
                  pl.BlockSpec((2, 3, MOBA_BLOCK, MOBA_BLOCK), lambda i, h, t: (h, 0, 0, 0)),
                  pl.BlockSpec(eye.shape, lambda i, h, t: (0, 0))],
        out_specs=pl.BlockSpec((MOBA_BLOCK, LANES), lambda i, h, t: (i * nq + t, h)),
        out_shape=jax.ShapeDtypeStruct((b * s, d), BF16),
        compiler_params=_cparams("parallel", "parallel", "arbitrary"),
        name="moba_prompt",
    )(q, kvb.reshape(b, s, 2 * d), kmean.reshape(b, nb, 2 * d), tb, eye)


def _gelu_tanh(x):
    return 0.5 * x * (1.0 + jnp.tanh(math.sqrt(2.0 / math.pi) * (x + 0.044715 * (x * x * x))))


def _compress_weights(cmp_pos, w1, b1, w2, k_gain0):
    half = CMP_BLOCK // 2
    z = jnp.zeros((half, HEAD_DIM, CMP_HIDDEN), F32)

    def first_layer(lo):
        wk, wv = w1[0, lo:lo + half], w1[1, lo:lo + half]
        top = jnp.concatenate([wk, z], axis=2)
        bot = jnp.concatenate([z, wv], axis=2)
        return jnp.concatenate([top, bot], axis=1).reshape(half * LANES, 2 * CMP_HIDDEN).astype(BF16)

    zz = jnp.zeros((CMP_HIDDEN, HEAD_DIM), F32)
    w2bd = jnp.concatenate([jnp.concatenate([w2[0], zz], axis=1),
                            jnp.concatenate([zz, w2[1]], axis=1)], axis=0).astype(BF16)
    pos_a = cmp_pos[:half].reshape(half, LANES).astype(F32)
    pos_b = cmp_pos[half:].reshape(half, LANES).astype(F32)
    return (pos_a, pos_b, first_layer(0), first_layer(half), b1.reshape(1, 2 * CMP_HIDDEN).astype(F32), w2bd,
            _interleave_gain(k_gain0, 1))


def _compress_tail(xa_s, xb_s, wa_ref, wb_ref, b1_ref, w2_ref, kg_ref):
    ha = _mm(xa_s[...], wa_ref[...])
    hb = _mm(xb_s[...], wb_ref[...])
    rows = ha.shape[0]
    h = ha + pltpu.roll(hb, rows - 1, 0) + b1_ref[...]
    out = _mm(_gelu_tanh(h).astype(BF16), w2_ref[...])
    lane = lax.broadcasted_iota(I32, out.shape, 1)
    is_k = lane < HEAD_DIM
    ss = jnp.sum(jnp.where(is_k, out * out, 0.0), axis=1, keepdims=True)
    kn = out * lax.rsqrt(ss * (1.0 / HEAD_DIM) + NORM_EPS) * kg_ref[...]
    return jnp.where(is_k, kn, out)


def _compress_prompt_kernel(*refs, nseg):
    x_refs = refs[:NSA_KV_HEADS]
    pa_ref, pb_ref, wa_ref, wb_ref, b1_ref, w2_ref, kg_ref, o_ref, xa_s, xb_s = refs[NSA_KV_HEADS:]
    half = CMP_BLOCK // 2
    for g in range(NSA_KV_HEADS):
        for p in range(half):
            v = x_refs[g][0, pl.ds(p, nseg, stride=CMP_STRIDE), :]
            xa_s[g * nseg:(g + 1) * nseg, p * LANES:(p + 1) * LANES] = (v + pa_ref[p:p + 1, :]).astype(BF16)
            xb_s[g * nseg:(g + 1) * nseg, p * LANES:(p + 1) * LANES] = (v + pb_ref[p:p + 1, :]).astype(BF16)
    res = _compress_tail(xa_s, xb_s, wa_ref, wb_ref, b1_ref, w2_ref, kg_ref)
    for g in range(NSA_KV_HEADS):
        o_ref[0, g] = res[g * nseg:(g + 1) * nseg]


def _compress_prompt(kv_cmp, cw, b, s):
    nseg = s // CMP_STRIDE
    gw = NSA_KV_HEADS * LANES
    kdim = (CMP_BLOCK // 2) * LANES
    const = lambda i: (0, 0)
    return pl.pallas_call(
        functools.partial(_compress_prompt_kernel, nseg=nseg),
        grid=(b,),
        in_specs=[pl.BlockSpec((1, s, LANES), functools.partial(lambda i, g: (i, 0, g), g=g))
                  for g in range(NSA_KV_HEADS)] + [pl.BlockSpec(w.shape, const) for w in cw],
        out_specs=pl.BlockSpec((1, NSA_KV_HEADS, nseg, LANES), lambda i: (i, 0, 0, 0)),
        out_shape=jax.ShapeDtypeStruct((b, NSA_KV_HEADS, nseg, LANES), F32),
        scratch_shapes=[pltpu.VMEM((NSA_KV_HEADS * nseg, kdim), BF16)] * 2,
        compiler_params=_cparams("parallel"),
        name="compress_prompt",
    )(*([kv_cmp.reshape(b, s, gw)] * NSA_KV_HEADS), *cw)


def _compress_sample_kernel(pt_ref, *refs, nseg, nrow):
    x_refs = refs[:2 * NSA_KV_HEADS]
    new_ref, pa_ref, pb_ref, wa_ref, wb_ref, b1_ref, w2_ref, kg_ref, o_ref, xa_s, xb_s = refs[2 * NSA_KV_HEADS:]
    j = pl.program_id(1)
    half = CMP_BLOCK // 2
    per_page = PAGE_SIZE // CMP_STRIDE
    tail = nrow - nseg

    @pl.when(j == 0)
    def _():
        first = lax.broadcasted_iota(I32, (tail, LANES), 0) == 0
        for g in range(NSA_KV_HEADS):
            rows = slice(g * nrow + nseg, (g + 1) * nrow)
            for p in range(half):
                v = jnp.zeros((tail, LANES), F32)
                if p == 0:
                    v = jnp.where(first, new_ref[0, :, g * LANES:(g + 1) * LANES], 0.0)
                xa_s[rows, p * LANES:(p + 1) * LANES] = (v + pa_ref[p:p + 1, :]).astype(BF16)
                xb_s[rows, p * LANES:(p + 1) * LANES] = (v + pb_ref[p:p + 1, :]).astype(BF16)

    for g in range(NSA_KV_HEADS):
        r0 = pl.multiple_of(g * nrow + j * (2 * per_page), 2 * per_page)
        for p in range(half):
            v = jnp.concatenate([xr[0, pl.ds(p, per_page, stride=CMP_STRIDE), :]
                                 for xr in x_refs[2 * g:2 * g + 2]], axis=0)
            xa_s[pl.ds(r0, 2 * per_page), p * LANES:(p + 1) * LANES] = (v + pa_ref[p:p + 1, :]).astype(BF16)
            xb_s[pl.ds(r0, 2 * per_page), p * LANES:(p + 1) * LANES] = (v + pb_ref[p:p + 1, :]).astype(BF16)

    @pl.when(j == pl.num_programs(1) - 1)
    def _():
        res = _compress_tail(xa_s, xb_s, wa_ref, wb_ref, b1_ref, w2_ref, kg_ref)
        for g in range(NSA_KV_HEADS):
            o_ref[0, g] = res[g * nrow:(g + 1) * nrow]


def _compress_sample(cache, page_table, new_rows, cw, past):
    db, n_pages = page_table.shape
    assert n_pages % 2 == 0
    pool = cache.shape[0]
    gw = NSA_KV_HEADS * LANES
    nseg = past // CMP_STRIDE
    nrow = _cmp_rows(past)
    kdim = (CMP_BLOCK // 2) * LANES
    const = lambda i, j, pt: (0, 0)
    grid_spec = pltpu.PrefetchScalarGridSpec(
        num_scalar_prefetch=1,
        grid=(db, n_pages // 2),
        in_specs=[pl.BlockSpec((1, PAGE_SIZE, LANES),
                               functools.partial(lambda i, j, pt, g, pg: (pt[i * n_pages + 2 * j + pg], 0, g),
                                                 g=g, pg=pg))
                  for g in range(NSA_KV_HEADS) for pg in range(2)]
                 + [pl.BlockSpec((1, 1, gw), lambda i, j, pt: (i, 0, 0))]
                 + [pl.BlockSpec(w.shape, const) for w in cw],
        out_specs=pl.BlockSpec((1, NSA_KV_HEADS, nrow, LANES), lambda i, j, pt: (i, 0, 0, 0)),
        scratch_shapes=[pltpu.VMEM((NSA_KV_HEADS * nrow, kdim), BF16)] * 2,
    )
    cview = cache.reshape(pool, PAGE_SIZE, gw)
    return pl.pallas_call(
        functools.partial(_compress_sample_kernel, nseg=nseg, nrow=nrow),
        grid_spec=grid_spec,
        out_shape=jax.ShapeDtypeStruct((db, NSA_KV_HEADS, nrow, LANES), F32),
        compiler_params=_cparams("parallel", "arbitrary"),
        name="compress_sample",
    )(page_table.reshape(-1), *([cview] * (2 * NSA_KV_HEADS)), new_rows.reshape(db, 1, gw), *cw)


def _imp_matrix(n_sel, n_rows, n_cmp):
    ratio = SEL_BLOCK // CMP_STRIDE
    lead = CMP_BLOCK // CMP_STRIDE - 1
    j = np.arange(n_sel)[:, None]
    n = np.arange(n_rows)[None, :]
    return ((n >= ratio * j - lead) & (n <= ratio * j + ratio - 1) & (n < n_cmp))


def _stack_heads(q_f32):
    tq = q_f32.shape[0]
    lane = lax.broadcasted_iota(I32, (tq, LANES), 1)
    parts = []
    for p in range(NSA_GROUP):
        blk = q_f32[:, (p // 2) * LANES:(p // 2 + 1) * LANES]
        if p % 2:
            blk = pltpu.roll(blk, HEAD_DIM, 1)
        parts.append(jnp.where(lane < HEAD_DIM, blk, 0.0))
    return jnp.concatenate(parts, axis=0)


def _nsa_prompt_kernel(q_ref, gate_ref, kc_ref, ks_ref, kw_ref, tsel_ref, tw_ref, tc_ref, at_ref, eye_ref,
                       o_ref, *, n_sel, ncp):
    t = pl.program_id(2)
    tq, tk, grp = NSA_TQ, NSA_TK, NSA_GROUP
    rows = grp * tq
    p0 = t * tq
    qs = _stack_heads(q_ref[...].astype(F32) * ATTN_SCALE).astype(BF16)
    qi = lax.broadcasted_iota(I32, (tq, 1), 0)

    kc = kc_ref[0, 0].astype(BF16)
    shift = lax.rem(t * (tq // CMP_STRIDE) - CMP_NEAR // 2 + ncp, ncp)
    bias_c = jnp.concatenate([pltpu.roll(tc_ref[p], shift, 1) for p in range(grp)], axis=0)
    c_end = lax.broadcasted_iota(I32, (tq, ncp), 1) * CMP_STRIDE + (CMP_BLOCK - 1)
    ok_c = (p0 + qi) >= c_end
    ok_c = jnp.concatenate([ok_c] * grp, axis=0)
    p_c = _softmax_rows(jnp.where(ok_c, _nt(qs, kc) + bias_c, -jnp.inf))
    o_c = _mm(p_c.astype(BF16), kc)
    p_sum = p_c[0:tq]
    for p in range(1, grp):
        p_sum = p_sum + p_c[p * tq:(p + 1) * tq]
    at = at_ref[...]
    imp = sum(_nt(at, part) for part in _split3(p_sum))

    blk = lax.broadcasted_iota(I32, (n_sel, tq), 0)
    cur = jnp.right_shift(p0 + lax.broadcasted_iota(I32, (n_sel, tq), 1), SEL_SHIFT)
    valid = blk <= cur
    forced = (blk == 0) | (blk >= cur - (N_LOCAL_SEL - 1))
    score = jnp.where(valid, jnp.where(forced, jnp.inf, imp), -jnp.inf)
    state = [jnp.full((n_sel, tq), MASK_NEG, F32)]

    def pick(hit, m):
        state[0] = jnp.where(hit, jnp.where(m > -jnp.inf, 0.0, state[0]), state[0])

    _topk_rows(score, SEL_TOPK, pick)
    negt = state[0]
    if n_sel < LANES:
        negt = jnp.concatenate([negt, jnp.zeros((LANES - n_sel, tq), F32)], axis=0)
    negsel = _nt(eye_ref[...], negt.astype(BF16)).astype(BF16)
    qaug = jnp.concatenate([qs, jnp.concatenate([negsel] * grp, axis=0)], axis=1)

    kj = lax.broadcasted_iota(I32, (tk, LANES), 0)
    kl = lax.broadcasted_iota(I32, (tk, LANES), 1)

    def sel_scores(kt, variant):
        kv = ks_ref[0, pl.ds(pl.multiple_of(kt * tk, tk), tk), :]
        onehot = jnp.where(kl == kt * (tk // SEL_BLOCK) + jnp.right_shift(kj, SEL_SHIFT), 1.0, 0.0).astype(BF16)
        bias = tsel_ref[0, variant].reshape(rows, tk)
        return _nt(qaug, jnp.concatenate([kv, onehot], axis=1)) + bias, kv

    kt_d = t // (tk // tq)
    off_d = lax.rem(t, tk // tq) * tq
    s, kv = sel_scores(kt_d, off_d // tq)
    col = lax.broadcasted_iota(I32, (tq, tk), 1)
    ok_d = jnp.concatenate([col <= qi + off_d] * grp, axis=0)
    s = jnp.where(ok_d, s, -jnp.inf)
    m0 = jnp.max(s, axis=1, keepdims=True)
    pr = jnp.exp(s - m0)
    l0 = jnp.sum(pr, axis=1, keepdims=True)
    acc0 = _mm(pr.astype(BF16), kv)

    def body(kt, carry):
        m, l, acc = carry
        s, kv = sel_scores(kt, jnp.minimum((p0 - kt * tk) // tq, 3))
        mn = jnp.maximum(m, jnp.max(s, axis=1, keepdims=True))
        a = jnp.exp(m - mn)
        pr = jnp.exp(s - mn)
        return mn, a * l + jnp.sum(pr, axis=1, keepdims=True), a * acc + _mm(pr.astype(BF16), kv)

    _, l, acc = lax.fori_loop(0, kt_d, body, (m0, l0, acc0))
    o_s = acc / jnp.maximum(l, 1e-30)

    kw = kw_ref[0, pl.ds(pl.multiple_of(p0, tq), WIN_KEYS), :]
    wj = lax.broadcasted_iota(I32, (tq, WIN_KEYS), 1)
    ok_w = (wj > qi) & (wj <= qi + WINDOW) & (p0 + wj >= WINDOW)
    ok_w = jnp.concatenate([ok_w] * grp, axis=0)
    s_w = _nt(qs, kw) + tw_ref[...].reshape(rows, WIN_KEYS)
    p_w = _softmax_rows(jnp.where(ok_w, s_w, -jnp.inf))
    o_w = _mm(p_w.astype(BF16), kw)

    gate = gate_ref[...]
    lane = lax.broadcasted_iota(I32, (tq, LANES), 1)
    per_head = []
    for p in range(grp):
        r = slice(p * tq, (p + 1) * tq)
        g = [gate[:, N_BRANCH * p + k:N_BRANCH * p + k + 1] for k in range(N_BRANCH)]
        per_head.append(g[0] * o_c[r] + g[1] * o_s[r] + g[2] * o_w[r])
    halves = [jnp.where(lane < HEAD_DIM, pltpu.roll(per_head[2 * c], HEAD_DIM, 1), per_head[2 * c + 1])
              for c in range(grp // 2)]
    o_ref[...] = jnp.concatenate(halves, axis=1).astype(o_ref.dtype)


def _nsa_prompt(q, gate, kcvc, selb, winb, tabs, b, s):
    d = q.shape[1]
    gw = NSA_KV_HEADS * LANES
    ncp = s // CMP_STRIDE
    n_sel = s // SEL_BLOCK
    assert s % NSA_TK == 0 and SEL_TOPK <= n_sel <= LANES and ncp >= CMP_NEAR
    nq = s // NSA_TQ
    at = jnp.asarray(_imp_matrix(n_sel, ncp, ncp - 1), BF16)
    eye = jnp.asarray(np.eye(NSA_TQ), BF16)
    winp = jnp.pad(winb.reshape(b, s, gw), ((0, 0), (WINDOW, 0), (0, 0)))
    gq = NSA_GROUP
    tsel = tabs["sel"].reshape(NSA_KV_HEADS, gq, 4, NSA_TQ, NSA_TK).transpose(0, 2, 1, 3, 4)
    return pl.pallas_call(
        functools.partial(_nsa_prompt_kernel, n_sel=n_sel, ncp=ncp),
        grid=(b, NSA_KV_HEADS, nq),
        in_specs=[pl.BlockSpec((NSA_TQ, gq * HEAD_DIM), lambda i, g, t: (i * nq + t, g)),
                  pl.BlockSpec((NSA_TQ, LANES), lambda i, g, t: (i * nq + t, g)),
                  pl.BlockSpec((1, 1, ncp, LANES), lambda i, g, t: (i, g, 0, 0)),
                  pl.BlockSpec((1, s, LANES), lambda i, g, t: (i, 0, g)),
                  pl.BlockSpec((1, s + WINDOW, LANES), lambda i, g, t: (i, 0, g)),
                  pl.BlockSpec((1, 4, gq, NSA_TQ, NSA_TK), lambda i, g, t: (g, 0, 0, 0, 0)),
                  pl.BlockSpec((gq, NSA_TQ, WIN_KEYS), lambda i, g, t: (g, 0, 0)),
                  pl.BlockSpec((gq, NSA_TQ, ncp), lambda i, g, t: (g, 0, 0)),
                  pl.BlockSpec(at.shape, lambda i, g, t: (0, 0)),
                  pl.BlockSpec(eye.shape, lambda i, g, t: (0, 0))],
        out_specs=pl.BlockSpec((NSA_TQ, gq * HEAD_DIM), lambda i, g, t: (i * nq + t, g)),
        out_shape=jax.ShapeDtypeStruct((b * s, d), BF16),
        compiler_params=_cparams("parallel", "parallel", "arbitrary"),
        name="nsa_prompt",
    )(q, gate, kcvc, selb.reshape(b, s, gw), winp, tsel, tabs["win"], tabs["cmp"], at, eye)


def _topk_lanes(x, k):
    lane = lax.broadcasted_iota(I32, x.shape, 1)
    out_lane = lax.broadcasted_iota(I32, (x.shape[0], LANES), 1)
    ids = jnp.zeros((x.shape[0], LANES), I32)
    for r in range(k):
        m = jnp.max(x, axis=1, keepdims=True)
        idx = jnp.min(jnp.where(x == m, lane, x.shape[1]), axis=1, keepdims=True)
        ids = jnp.where(out_lane == r, idx, ids)
        x = jnp.where(lane == idx, -jnp.inf, x)
    return ids


def _moba_select_kernel(pt_ref, x_ref, q_ref, ids_ref, ksum_s, *, pages_per_block):
    j = pl.program_id(1)

    @pl.when(j == 0)
    def _():
        ksum_s[...] = jnp.zeros_like(ksum_s)

    blk = j // pages_per_block
    ksum_s[pl.ds(blk, 1), :] += jnp.sum(x_ref[0], axis=0, keepdims=True)

    @pl.when(j == pl.num_programs(1) - 1)
    def _():
        km = ksum_s[...] * (1.0 / MOBA_BLOCK)
        hrow = lax.broadcasted_iota(I32, (N_HEADS, km.shape[1]), 0)
        hl = jnp.right_shift(lax.broadcasted_iota(I32, (N_HEADS, km.shape[1]), 1), LANE_SHIFT)
        qm = jnp.where(hrow == hl, q_ref[0], 0.0)
        gate = sum(_nt(a, b_) for a in _split3(qm) for b_ in _split3(km))
        ids_ref[0] = _topk_lanes(gate, MOBA_TOPK)


def _moba_select(cache, page_table, q_wide, past):
    db, n_pages = page_table.shape
    pool = cache.shape[0]
    w = N_HEADS * LANES
    ppb = MOBA_BLOCK // PAGE_SIZE
    nb = past // MOBA_BLOCK
    assert past % MOBA_BLOCK == 0 and nb >= MOBA_TOPK
    grid_spec = pltpu.PrefetchScalarGridSpec(
        num_scalar_prefetch=1,
        grid=(db, n_pages),
        in_specs=[pl.BlockSpec((1, PAGE_SIZE, w), lambda i, j, pt: (pt[i * n_pages + j], 0, 0)),
                  pl.BlockSpec((1, 1, w), lambda i, j, pt: (i, 0, 0))],
        out_specs=pl.BlockSpec((1, N_HEADS, LANES), lambda i, j, pt: (i, 0, 0)),
        scratch_shapes=[pltpu.VMEM((nb, w), F32)],
    )
    return pl.pallas_call(
        functools.partial(_moba_select_kernel, pages_per_block=ppb),
        grid_spec=grid_spec,
        out_shape=jax.ShapeDtypeStruct((db, N_HEADS, LANES), I32),
        compiler_params=_cparams("parallel", "arbitrary"),
        name="moba_select",
    )(page_table.reshape(-1), cache.reshape(pool, PAGE_SIZE, w), q_wide.reshape(db, 1, w))


def _pad_rows(x, rows):
    return jnp.concatenate([x, jnp.zeros((rows - x.shape[0],) + x.shape[1:], x.dtype)], axis=0)


def _moba_sample_kernel(pt_ref, ids_ref, q_ref, new_ref, *rest, n_tiles, last_blk, pages_per_block):
    tiles = rest[:n_tiles]
    tab_ref, o_ref = rest[n_tiles], rest[n_tiles + 1]
    i, h = pl.program_id(0), pl.program_id(1)
    q8 = _pad_rows(q_ref[0, 0] * ATTN_SCALE, 8)
    q8b = q8.astype(BF16)
    kvs = jnp.concatenate([r[0] for r in tiles], axis=0)
    kvb = kvs.astype(BF16)
    pieces = []
    for ti in range(n_tiles):
        k, pp = divmod(ti, pages_per_block)
        blk = ids_ref[(i * N_HEADS + h) * MOBA_TOPK + k]
        near = jnp.logical_and(blk == last_blk, pp == pages_per_block - 1)
        pieces.append(jnp.where(near, tab_ref[0, 0:1, :], tab_ref[0, 1:2, :]))
    s = _nt(q8b, kvb) + jnp.concatenate(pieces, axis=1)
    new = new_ref[0, 0]
    s_self = jnp.sum(q8 * new, axis=1, keepdims=True) + tab_ref[0, 2:3, 0:1]
    m = jnp.maximum(jnp.max(s, axis=1, keepdims=True), s_self)
    p = jnp.exp(s - m)
    p_self = jnp.exp(s_self - m)
    l = jnp.sum(p, axis=1, keepdims=True) + p_self
    o_ref[0, 0] = (_mm(p.astype(BF16), kvb) + p_self * new) / jnp.maximum(l, 1e-30)


def _moba_sample(cache, page_table, ids, q_wide, kv_new, tab_ms, past):
    db, n_pages = page_table.shape
    pool = cache.shape[0]
    w = N_HEADS * LANES
    ppb = MOBA_BLOCK // PAGE_SIZE
    n_tiles = MOBA_TOPK * ppb

    def tile_spec(ti):
        k, pp = divmod(ti, ppb)
        return pl.BlockSpec(
            (1, PAGE_SIZE, LANES),
            lambda i, h, pt, sel: (pt[i * n_pages + sel[(i * N_HEADS + h) * MOBA_TOPK + k] * ppb + pp], 0, h))

    per_head = pl.BlockSpec((1, 1, 1, LANES), lambda i, h, pt, sel: (i, h, 0, 0))
    grid_spec = pltpu.PrefetchScalarGridSpec(
        num_scalar_prefetch=2,
        grid=(db, N_HEADS),
        in_specs=[per_head, per_head] + [tile_spec(ti) for ti in range(n_tiles)]
                 + [pl.BlockSpec((1, 3, LANES), lambda i, h, pt, sel: (h, 0, 0))],
        out_specs=pl.BlockSpec((1, 1, 8, LANES), lambda i, h, pt, sel: (i, h, 0, 0)),
    )
    cview = cache.reshape(pool, PAGE_SIZE, w)
    out = pl.pallas_call(
        functools.partial(_moba_sample_kernel, n_tiles=n_tiles, last_blk=past // MOBA_BLOCK - 1,
                          pages_per_block=ppb),
        grid_spec=grid_spec,
        out_shape=jax.ShapeDtypeStruct((db, N_HEADS, 8, LANES), F32),
        compiler_params=_cparams("parallel", "arbitrary"),
        name="moba_sample",
    )(page_table.reshape(-1), ids[:, :, :MOBA_TOPK].reshape(-1), q_wide.reshape(db, N_HEADS, 1, LANES),
      kv_new.reshape(db, N_HEADS, 1, LANES), *([cview] * n_tiles), tab_ms)
    return out[:, :, 0, HEAD_DIM:].reshape(db, N_HEADS * HEAD_DIM)


def _nsa_select_kernel(q_ref, kc_ref, tc_ref, a_ref, oc_ref, ids_ref, *, n_valid, n_sel, cur):
    nrow = kc_ref.shape[2]
    lane = lax.broadcasted_iota(I32, (8, nrow), 1)
    psums = []
    for g in range(NSA_KV_HEADS):
        q8 = (q_ref[0, g] * ATTN_SCALE).astype(BF16)
        kc = kc_ref[0, g].astype(BF16)
        s = jnp.where(lane < n_valid, _nt(q8, kc) + tc_ref[g], -jnp.inf)
        p_c = _softmax_rows(s)
        oc_ref[0, g] = _mm(p_c.astype(BF16), kc)
        psums.append(jnp.sum(p_c[0:NSA_GROUP], axis=0, keepdims=True))
    p_sum = _pad_rows(jnp.concatenate(psums, axis=0), 8)
    a = a_ref[...]
    imp = sum(_mm(part, a) for part in _split3(p_sum))
    blk = lax.broadcasted_iota(I32, imp.shape, 1)
    valid = blk <= min(cur, n_sel - 1)
    forced = (blk == 0) | (blk >= cur - (N_LOCAL_SEL - 1))
    score = jnp.where(valid, jnp.where(forced, jnp.inf, imp), -jnp.inf)
    ids_ref[0] = _topk_lanes(score, SEL_TOPK)


def _nsa_select(q4, kcvc, tab_cs, past):
    db = q4.shape[0]
    nrow = kcvc.shape[2]
    n_cmp_valid = (past - (CMP_BLOCK - 1)) // CMP_STRIDE + 1
    n_sel = max(-(-(past + 1) // SEL_BLOCK), SEL_TOPK)
    n_cmp = n_sel * SEL_BLOCK // CMP_STRIDE - CMP_BLOCK // CMP_STRIDE + 1
    cur = past // SEL_BLOCK
    assert n_sel <= 2 * LANES and cur >= SEL_TOPK - 1 and n_cmp <= nrow
    a = jnp.asarray(_imp_matrix(2 * LANES, nrow, n_cmp).T, BF16)
    tc = jnp.concatenate([tab_cs.reshape(NSA_KV_HEADS, NSA_GROUP, nrow)] * 2, axis=1)
    return pl.pallas_call(
        functools.partial(_nsa_select_kernel, n_valid=n_cmp_valid, n_sel=n_sel, cur=cur),
        grid=(db,),
        in_specs=[pl.BlockSpec((1, NSA_KV_HEADS, 8, LANES), lambda i: (i, 0, 0, 0)),
                  pl.BlockSpec((1, NSA_KV_HEADS, nrow, LANES), lambda i: (i, 0, 0, 0)),
                  pl.BlockSpec(tc.shape, lambda i: (0, 0, 0)),
                  pl.BlockSpec(a.shape, lambda i: (0, 0))],
        out_specs=[pl.BlockSpec((1, NSA_KV_HEADS, 8, LANES), lambda i: (i, 0, 0, 0)),
                   pl.BlockSpec((1, 8, LANES), lambda i: (i, 0, 0))],
        out_shape=[jax.ShapeDtypeStruct((db, NSA_KV_HEADS, 8, LANES), F32),
                   jax.ShapeDtypeStruct((db, 8, LANES), I32)],
        compiler_params=_cparams("parallel"),
        name="nsa_select",
    )(q4, kcvc, tc, a)


def _nsa_sample_kernel(pt_ref, ids_ref, q_ref, oc_ref, gate_ref, snew_ref, wnew_ref, st_ref, *rest,
                       n_cache_blk):
    tiles = rest[:SEL_TOPK]
    tse_ref, tso_ref, tws_ref, t0_ref, o_ref, st_out = rest[SEL_TOPK:]
    i, g = pl.program_id(0), pl.program_id(1)
    q8 = q_ref[0, 0] * ATTN_SCALE
    q8b = q8.astype(BF16)
    lane = lax.broadcasted_iota(I32, (8, LANES), 1)
    t0 = t0_ref[0][:, 0:1]

    kvs = jnp.concatenate([r[0] for r in tiles], axis=0)
    kvb = kvs.astype(BF16)
    s = _nt(q8b, kvb)
    pieces = []
    for c in range(SEL_TOPK // 2):
        be = ids_ref[(i * NSA_KV_HEADS + g) * SEL_TOPK + 2 * c]
        bo = ids_ref[(i * NSA_KV_HEADS + g) * SEL_TOPK + 2 * c + 1]
        bias = jnp.where(lane < HEAD_DIM, tse_ref[0, jnp.clip(n_cache_blk - be, 0, 3)],
                         tso_ref[0, jnp.clip(n_cache_blk - bo, 0, 3)])
        ok = jnp.where(lane < HEAD_DIM, be, bo) < n_cache_blk
        pieces.append(jnp.where(ok, s[:, c * LANES:(c + 1) * LANES] + bias, -jnp.inf))
    s = jnp.concatenate(pieces, axis=1)
    snew = snew_ref[0]
    s_self = jnp.sum(q8 * snew, axis=1, keepdims=True) + t0
    m = jnp.maximum(jnp.max(s, axis=1, keepdims=True), s_self)
    p = jnp.exp(s - m)
    p_self = jnp.exp(s_self - m)
    l = jnp.sum(p, axis=1, keepdims=True) + p_self
    o_s = (_mm(p.astype(BF16), kvb) + p_self * snew) / jnp.maximum(l, 1e-30)

    ws = st_ref[0]
    wsb = ws.astype(BF16)
    wl = lax.broadcasted_iota(I32, (8, WINDOW), 1)
    s_w = jnp.where(wl >= 1, _nt(q8b, wsb) + tws_ref[0], -jnp.inf)
    wnew = wnew_ref[0]
    w_self = jnp.sum(q8 * wnew, axis=1, keepdims=True) + t0
    m = jnp.maximum(jnp.max(s_w, axis=1, keepdims=True), w_self)
    p = jnp.exp(s_w - m)
    p_self = jnp.exp(w_self - m)
    l = jnp.sum(p, axis=1, keepdims=True) + p_self
    o_w = (_mm(p.astype(BF16), wsb) + p_self * wnew) / jnp.maximum(l, 1e-30)

    gate = gate_ref[0, 0]
    o_ref[0, 0] = gate[:, 0:1] * oc_ref[0, 0] + gate[:, 1:2] * o_s + gate[:, 2:3] * o_w
    last = lax.broadcasted_iota(I32, ws.shape, 0) == WINDOW - 1
    st_out[0] = jnp.where(last, wnew, pltpu.roll(ws, WINDOW - 1, 0))


def _nsa_sample(cache_sel, state, page_table, ids, q4, o_c, gates, sel_new, win_new, tabs, past):
    db, n_pages = page_table.shape
    pool = cache_sel.shape[0]
    gw = NSA_KV_HEADS * LANES
    halves = PAGE_SIZE // SEL_BLOCK
    n_cache_blk = past // SEL_BLOCK
    assert state.shape[1] == WINDOW and past >= WINDOW

    def tile_spec(k):
        def imap(i, g, pt, sel):
            blk = jnp.minimum(sel[(i * NSA_KV_HEADS + g) * SEL_TOPK + k], n_cache_blk - 1)
            return (pt[i * n_pages + blk // halves] * halves + blk % halves, 0, g)
        return pl.BlockSpec((1, SEL_BLOCK, LANES), imap)

    grp = pl.BlockSpec((1, 1, 8, LANES), lambda i, g, pt, sel: (i, g, 0, 0))
    new = pl.BlockSpec((1, 1, LANES), lambda i, g, pt, sel: (i, 0, g))
    st = pl.BlockSpec((1, WINDOW, LANES), lambda i, g, pt, sel: (i, 0, g))
    pad8 = lambda x: jnp.concatenate([x, jnp.zeros_like(x)], axis=1)
    tse = jnp.stack([pad8(tabs["sse"].reshape(NSA_KV_HEADS, NSA_GROUP, 4, LANES)[:, :, v]) for v in range(4)], 1)
    tso = jnp.stack([pad8(tabs["sso"].reshape(NSA_KV_HEADS, NSA_GROUP, 4, LANES)[:, :, v]) for v in range(4)], 1)
    tws = pad8(tabs["ws"].reshape(NSA_KV_HEADS, NSA_GROUP, WINDOW))
    t0 = pad8(tabs["ms"][:, 2].reshape(NSA_KV_HEADS, NSA_GROUP, LANES))
    grid_spec = pltpu.PrefetchScalarGridSpec(
        num_scalar_prefetch=2,
        grid=(db, NSA_KV_HEADS),
        in_specs=[grp, grp, grp, new, new, st] + [tile_spec(k) for k in range(SEL_TOPK)]
                 + [pl.BlockSpec((1, 4, 8, LANES), lambda i, g, pt, sel: (g, 0, 0, 0)),
                    pl.BlockSpec((1, 4, 8, LANES), lambda i, g, pt, sel: (g, 0, 0, 0)),
                    pl.BlockSpec((1, 8, WINDOW), lambda i, g, pt, sel: (g, 0, 0)),
                    pl.BlockSpec((1, 8, LANES), lambda i, g, pt, sel: (g, 0, 0))],
        out_specs=[grp, st],
    )
    cview = cache_sel.reshape(pool * halves, SEL_BLOCK, gw)
    o, st_new = pl.pallas_call(
        functools.partial(_nsa_sample_kernel, n_cache_blk=n_cache_blk),
        grid_spec=grid_spec,
        out_shape=[jax.ShapeDtypeStruct((db, NSA_KV_HEADS, 8, LANES), F32),
                   jax.ShapeDtypeStruct((db, WINDOW, gw), F32)],
        compiler_params=_cparams("parallel", "arbitrary"),
        name="nsa_sample",
    )(page_table.reshape(-1), ids[:, :NSA_KV_HEADS, :SEL_TOPK].reshape(-1), q4, o_c, gates,
      sel_new.reshape(db, 1, gw), win_new.reshape(db, 1, gw), state.reshape(db, WINDOW, gw),
      *([cview] * SEL_TOPK), tse, tso, tws, t0)
    return o[:, :, :NSA_GROUP, HEAD_DIM:].reshape(db, N_HEADS * HEAD_DIM), st_new


def kernel(x_prompt, x_sample, cache_moba_kv, cache_nsa_cmp_kv, cache_nsa_sel_kv, state_nsa_win_kv, page_table,
           rel_bias, attn_norm, ffn_norm, moba_w_qkv, moba_q_norm, moba_k_norm, moba_w_o, nsa_w_in,
           nsa_gate_bias, nsa_q_norm, nsa_k_norm, nsa_cmp_pos, nsa_cmp_w1, nsa_cmp_b1, nsa_cmp_w2, nsa_w_o,
           ffn_w_up, ffn_w_down):
    b, s, d = x_prompt.shape
    db = x_sample.shape[0]
    assert x_sample.shape[1] == 1 and d == N_HEADS * HEAD_DIM
    past = page_table.shape[1] * PAGE_SIZE
    depth = attn_norm.shape[0]
    tabs = _bias_tables(rel_bias, s, past)
    hp = x_prompt.reshape(b * s, d)
    hs = x_sample.reshape(db, d)
    outs = {k: [] for k in ("moba_p", "moba_s", "cmp_p", "cmp_s", "sel_p", "sel_s", "win_p", "win_s")}
    kvshape = lambda n, g: (n, -1, g, 2, HEAD_DIM)
    for i in range(depth):
        j = i // 2
        if i % 2 == 0:
            qp, kvp, kvpb, kmean = _proj_moba(hp, attn_norm[i], moba_w_qkv[j], moba_q_norm[j], moba_k_norm[j],
                                              q_wide=False, with_mean=True)
            qs, kvs, _ = _proj_moba(hs, attn_norm[i], moba_w_qkv[j], moba_q_norm[j], moba_k_norm[j],
                                    q_wide=True, with_mean=False)
            op = _moba_prompt(qp, kvpb, kmean, tabs["moba"], b, s)
            ids = _moba_select(cache_moba_kv[j], page_table, qs, past)
            osm = _moba_sample(cache_moba_kv[j], page_table, ids, qs, kvs, tabs["ms"], past)
            outs["moba_p"].append(kvp.reshape(kvshape(b, N_HEADS)))
            outs["moba_s"].append(kvs.reshape(kvshape(db, N_HEADS)))
            w_o = moba_w_o[j]
        else:
            qp, cp, sp, wp, spb, wpb, gp = _proj_nsa(hp, attn_norm[i], nsa_w_in[j], nsa_gate_bias[j],
                                                     nsa_q_norm[j], nsa_k_norm[j], q_wide=False)
            qs, cs, ss, ws, _, _, gs = _proj_nsa(hs, attn_norm[i], nsa_w_in[j], nsa_gate_bias[j],
                                                 nsa_q_norm[j], nsa_k_norm[j], q_wide=True)
            cw = _compress_weights(nsa_cmp_pos[j], nsa_cmp_w1[j], nsa_cmp_b1[j], nsa_cmp_w2[j], nsa_k_norm[j, 0])
            kcvc_p = _compress_prompt(cp, cw, b, s)
            op = _nsa_prompt(qp, gp, kcvc_p, spb, wpb, tabs, b, s)
            kcvc_s = _compress_sample(cache_nsa_cmp_kv[j], page_table, cs, cw, past)
            pad8 = lambda x: jnp.concatenate([x, jnp.zeros_like(x)], axis=2)
            q4 = pad8(qs.reshape(db, NSA_KV_HEADS, NSA_GROUP, LANES))
            g4 = pad8(gs.reshape(db, NSA_KV_HEADS, LANES)[:, :, :NSA_GROUP * N_BRANCH]
                      .reshape(db, NSA_KV_HEADS, NSA_GROUP, N_BRANCH))
            g4 = jnp.pad(g4, ((0, 0), (0, 0), (0, 0), (0, LANES - N_BRANCH)))
            o_c, sel_ids = _nsa_select(q4, kcvc_s, tabs["cs"], past)
            osm, ws_new = _nsa_sample(cache_nsa_sel_kv[j], state_nsa_win_kv[j], page_table, sel_ids, q4, o_c, g4,
                                      ss, ws, tabs, past)
            gsh = (NSA_KV_HEADS, 2, HEAD_DIM)
            outs["cmp_p"].append(cp.reshape((b, s) + gsh))
            outs["cmp_s"].append(cs.reshape((db, 1) + gsh))
            outs["sel_p"].append(sp.reshape((b, s) + gsh))
            outs["sel_s"].append(ss.reshape((db, 1) + gsh))
            outs["win_p"].append(wp.reshape((b, s) + gsh)[:, s - min(WINDOW, s):])
            outs["win_s"].append(ws_new.reshape((db, WINDOW) + gsh))
            w_o = nsa_w_o[j]
        hp = _attn_out_mlp(hp, op, w_o, ffn_norm[i], ffn_w_up[i], ffn_w_down[i])
        hs = _attn_out_mlp(hs, osm, w_o, ffn_norm[i], ffn_w_up[i], ffn_w_down[i])
    stack = lambda k: jnp.stack(outs[k])
    return (hp.reshape(b, s, d), hs.reshape(db, 1, d), stack("moba_p"), stack("moba_s"), stack("cmp_p"),
            stack("cmp_s"), stack("sel_p"), stack("sel_s"), stack("win_p"), stack("win_s"))
```

```python
import functools
import math

import numpy as np
import jax
import jax.numpy as jnp
from jax import lax
from jax.experimental import pallas as pl
from jax.experimental.pallas import tpu as pltpu

F32 = jnp.float32
BF16 = jnp.bfloat16
I32 = jnp.int32

N_HEADS = 16
HEAD_DIM = 64
NORM_EPS = 1e-6
ATTN_SCALE = HEAD_DIM ** -0.5
REL_BUCKETS = 32
REL_MAX_DIST = 128
PAGE_SIZE = 128
MOBA_BLOCK = 256
MOBA_TOPK = 3
NSA_KV_HEADS = 4
NSA_GROUP = N_HEADS // NSA_KV_HEADS
CMP_BLOCK = 32
CMP_STRIDE = 16
CMP_HIDDEN = 2 * HEAD_DIM
SEL_BLOCK = 64
SEL_TOPK = 16
N_LOCAL_SEL = 2
WINDOW = 512
N_BRANCH = 3

LANES = 128
LANE_SHIFT = 7
SEL_SHIFT = 6
MASK_NEG = -(2.0 ** 100)
VMEM_LIMIT = 56 * 1024 * 1024
NSA_TQ = 128
NSA_TK = 256
WIN_KEYS = WINDOW + NSA_TQ
CMP_NEAR = 32


def _cparams(*sem):
    return pltpu.CompilerParams(dimension_semantics=sem, vmem_limit_bytes=VMEM_LIMIT)


def _nt(a, b):
    return lax.dot_general(a, b, (((1,), (1,)), ((), ())), preferred_element_type=F32)


def _mm(a, b):
    return jnp.dot(a, b, preferred_element_type=F32)


def _split2(x):
    hi = x.astype(BF16)
    lo = (x - hi.astype(F32)).astype(BF16)
    return hi, lo


def _split3(x):
    hi = x.astype(BF16)
    r = x - hi.astype(F32)
    mid = r.astype(BF16)
    lo = (r - mid.astype(F32)).astype(BF16)
    return hi, mid, lo


def _rms_rows(x, g):
    return x * lax.rsqrt(jnp.mean(x * x, axis=-1, keepdims=True) + NORM_EPS) * g


def _group_norm(h, gmat_ref, gain, k_lanes_only):
    hi, lo = _split2(h * h)
    gm = gmat_ref[...]
    ss = _mm(hi, gm) + _mm(lo, gm)
    y = h * lax.rsqrt(ss * (1.0 / HEAD_DIM) + NORM_EPS) * gain
    if k_lanes_only:
        lane = lax.broadcasted_iota(I32, h.shape, 1)
        y = jnp.where((lane & (LANES - 1)) < HEAD_DIM, y, h)
    return y


def _softmax_rows(s):
    m = jnp.max(s, axis=-1, keepdims=True)
    m = jnp.where(m == -jnp.inf, 0.0, m)
    e = jnp.exp(s - m)
    return e / jnp.maximum(jnp.sum(e, axis=-1, keepdims=True), 1e-30)


def _bucket_of_dist():
    n = np.arange(REL_MAX_DIST + 1)
    max_exact = REL_BUCKETS // 2
    nf = np.maximum(n, 1).astype(np.float32)
    large = max_exact + (np.log(nf / np.float32(max_exact)) / np.float32(math.log(REL_MAX_DIST / max_exact))
                         * np.float32(REL_BUCKETS - max_exact)).astype(np.int32)
    large = np.minimum(large, REL_BUCKETS - 1)
    return np.where(n < max_exact, n, large).astype(np.int32)


def _dist_tables(seq, past):
    c = lambda d: np.clip(d, 0, REL_MAX_DIST)
    ncp = seq // CMP_STRIDE
    i256 = np.arange(MOBA_BLOCK)[:, None]
    j256 = np.arange(MOBA_BLOCK)[None, :]
    i128 = np.arange(NSA_TQ)[:, None]
    t = {}
    t["moba"] = np.stack([c(i256 - j256), c(MOBA_BLOCK + i256 - j256),
                          np.full((MOBA_BLOCK, MOBA_BLOCK), REL_MAX_DIST)])
    jk = np.arange(NSA_TK)[None, :]
    t["sel"] = np.stack([c(NSA_TQ * v + i128 - jk) for v in range(4)])
    t["win"] = c(i128 + WINDOW - np.arange(WIN_KEYS)[None, :])
    mcol = np.arange(ncp)[None, :]
    near = i128 - CMP_STRIDE * (mcol - CMP_NEAR // 2) - (CMP_BLOCK - 1)
    t["cmp"] = np.where(mcol < CMP_NEAR, c(near), REL_MAX_DIST)
    r128 = np.arange(LANES)
    t["ms"] = np.stack([c(LANES - r128), np.full(LANES, REL_MAX_DIST), np.zeros(LANES, np.int64)])
    nrow = _cmp_rows(past)
    t["cs"] = c(past - CMP_STRIDE * np.arange(nrow) - (CMP_BLOCK - 1))
    t["ws"] = c(WINDOW - np.arange(WINDOW))
    return t


def _cmp_rows(past):
    return -(-(past // CMP_STRIDE + 4) // 16) * 16


def _tab_kernel(idx_ref, rb_ref, o_ref):
    idx = idx_ref[...]
    b = lax.broadcasted_iota(I32, (REL_BUCKETS, idx.shape[1]), 0)
    oh = jnp.where(b == idx, 1.0, 0.0).astype(BF16)
    o_ref[...] = _mm(rb_ref[0], oh) + _mm(rb_ref[1], oh) + _mm(rb_ref[2], oh)


def _bias_tables(rel_bias, seq, past):
    pats = _dist_tables(seq, past)
    bucket = _bucket_of_dist()
    chunk = 16384
    flat, spans, off = [], {}, 0
    for name, d in pats.items():
        n = d.size
        pad = -n % LANES
        flat.append(bucket[d.reshape(-1)])
        flat.append(np.zeros(pad, np.int32))
        spans[name] = (off, n, d.shape)
        off += n + pad
    total = -(-off // chunk) * chunk
    flat.append(np.zeros(total - off, np.int32))
    idx = jnp.asarray(np.concatenate(flat).astype(np.int32)).reshape(1, total)
    rb3 = jnp.stack(_split3(rel_bias.T.astype(F32)))
    tab = pl.pallas_call(
        _tab_kernel,
        grid=(total // chunk,),
        in_specs=[pl.BlockSpec((1, chunk), lambda i: (0, i)),
                  pl.BlockSpec((3, N_HEADS, REL_BUCKETS), lambda i: (0, 0, 0))],
        out_specs=pl.BlockSpec((N_HEADS, chunk), lambda i: (0, i)),
        out_shape=jax.ShapeDtypeStruct((N_HEADS, total), F32),
        compiler_params=_cparams("parallel"),
        name="bias_tables",
    )(idx, rb3)
    return {name: tab[:, o:o + n].reshape((N_HEADS,) + shp) for name, (o, n, shp) in spans.items()}


def _group_mats():
    i = np.arange(2 * LANES)
    g64 = (i[:, None] // HEAD_DIM == i[None, :] // HEAD_DIM)
    g128 = (i[:, None] // LANES == i[None, :] // LANES) & ((i[:, None] % LANES) < HEAD_DIM)
    return jnp.asarray(g64, BF16), jnp.asarray(g128, BF16)


def _interleave_gain(g, n):
    return jnp.tile(jnp.concatenate([g.astype(F32), jnp.ones((HEAD_DIM,), F32)]), n).reshape(1, n * LANES)


def _store_rows_or_columns(ref, sl, x):
    if len(ref.shape) == 3:
        ref[0, sl, :] = x.T
    else:
        ref[:, sl] = x


def _token_major(x_t, b, s, n_kv):
    return x_t.reshape(b, n_kv, 2, HEAD_DIM, s).transpose(0, 4, 1, 2, 3)


def _proj_moba_kernel(x_ref, an_ref, wq_ref, wkv_ref, qg_ref, kg_ref, g64_ref, g128_ref,
                      q_ref, kv_ref, kvb_ref, *rest, n_mean):
    xn = _rms_rows(x_ref[...], an_ref[...]).astype(BF16)
    cw = 2 * LANES
    for c in range(wq_ref.shape[1] // cw):
        sl = slice(c * cw, (c + 1) * cw)
        h = _mm(xn, wq_ref[:, sl])
        q = _group_norm(h, g64_ref, qg_ref[:, sl], False)
        q_ref[:, sl] = q.astype(q_ref.dtype)
    for c in range(wkv_ref.shape[1] // cw):
        sl = slice(c * cw, (c + 1) * cw)
        h = _mm(xn, wkv_ref[:, sl])
        kv = _group_norm(h, g128_ref, kg_ref[:, sl], True)
        _store_rows_or_columns(kv_ref, sl, kv)
        kvb_ref[:, sl] = kv.astype(BF16)
        if n_mean:
            km_ref = rest[0]
            for r in range(n_mean):
                km_ref[0, r:r + 1, sl] = jnp.mean(kv[r * MOBA_BLOCK:(r + 1) * MOBA_BLOCK], axis=0, keepdims=True)


def _proj_moba(x, an, w_qkv, q_gain, k_gain, *, qdt, seq=None):
    m, d = x.shape
    tm = min(m, 512)
    with_mean = seq is not None
    g64, g128 = _group_mats()
    w3 = w_qkv.reshape(d, 3, N_HEADS, HEAD_DIM)
    wkv = jnp.stack([w3[:, 1], w3[:, 2]], axis=2).reshape(d, 2 * d).astype(BF16)
    wq = w3[:, 0].reshape(d, d).astype(BF16)
    qg = jnp.tile(q_gain.astype(F32), N_HEADS).reshape(1, d)
    kg = _interleave_gain(k_gain, N_HEADS)
    qn = wq.shape[1]
    n_mean = tm // MOBA_BLOCK if with_mean else 0
    const = lambda i: (0, 0)
    if seq is None:
        kv_shape, kv_spec = (m, 2 * d), pl.BlockSpec((tm, 2 * d), lambda i: (i, 0))
    else:
        nt = seq // tm
        kv_shape, kv_spec = (m // seq, 2 * d, seq), pl.BlockSpec((1, 2 * d, tm), lambda i: (i // nt, 0, i % nt))
    out_shape = [jax.ShapeDtypeStruct((m, qn), qdt), jax.ShapeDtypeStruct(kv_shape, F32),
                 jax.ShapeDtypeStruct((m, 2 * d), BF16)]
    out_specs = [pl.BlockSpec((tm, qn), lambda i: (i, 0)), kv_spec,
                 pl.BlockSpec((tm, 2 * d), lambda i: (i, 0))]
    if n_mean:
        out_shape.append(jax.ShapeDtypeStruct((m // tm, n_mean, 2 * d), F32))
        out_specs.append(pl.BlockSpec((1, n_mean, 2 * d), lambda i: (i, 0, 0)))
    return pl.pallas_call(
        functools.partial(_proj_moba_kernel, n_mean=n_mean),
        grid=(m // tm,),
        in_specs=[pl.BlockSpec((tm, d), lambda i: (i, 0)), pl.BlockSpec((1, d), const),
                  pl.BlockSpec((d, qn), const), pl.BlockSpec((d, 2 * d), const),
                  pl.BlockSpec((1, qn), const), pl.BlockSpec((1, 2 * d), const),
                  pl.BlockSpec(g64.shape, const), pl.BlockSpec(g128.shape, const)],
        out_specs=out_specs, out_shape=out_shape,
        compiler_params=_cparams("parallel"),
        name="proj_moba",
    )(x, an.reshape(1, d).astype(F32), wq, wkv, qg, kg, g64, g128)


def _proj_nsa_kernel(x_ref, an_ref, wq_ref, wkv_ref, wg_ref, gb_ref, qg_ref, kg_ref, g64_ref, g128_ref,
                     q_ref, cmp_ref, sel_ref, win_ref, selb_ref, winb_ref, gate_ref, *rest):
    cmp_t_ref = rest[0] if rest else None
    xn = _rms_rows(x_ref[...], an_ref[...]).astype(BF16)
    cw = 2 * LANES
    for c in range(wq_ref.shape[1] // cw):
        sl = slice(c * cw, (c + 1) * cw)
        h = _mm(xn, wq_ref[:, sl])
        q = _group_norm(h, g64_ref, qg_ref[:, sl], False)
        q_ref[:, sl] = q.astype(q_ref.dtype)
    per_branch = NSA_KV_HEADS * LANES // cw
    for c in range(wkv_ref.shape[1] // cw):
        sl = slice(c * cw, (c + 1) * cw)
        br, cc = divmod(c, per_branch)
        osl = slice(cc * cw, (cc + 1) * cw)
        h = _mm(xn, wkv_ref[:, sl])
        if br == 0:
            cmp_ref[:, osl] = h
            if cmp_t_ref is not None:
                _store_rows_or_columns(cmp_t_ref, osl, h)
        else:
            kv = _group_norm(h, g128_ref, kg_ref[:, sl], True)
            o32, o16 = (sel_ref, selb_ref) if br == 1 else (win_ref, winb_ref)
            _store_rows_or_columns(o32, osl, kv)
            o16[:, osl] = kv.astype(BF16)
    hg = _mm(xn, wg_ref[...]) + gb_ref[...]
    gate_ref[...] = 1.0 / (1.0 + jnp.exp(-hg))


def _proj_nsa(x, an, w_in, gate_bias, q_gain, k_gain, *, qdt, seq=None):
    m, d = x.shape
    tm = min(m, 512)
    g64, g128 = _group_mats()
    kvd = N_BRANCH * NSA_KV_HEADS * LANES
    gw = NSA_KV_HEADS * LANES
    ng = NSA_GROUP * N_BRANCH
    qg = jnp.tile(q_gain.astype(F32), N_HEADS).reshape(1, d)
    wq = w_in[:, :d].astype(BF16)
    wkv = w_in[:, d:d + kvd].astype(BF16)
    wg = jnp.zeros((d, NSA_KV_HEADS, LANES), F32).at[:, :, :ng].set(
        w_in[:, d + kvd:].reshape(d, NSA_KV_HEADS, ng)).reshape(d, gw).astype(BF16)
    gb = jnp.zeros((NSA_KV_HEADS, LANES), F32).at[:, :ng].set(
        gate_bias.astype(F32).reshape(NSA_KV_HEADS, ng)).reshape(1, gw)
    kg = jnp.concatenate([jnp.ones((1, gw), F32), _interleave_gain(k_gain[1], NSA_KV_HEADS),
                          _interleave_gain(k_gain[2], NSA_KV_HEADS)], axis=1)
    qn = wq.shape[1]
    const = lambda i: (0, 0)
    row = lambda n: pl.BlockSpec((tm, n), lambda i: (i, 0))
    rows32 = jax.ShapeDtypeStruct((m, gw), F32)
    if seq is None:
        kv_shape, kv_spec, extra_shape, extra_spec = rows32, row(gw), [], []
    else:
        nt = seq // tm
        kv_shape = jax.ShapeDtypeStruct((m // seq, gw, seq), F32)
        kv_spec = pl.BlockSpec((1, gw, tm), lambda i: (i // nt, 0, i % nt))
        extra_shape, extra_spec = [kv_shape], [kv_spec]
    return pl.pallas_call(
        _proj_nsa_kernel,
        grid=(m // tm,),
        in_specs=[row(d), pl.BlockSpec((1, d), const), pl.BlockSpec((d, qn), const),
                  pl.BlockSpec((d, kvd), const), pl.BlockSpec((d, gw), const), pl.BlockSpec((1, gw), const),
                  pl.BlockSpec((1, qn), const), pl.BlockSpec((1, kvd), const),
                  pl.BlockSpec(g64.shape, const), pl.BlockSpec(g128.shape, const)],
        out_specs=[row(qn), row(gw), kv_spec, kv_spec, row(gw), row(gw), row(gw)] + extra_spec,
        out_shape=[jax.ShapeDtypeStruct((m, qn), qdt), rows32, kv_shape, kv_shape]
                  + [jax.ShapeDtypeStruct((m, gw), BF16)] * 2 + [rows32] + extra_shape,
        compiler_params=_cparams("parallel"),
        name="proj_nsa",
    )(x, an.reshape(1, d).astype(F32), wq, wkv, wg, gb, qg, kg, g64, g128)


def _mlp_kernel(h_ref, o_ref, wo_ref, g_ref, wup_ref, wdn_ref, out_ref, h1_s, xn_s, acc_s):
    f = pl.program_id(1)

    @pl.when(f == 0)
    def _():
        h1 = h_ref[...] + _mm(o_ref[...], wo_ref[...])
        h1_s[...] = h1
        xn_s[...] = _rms_rows(h1, g_ref[...]).astype(BF16)
        acc_s[...] = jnp.zeros_like(acc_s)

    u = _mm(xn_s[...], wup_ref[...])
    u = jnp.square(jnp.maximum(u, 0.0)).astype(BF16)
    acc_s[...] += _mm(u, wdn_ref[...])

    @pl.when(f == pl.num_programs(1) - 1)
    def _():
        out_ref[...] = h1_s[...] + acc_s[...]


def _attn_out_mlp(h, o, w_o, fn, w_up, w_down):
    m, d = h.shape
    dff = w_up.shape[1]
    tm = min(m, 512)
    tf = 512
    return pl.pallas_call(
        _mlp_kernel,
        grid=(m // tm, dff // tf),
        in_specs=[pl.BlockSpec((tm, d), lambda i, f: (i, 0)), pl.BlockSpec((tm, d), lambda i, f: (i, 0)),
                  pl.BlockSpec((d, d), lambda i, f: (0, 0)), pl.BlockSpec((1, d), lambda i, f: (0, 0)),
                  pl.BlockSpec((d, tf), lambda i, f: (0, f)), pl.BlockSpec((tf, d), lambda i, f: (f, 0))],
        out_specs=pl.BlockSpec((tm, d), lambda i, f: (i, 0)),
        out_shape=jax.ShapeDtypeStruct((m, d), F32),
        scratch_shapes=[pltpu.VMEM((tm, d), F32), pltpu.VMEM((tm, d), BF16), pltpu.VMEM((tm, d), F32)],
        compiler_params=_cparams("parallel", "arbitrary"),
        name="attn_out_mlp",
    )(h, o.astype(BF16), w_o.astype(BF16), fn.reshape(1, d).astype(F32), w_up.astype(BF16), w_down.astype(BF16))


def _topk_rows(x, k, on_pick):
    n = x.shape[0]
    row = lax.broadcasted_iota(I32, x.shape, 0)
    for _ in range(k):
        m = jnp.max(x, axis=0, keepdims=True)
        idx = jnp.min(jnp.where(x == m, row, n), axis=0, keepdims=True)
        hit = row == idx
        on_pick(hit, m)
        x = jnp.where(hit, -jnp.inf, x)


def _online_softmax_step(m_ref, l_ref, acc_ref, slot, tiles):
    m_old = m_ref[slot]
    mx = jnp.max(tiles[0][0], axis=1, keepdims=True)
    for s, _ in tiles[1:]:
        mx = jnp.maximum(mx, jnp.max(s, axis=1, keepdims=True))
    mn = jnp.maximum(m_old, mx)
    a = jnp.exp(m_old - mn)
    l = a * l_ref[slot]
    acc = a * acc_ref[slot]
    for s, v in tiles:
        p = jnp.exp(s - mn)
        l = l + jnp.sum(p, axis=1, keepdims=True)
        acc = acc + _mm(p.astype(BF16), v)
    m_ref[slot] = mn
    l_ref[slot] = l
    acc_ref[slot] = acc


def _softmax_state_init(m_ref, l_ref, acc_ref):
    m_ref[...] = jnp.full(m_ref.shape, -jnp.inf, F32)
    l_ref[...] = jnp.zeros(l_ref.shape, F32)
    acc_ref[...] = jnp.zeros(acc_ref.shape, F32)


def _moba_prompt_kernel(q_ref, kv_ref, km_ref, tb_ref, eye_ref, o_ref, m_s, l_s, acc_s, *, nb):
    cur = pl.program_id(2)
    tq = MOBA_BLOCK
    lane = lax.broadcasted_iota(I32, (tq, LANES), 1)
    q2 = q_ref[...].astype(F32) * ATTN_SCALE
    causal = lax.broadcasted_iota(I32, (tq, tq), 1) <= lax.broadcasted_iota(I32, (tq, tq), 0)
    rown = lax.broadcasted_iota(I32, (nb, tq), 0)
    _softmax_state_init(m_s, l_s, acc_s)
    qzs, qaugs = [], []
    for hh in range(2):
        qz = jnp.where(lane < HEAD_DIM, q2 if hh == 0 else pltpu.roll(q2, HEAD_DIM, 1), 0.0).astype(BF16)
        km_hi, km_lo = _split2(km_ref[0, :, hh * LANES:(hh + 1) * LANES])
        gate = _nt(km_hi, qz) + _nt(km_lo, qz)
        state = [jnp.full((nb, tq), MASK_NEG, F32)]

        def pick(hit, m, state=state):
            state[0] = jnp.where(hit, jnp.where(m > -jnp.inf, 0.0, state[0]), state[0])

        _topk_rows(jnp.where(rown < cur, gate, -jnp.inf), MOBA_TOPK, pick)
        negt = jnp.concatenate([state[0], jnp.zeros((LANES - nb, tq), F32)], axis=0).astype(BF16)
        negsel = _nt(eye_ref[...], negt).astype(BF16)
        qzs.append(qz)
        qaugs.append(jnp.concatenate([qz, negsel], axis=1))

    def kv_tile(hh, n):
        return kv_ref[0, pl.ds(pl.multiple_of(n * tq, tq), tq), hh * LANES:(hh + 1) * LANES]

    for hh in range(2):
        kv_own = kv_tile(hh, cur)
        s = jnp.where(causal, _nt(qzs[hh], kv_own) + tb_ref[hh, 0], -jnp.inf)
        _online_softmax_step(m_s, l_s, acc_s, hh, [(s, kv_own)])

    def past_blocks(blocks):
        for hh in range(2):
            tiles = []
            for n in blocks:
                kvn = kv_tile(hh, n)
                onehot = jnp.where(lane == n, 1.0, 0.0).astype(BF16)
                s = _nt(qaugs[hh], jnp.concatenate([kvn, onehot], axis=1))
                tiles.append((s + tb_ref[hh, jnp.where(cur - n == 1, 1, 2)], kvn))
            _online_softmax_step(m_s, l_s, acc_s, hh, tiles)

    odd = lax.rem(cur, 2)

    @pl.when(odd == 1)
    def _():
        past_blocks([0])

    def body(i, carry):
        n = odd + 2 * i
        past_blocks([n, n + 1])
        return carry

    lax.fori_loop(0, cur // 2, body, 0)
    outs = [acc_s[hh] / jnp.maximum(l_s[hh], 1e-30) for hh in range(2)]
    o_ref[...] = jnp.where(lane < HEAD_DIM, pltpu.roll(outs[0], HEAD_DIM, 1), outs[1]).astype(o_ref.dtype)


def _moba_prompt(q, kvb, kmean, tb, b, s):
    d = q.shape[1]
    nb = s // MOBA_BLOCK
    assert s % MOBA_BLOCK == 0 and MOBA_TOPK <= nb <= LANES
    nq = s // MOBA_BLOCK
    eye = jnp.asarray(np.eye(MOBA_BLOCK), BF16)
    return pl.pallas_call(
        functools.partial(_moba_prompt_kernel, nb=nb),
        grid=(b, N_HEADS // 2, nq),
        in_specs=[pl.BlockSpec((MOBA_BLOCK, LANES), lambda i, h, t: (i * nq + t, h)),
                  pl.BlockSpec((1, s, 2 * LANES), lambda i, h, t: (i, 0, h)),
                  pl.BlockSpec((1, nb, 2 * LANES), lambda i, h, t: (i, 0, h)),
                  pl.BlockSpec((2, 3, MOBA_BLOCK, MOBA_BLOCK), lambda i, h, t: (h, 0, 0, 0)),
                  pl.BlockSpec(eye.shape, lambda i, h, t: (0, 0))],
        out_specs=pl.BlockSpec((MOBA_BLOCK, LANES), lambda i, h, t: (i * nq + t, h)),
        out_shape=jax.ShapeDtypeStruct((b * s, d), BF16),
        scratch_shapes=[pltpu.VMEM((2, MOBA_BLOCK, 1), F32), pltpu.VMEM((2, MOBA_BLOCK, 1), F32),
                        pltpu.VMEM((2, MOBA_BLOCK, LANES), F32)],
        compiler_params=_cparams("parallel", "parallel", "arbitrary"),
        name="moba_prompt",
    )(q, kvb.reshape(b, s, 2 * d), kmean.reshape(b, nb, 2 * d), tb, eye)


def _gelu_tanh(x):
    return 0.5 * x * (1.0 + jnp.tanh(math.sqrt(2.0 / math.pi) * (x + 0.044715 * (x * x * x))))


def _compress_weights(cmp_pos, w1, b1, w2, k_gain0):
    half = CMP_BLOCK // 2
    z = jnp.zeros((half, HEAD_DIM, CMP_HIDDEN), F32)

    def first_layer(lo):
        wk, wv = w1[0, lo:lo + half], w1[1, lo:lo + half]
        top = jnp.concatenate([wk, z], axis=2)
        bot = jnp.concatenate([z, wv], axis=2)
        return jnp.concatenate([top, bot], axis=1).reshape(half * LANES, 2 * CMP_HIDDEN).astype(BF16)

    zz = jnp.zeros((CMP_HIDDEN, HEAD_DIM), F32)
    w2bd = jnp.concatenate([jnp.concatenate([w2[0], zz], axis=1),
                            jnp.concatenate([zz, w2[1]], axis=1)], axis=0).astype(BF16)
    pos_a = cmp_pos[:half].reshape(half, LANES).astype(F32)
    pos_b = cmp_pos[half:].reshape(half, LANES).astype(F32)
    return (pos_a, pos_b, first_layer(0), first_layer(half), b1.reshape(1, 2 * CMP_HIDDEN).astype(F32), w2bd,
            _interleave_gain(k_gain0, 1))


def _compress_tail(xa_s, xb_s, wa_ref, wb_ref, b1_ref, w2_ref, kg_ref):
    ha = _mm(xa_s[...], wa_ref[...])
    hb = _mm(xb_s[...], wb_ref[...])
    rows = ha.shape[0]
    h = ha + pltpu.roll(hb, rows - 1, 0) + b1_ref[...]
    out = _mm(_gelu_tanh(h).astype(BF16), w2_ref[...])
    lane = lax.broadcasted_iota(I32, out.shape, 1)
    is_k = lane < HEAD_DIM
    ss = jnp.sum(jnp.where(is_k, out * out, 0.0), axis=1, keepdims=True)
    kn = out * lax.rsqrt(ss * (1.0 / HEAD_DIM) + NORM_EPS) * kg_ref[...]
    return jnp.where(is_k, kn, out)


def _compress_prompt_kernel(*refs, nseg):
    x_refs = refs[:NSA_KV_HEADS]
    pa_ref, pb_ref, wa_ref, wb_ref, b1_ref, w2_ref, kg_ref, o_ref, xa_s, xb_s = refs[NSA_KV_HEADS:]
    half = CMP_BLOCK // 2
    for g in range(NSA_KV_HEADS):
        for p in range(half):
            v = x_refs[g][0, pl.ds(p, nseg, stride=CMP_STRIDE), :]
            xa_s[g * nseg:(g + 1) * nseg, p * LANES:(p + 1) * LANES] = (v + pa_ref[p:p + 1, :]).astype(BF16)
            xb_s[g * nseg:(g + 1) * nseg, p * LANES:(p + 1) * LANES] = (v + pb_ref[p:p + 1, :]).astype(BF16)
    res = _compress_tail(xa_s, xb_s, wa_ref, wb_ref, b1_ref, w2_ref, kg_ref)
    for g in range(NSA_KV_HEADS):
        o_ref[0, g] = res[g * nseg:(g + 1) * nseg]


def _compress_prompt(kv_cmp, cw, b, s):
    nseg = s // CMP_STRIDE
    gw = NSA_KV_HEADS * LANES
    kdim = (CMP_BLOCK // 2) * LANES
    const = lambda i: (0, 0)
    return pl.pallas_call(
        functools.partial(_compress_prompt_kernel, nseg=nseg),
        grid=(b,),
        in_specs=[pl.BlockSpec((1, s, LANES), functools.partial(lambda i, g: (i, 0, g), g=g))
                  for g in range(NSA_KV_HEADS)] + [pl.BlockSpec(w.shape, const) for w in cw],
        out_specs=pl.BlockSpec((1, NSA_KV_HEADS, nseg, LANES), lambda i: (i, 0, 0, 0)),
        out_shape=jax.ShapeDtypeStruct((b, NSA_KV_HEADS, nseg, LANES), F32),
        scratch_shapes=[pltpu.VMEM((NSA_KV_HEADS * nseg, kdim), BF16)] * 2,
        compiler_params=_cparams("parallel"),
        name="compress_prompt",
    )(*([kv_cmp.reshape(b, s, gw)] * NSA_KV_HEADS), *cw)


def _compress_sample_kernel(pt_ref, *refs, nseg, nrow):
    x_refs = refs[:2 * NSA_KV_HEADS]
    (new_ref, pa_ref, pb_ref, wa_ref, wb_ref, b1_ref, w2_ref, kg_ref, o_ref,
     xa_s, xb_s, xt_s) = refs[2 * NSA_KV_HEADS:]
    j = pl.program_id(1)
    for t in range(2 * NSA_KV_HEADS):
        xt_s[t] = x_refs[t][0, 0].T
    half = CMP_BLOCK // 2
    per_page = PAGE_SIZE // CMP_STRIDE
    tail = nrow - nseg

    @pl.when(j == 0)
    def _():
        first = lax.broadcasted_iota(I32, (tail, LANES), 0) == 0
        for g in range(NSA_KV_HEADS):
            rows = slice(g * nrow + nseg, (g + 1) * nrow)
            for p in range(half):
                v = jnp.zeros((tail, LANES), F32)
                if p == 0:
                    v = jnp.where(first, new_ref[0, :, g * LANES:(g + 1) * LANES], 0.0)
                xa_s[rows, p * LANES:(p + 1) * LANES] = (v + pa_ref[p:p + 1, :]).astype(BF16)
                xb_s[rows, p * LANES:(p + 1) * LANES] = (v + pb_ref[p:p + 1, :]).astype(BF16)

    for g in range(NSA_KV_HEADS):
        r0 = pl.multiple_of(g * nrow + j * (2 * per_page), 2 * per_page)
        for p in range(half):
            v = jnp.concatenate([xt_s[2 * g + pg, pl.ds(p, per_page, stride=CMP_STRIDE), :]
                                 for pg in range(2)], axis=0)
            xa_s[pl.ds(r0, 2 * per_page), p * LANES:(p + 1) * LANES] = (v + pa_ref[p:p + 1, :]).astype(BF16)
            xb_s[pl.ds(r0, 2 * per_page), p * LANES:(p + 1) * LANES] = (v + pb_ref[p:p + 1, :]).astype(BF16)

    @pl.when(j == pl.num_programs(1) - 1)
    def _():
        res = _compress_tail(xa_s, xb_s, wa_ref, wb_ref, b1_ref, w2_ref, kg_ref)
        for g in range(NSA_KV_HEADS):
            o_ref[0, g] = res[g * nrow:(g + 1) * nrow]


def _compress_sample(cache, page_table, new_rows, cw, past):
    db, n_pages = page_table.shape
    assert n_pages % 2 == 0
    pool = cache.shape[0]
    gw = NSA_KV_HEADS * LANES
    nseg = past // CMP_STRIDE
    nrow = _cmp_rows(past)
    kdim = (CMP_BLOCK // 2) * LANES
    const = lambda i, j, pt: (0, 0)
    grid_spec = pltpu.PrefetchScalarGridSpec(
        num_scalar_prefetch=1,
        grid=(db, n_pages // 2),
        in_specs=[pl.BlockSpec((1, 1, LANES, PAGE_SIZE),
                               functools.partial(lambda i, j, pt, g, pg: (pt[i * n_pages + 2 * j + pg], g, 0, 0),
                                                 g=g, pg=pg))
                  for g in range(NSA_KV_HEADS) for pg in range(2)]
                 + [pl.BlockSpec((1, 1, gw), lambda i, j, pt: (i, 0, 0))]
                 + [pl.BlockSpec(w.shape, const) for w in cw],
        out_specs=pl.BlockSpec((1, NSA_KV_HEADS, nrow, LANES), lambda i, j, pt: (i, 0, 0, 0)),
        scratch_shapes=[pltpu.VMEM((NSA_KV_HEADS * nrow, kdim), BF16)] * 2
                       + [pltpu.VMEM((2 * NSA_KV_HEADS, PAGE_SIZE, LANES), F32)],
    )
    cview = _slot_minor(cache, NSA_KV_HEADS)
    return pl.pallas_call(
        functools.partial(_compress_sample_kernel, nseg=nseg, nrow=nrow),
        grid_spec=grid_spec,
        out_shape=jax.ShapeDtypeStruct((db, NSA_KV_HEADS, nrow, LANES), F32),
        compiler_params=_cparams("parallel", "arbitrary"),
        name="compress_sample",
    )(page_table.reshape(-1), *([cview] * (2 * NSA_KV_HEADS)), new_rows.reshape(db, 1, gw), *cw)


def _imp_matrix(n_sel, n_rows, n_cmp):
    ratio = SEL_BLOCK // CMP_STRIDE
    lead = CMP_BLOCK // CMP_STRIDE - 1
    j = np.arange(n_sel)[:, None]
    n = np.arange(n_rows)[None, :]
    return ((n >= ratio * j - lead) & (n <= ratio * j + ratio - 1) & (n < n_cmp))


def _stack_heads(q_f32):
    tq = q_f32.shape[0]
    lane = lax.broadcasted_iota(I32, (tq, LANES), 1)
    parts = []
    for p in range(NSA_GROUP):
        blk = q_f32[:, (p // 2) * LANES:(p // 2 + 1) * LANES]
        if p % 2:
            blk = pltpu.roll(blk, HEAD_DIM, 1)
        parts.append(jnp.where(lane < HEAD_DIM, blk, 0.0))
    return jnp.concatenate(parts, axis=0)


def _nsa_prompt_kernel(q_ref, gate_ref, kc_ref, ks_ref, kw_ref, tsel_ref, tw_ref, tc_ref, at_ref, eye_ref,
                       o_ref, m_s, l_s, acc_s, *, n_sel, ncp):
    t = pl.program_id(2)
    tq, tk, grp = NSA_TQ, NSA_TK, NSA_GROUP
    rows = grp * tq
    p0 = t * tq
    qs = _stack_heads(q_ref[...].astype(F32) * ATTN_SCALE).astype(BF16)
    qi = lax.broadcasted_iota(I32, (tq, 1), 0)

    kc = kc_ref[0, 0].astype(BF16)
    shift = lax.rem(t * (tq // CMP_STRIDE) - CMP_NEAR // 2 + ncp, ncp)
    bias_c = jnp.concatenate([pltpu.roll(tc_ref[p], shift, 1) for p in range(grp)], axis=0)
    c_end = lax.broadcasted_iota(I32, (tq, ncp), 1) * CMP_STRIDE + (CMP_BLOCK - 1)
    ok_c = (p0 + qi) >= c_end
    ok_c = jnp.concatenate([ok_c] * grp, axis=0)
    p_c = _softmax_rows(jnp.where(ok_c, _nt(qs, kc) + bias_c, -jnp.inf))
    o_c = _mm(p_c.astype(BF16), kc)
    p_sum = p_c[0:tq]
    for p in range(1, grp):
        p_sum = p_sum + p_c[p * tq:(p + 1) * tq]
    at = at_ref[...]
    imp = sum(_nt(at, part) for part in _split3(p_sum))

    blk = lax.broadcasted_iota(I32, (n_sel, tq), 0)
    cur = jnp.right_shift(p0 + lax.broadcasted_iota(I32, (n_sel, tq), 1), SEL_SHIFT)
    valid = blk <= cur
    forced = (blk == 0) | (blk >= cur - (N_LOCAL_SEL - 1))
    score = jnp.where(valid, jnp.where(forced, jnp.inf, imp), -jnp.inf)
    state = [jnp.full((n_sel, tq), MASK_NEG, F32)]

    def pick(hit, m):
        state[0] = jnp.where(hit, jnp.where(m > -jnp.inf, 0.0, state[0]), state[0])

    _topk_rows(score, SEL_TOPK, pick)
    negt = state[0]
    if n_sel < LANES:
        negt = jnp.concatenate([negt, jnp.zeros((LANES - n_sel, tq), F32)], axis=0)
    negsel = _nt(eye_ref[...], negt.astype(BF16)).astype(BF16)
    qaug = jnp.concatenate([qs, jnp.concatenate([negsel] * grp, axis=0)], axis=1)

    kj = lax.broadcasted_iota(I32, (tk, LANES), 0)
    kl = lax.broadcasted_iota(I32, (tk, LANES), 1)

    def sel_scores(kt, variant):
        kv = ks_ref[0, pl.ds(pl.multiple_of(kt * tk, tk), tk), :]
        onehot = jnp.where(kl == kt * (tk // SEL_BLOCK) + jnp.right_shift(kj, SEL_SHIFT), 1.0, 0.0).astype(BF16)
        bias = tsel_ref[0, variant].reshape(rows, tk)
        return _nt(qaug, jnp.concatenate([kv, onehot], axis=1)) + bias, kv

    _softmax_state_init(m_s, l_s, acc_s)
    kt_d = t // (tk // tq)
    off_d = lax.rem(t, tk // tq) * tq
    s, kv = sel_scores(kt_d, off_d // tq)
    col = lax.broadcasted_iota(I32, (tq, tk), 1)
    ok_d = jnp.concatenate([col <= qi + off_d] * grp, axis=0)
    _online_softmax_step(m_s, l_s, acc_s, 0, [(jnp.where(ok_d, s, -jnp.inf), kv)])

    def past_tiles(kts):
        _online_softmax_step(m_s, l_s, acc_s, 0,
                             [sel_scores(kt, jnp.minimum((p0 - kt * tk) // tq, 3)) for kt in kts])

    odd = lax.rem(kt_d, 2)

    @pl.when(odd == 1)
    def _():
        past_tiles([0])

    def body(i, carry):
        kt = odd + 2 * i
        past_tiles([kt, kt + 1])
        return carry

    lax.fori_loop(0, kt_d // 2, body, 0)
    o_s = acc_s[0] / jnp.maximum(l_s[0], 1e-30)

    kw = kw_ref[0, pl.ds(pl.multiple_of(p0, tq), WIN_KEYS), :]
    wj = lax.broadcasted_iota(I32, (tq, WIN_KEYS), 1)
    ok_w = (wj > qi) & (wj <= qi + WINDOW) & (p0 + wj >= WINDOW)
    ok_w = jnp.concatenate([ok_w] * grp, axis=0)
    s_w = _nt(qs, kw) + tw_ref[...].reshape(rows, WIN_KEYS)
    p_w = _softmax_rows(jnp.where(ok_w, s_w, -jnp.inf))
    o_w = _mm(p_w.astype(BF16), kw)

    gate = gate_ref[...]
    lane = lax.broadcasted_iota(I32, (tq, LANES), 1)
    per_head = []
    for p in range(grp):
        r = slice(p * tq, (p + 1) * tq)
        g = [gate[:, N_BRANCH * p + k:N_BRANCH * p + k + 1] for k in range(N_BRANCH)]
        per_head.append(g[0] * o_c[r] + g[1] * o_s[r] + g[2] * o_w[r])
    halves = [jnp.where(lane < HEAD_DIM, pltpu.roll(per_head[2 * c], HEAD_DIM, 1), per_head[2 * c + 1])
              for c in range(grp // 2)]
    o_ref[...] = jnp.concatenate(halves, axis=1).astype(o_ref.dtype)


def _nsa_prompt(q, gate, kcvc, selb, winb, tabs, b, s):
    d = q.shape[1]
    gw = NSA_KV_HEADS * LANES
    ncp = s // CMP_STRIDE
    n_sel = s // SEL_BLOCK
    assert s % NSA_TK == 0 and SEL_TOPK <= n_sel <= LANES and ncp >= CMP_NEAR
    nq = s // NSA_TQ
    at = jnp.asarray(_imp_matrix(n_sel, ncp, ncp - 1), BF16)
    eye = jnp.asarray(np.eye(NSA_TQ), BF16)
    winp = jnp.pad(winb.reshape(b, s, gw), ((0, 0), (WINDOW, 0), (0, 0)))
    gq = NSA_GROUP
    tsel = tabs["sel"].reshape(NSA_KV_HEADS, gq, 4, NSA_TQ, NSA_TK).transpose(0, 2, 1, 3, 4)
    return pl.pallas_call(
        functools.partial(_nsa_prompt_kernel, n_sel=n_sel, ncp=ncp),
        grid=(b, NSA_KV_HEADS, nq),
        in_specs=[pl.BlockSpec((NSA_TQ, gq * HEAD_DIM), lambda i, g, t: (i * nq + t, g)),
                  pl.BlockSpec((NSA_TQ, LANES), lambda i, g, t: (i * nq + t, g)),
                  pl.BlockSpec((1, 1, ncp, LANES), lambda i, g, t: (i, g, 0, 0)),
                  pl.BlockSpec((1, s, LANES), lambda i, g, t: (i, 0, g)),
                  pl.BlockSpec((1, s + WINDOW, LANES), lambda i, g, t: (i, 0, g)),
                  pl.BlockSpec((1, 4, gq, NSA_TQ, NSA_TK), lambda i, g, t: (g, 0, 0, 0, 0)),
                  pl.BlockSpec((gq, NSA_TQ, WIN_KEYS), lambda i, g, t: (g, 0, 0)),
                  pl.BlockSpec((gq, NSA_TQ, ncp), lambda i, g, t: (g, 0, 0)),
                  pl.BlockSpec(at.shape, lambda i, g, t: (0, 0)),
                  pl.BlockSpec(eye.shape, lambda i, g, t: (0, 0))],
        out_specs=pl.BlockSpec((NSA_TQ, gq * HEAD_DIM), lambda i, g, t: (i * nq + t, g)),
        out_shape=jax.ShapeDtypeStruct((b * s, d), BF16),
        scratch_shapes=[pltpu.VMEM((1, gq * NSA_TQ, 1), F32), pltpu.VMEM((1, gq * NSA_TQ, 1), F32),
                        pltpu.VMEM((1, gq * NSA_TQ, LANES), F32)],
        compiler_params=_cparams("parallel", "parallel", "arbitrary"),
        name="nsa_prompt",
    )(q, gate, kcvc, selb.reshape(b, s, gw), winp, tsel, tabs["win"], tabs["cmp"], at, eye)


def _topk_lanes(x, k):
    ax = x.ndim - 1
    lane = lax.broadcasted_iota(I32, x.shape, ax)
    oshape = x.shape[:-1] + (LANES,)
    out_lane = lax.broadcasted_iota(I32, oshape, ax)
    ids = jnp.zeros(oshape, I32)
    for r in range(k):
        m = jnp.max(x, axis=ax, keepdims=True)
        idx = jnp.min(jnp.where(x == m, lane, x.shape[ax]), axis=ax, keepdims=True)
        ids = jnp.where(out_lane == r, idx, ids)
        x = jnp.where(lane == idx, -jnp.inf, x)
    return ids


def _slot_minor(cache, n_kv):
    pages, slots = cache.shape[:2]
    return jnp.transpose(cache, (0, 2, 3, 4, 1)).reshape(pages, n_kv, LANES, slots)


MOBA_SELECT_PAGES = 4


def _moba_select_kernel(pt_ref, *refs, nb):
    x_refs = refs[:MOBA_SELECT_PAGES]
    q_ref, ids_ref, ksum_s = refs[MOBA_SELECT_PAGES:]
    j = pl.program_id(1)
    ppb = MOBA_BLOCK // PAGE_SIZE
    per_step = MOBA_SELECT_PAGES // ppb

    @pl.when(j == 0)
    def _():
        ksum_s[...] = jnp.zeros_like(ksum_s)

    lane = lax.broadcasted_iota(I32, ksum_s.shape, 2)
    acc = ksum_s[...]
    for bi in range(per_step):
        x = x_refs[bi * ppb][0]
        for pg in range(1, ppb):
            x = x + x_refs[bi * ppb + pg][0]
        col = jnp.sum(x, axis=2, keepdims=True)
        acc = acc + jnp.where(lane == j * per_step + bi, col, 0.0)
    ksum_s[...] = acc

    @pl.when(j == pl.num_programs(1) - 1)
    def _():
        km = ksum_s[...] * (1.0 / MOBA_BLOCK)
        gate = sum(jnp.einsum("hqd,hdn->hqn", a, b_, preferred_element_type=F32)
                   for a in _split3(q_ref[0]) for b_ in _split3(km))
        blk = lax.broadcasted_iota(I32, gate.shape, 2)
        ids_ref[0] = _topk_lanes(jnp.where(blk < nb, gate, -jnp.inf), MOBA_TOPK)


def _moba_select(cache_t, page_table, q, past):
    db, n_pages = page_table.shape
    nb = past // MOBA_BLOCK
    assert past % MOBA_BLOCK == 0 and MOBA_TOPK <= nb <= LANES and n_pages % MOBA_SELECT_PAGES == 0
    q8 = jnp.pad(q.reshape(db, N_HEADS, 1, HEAD_DIM), ((0, 0), (0, 0), (0, 7), (0, 0)))
    grid_spec = pltpu.PrefetchScalarGridSpec(
        num_scalar_prefetch=1,
        grid=(db, n_pages // MOBA_SELECT_PAGES),
        in_specs=[pl.BlockSpec((1, N_HEADS, HEAD_DIM, PAGE_SIZE),
                               functools.partial(lambda i, j, pt, pg: (pt[i * n_pages + MOBA_SELECT_PAGES * j + pg],
                                                                       0, 0, 0), pg=pg))
                  for pg in range(MOBA_SELECT_PAGES)]
                 + [pl.BlockSpec((1, N_HEADS, 8, HEAD_DIM), lambda i, j, pt: (i, 0, 0, 0))],
        out_specs=pl.BlockSpec((1, N_HEADS, 8, LANES), lambda i, j, pt: (i, 0, 0, 0)),
        scratch_shapes=[pltpu.VMEM((N_HEADS, HEAD_DIM, LANES), F32)],
    )
    ids = pl.pallas_call(
        functools.partial(_moba_select_kernel, nb=nb),
        grid_spec=grid_spec,
        out_shape=jax.ShapeDtypeStruct((db, N_HEADS, 8, LANES), I32),
        compiler_params=_cparams("parallel", "arbitrary"),
        name="moba_select",
    )(page_table.reshape(-1), *([cache_t] * MOBA_SELECT_PAGES), q8)
    return ids[:, :, 0, :MOBA_TOPK]


def _pad_rows(x, rows):
    return jnp.concatenate([x, jnp.zeros((rows - x.shape[0],) + x.shape[1:], x.dtype)], axis=0)


def _moba_sample_kernel(pt_ref, ids_ref, q_ref, new_ref, *rest, n_tiles, last_blk, pages_per_block):
    tiles = rest[:n_tiles]
    tab_ref, o_ref = rest[n_tiles], rest[n_tiles + 1]
    i, h = pl.program_id(0), pl.program_id(1)
    q8 = q_ref[0, 0] * ATTN_SCALE
    q8b = q8.astype(BF16)
    new = new_ref[0, 0]
    k_new, v_new = new[:, :HEAD_DIM], new[:, HEAD_DIM:]
    scores, vts = [], []
    for ti in range(n_tiles):
        k, pp = divmod(ti, pages_per_block)
        blk = ids_ref[(i * N_HEADS + h) * MOBA_TOPK + k]
        near = jnp.logical_and(blk == last_blk, pp == pages_per_block - 1)
        tile = tiles[ti][0, 0]
        vts.append(tile[HEAD_DIM:].astype(BF16))
        scores.append(_mm(q8b, tile[:HEAD_DIM].astype(BF16))
                      + jnp.where(near, tab_ref[0, 0:1, :], tab_ref[0, 1:2, :]))
    s = jnp.concatenate(scores, axis=1)
    s_self = jnp.sum(q8 * k_new, axis=1, keepdims=True) + tab_ref[0, 2:3, 0:1]
    m = jnp.maximum(jnp.max(s, axis=1, keepdims=True), s_self)
    p = jnp.exp(s - m)
    p_self = jnp.exp(s_self - m)
    l = jnp.sum(p, axis=1, keepdims=True) + p_self
    o = p_self * v_new
    for ti in range(n_tiles):
        o = o + _nt(p[:, ti * PAGE_SIZE:(ti + 1) * PAGE_SIZE].astype(BF16), vts[ti])
    o_ref[0, 0] = o / jnp.maximum(l, 1e-30)


def _moba_sample(cache_t, page_table, ids, q, kv_new, tab_ms, past):
    db, n_pages = page_table.shape
    ppb = MOBA_BLOCK // PAGE_SIZE
    n_tiles = MOBA_TOPK * ppb

    def tile_spec(ti):
        k, pp = divmod(ti, ppb)
        return pl.BlockSpec(
            (1, 1, LANES, PAGE_SIZE),
            lambda i, h, pt, sel: (pt[i * n_pages + sel[(i * N_HEADS + h) * MOBA_TOPK + k] * ppb + pp], h, 0, 0))

    q8 = jnp.pad(q.reshape(db, N_HEADS, 1, HEAD_DIM), ((0, 0), (0, 0), (0, 7), (0, 0)))
    grid_spec = pltpu.PrefetchScalarGridSpec(
        num_scalar_prefetch=2,
        grid=(db, N_HEADS),
        in_specs=[pl.BlockSpec((1, 1, 8, HEAD_DIM), lambda i, h, pt, sel: (i, h, 0, 0)),
                  pl.BlockSpec((1, 1, 1, LANES), lambda i, h, pt, sel: (i, h, 0, 0))]
                 + [tile_spec(ti) for ti in range(n_tiles)]
                 + [pl.BlockSpec((1, 3, LANES), lambda i, h, pt, sel: (h, 0, 0))],
        out_specs=pl.BlockSpec((1, 1, 8, HEAD_DIM), lambda i, h, pt, sel: (i, h, 0, 0)),
    )
    out = pl.pallas_call(
        functools.partial(_moba_sample_kernel, n_tiles=n_tiles, last_blk=past // MOBA_BLOCK - 1,
                          pages_per_block=ppb),
        grid_spec=grid_spec,
        out_shape=jax.ShapeDtypeStruct((db, N_HEADS, 8, HEAD_DIM), F32),
        compiler_params=_cparams("parallel", "arbitrary"),
        name="moba_sample",
    )(page_table.reshape(-1), ids.reshape(-1), q8, kv_new.reshape(db, N_HEADS, 1, LANES),
      *([cache_t] * n_tiles), tab_ms)
    return out[:, :, 0, :].reshape(db, N_HEADS * HEAD_DIM)


def _nsa_select_kernel(q_ref, kc_ref, tc_ref, a_ref, oc_ref, ids_ref, *, n_valid, n_sel, cur):
    nrow = kc_ref.shape[2]
    lane = lax.broadcasted_iota(I32, (8, nrow), 1)
    psums = []
    for g in range(NSA_KV_HEADS):
        q8 = (q_ref[0, g] * ATTN_SCALE).astype(BF16)
        kc = kc_ref[0, g].astype(BF16)
        s = jnp.where(lane < n_valid, _nt(q8, kc) + tc_ref[g], -jnp.inf)
        p_c = _softmax_rows(s)
        oc_ref[0, g] = _mm(p_c.astype(BF16), kc)[:, HEAD_DIM:]
        psums.append(jnp.sum(p_c[0:NSA_GROUP], axis=0, keepdims=True))
    p_sum = _pad_rows(jnp.concatenate(psums, axis=0), 8)
    a = a_ref[...]
    imp = sum(_mm(part, a) for part in _split3(p_sum))
    blk = lax.broadcasted_iota(I32, imp.shape, 1)
    valid = blk <= min(cur, n_sel - 1)
    forced = (blk == 0) | (blk >= cur - (N_LOCAL_SEL - 1))
    score = jnp.where(valid, jnp.where(forced, jnp.inf, imp), -jnp.inf)
    ids_ref[0] = _topk_lanes(score, SEL_TOPK)


def _nsa_select(q4, kcvc, tab_cs, past):
    db = q4.shape[0]
    nrow = kcvc.shape[2]
    n_cmp_valid = (past - (CMP_BLOCK - 1)) // CMP_STRIDE + 1
    n_sel = max(-(-(past + 1) // SEL_BLOCK), SEL_TOPK)
    n_cmp = n_sel * SEL_BLOCK // CMP_STRIDE - CMP_BLOCK // CMP_STRIDE + 1
    cur = past // SEL_BLOCK
    assert n_sel <= 2 * LANES and cur >= SEL_TOPK - 1 and n_cmp <= nrow
    a = jnp.asarray(_imp_matrix(2 * LANES, nrow, n_cmp).T, BF16)
    tc = jnp.concatenate([tab_cs.reshape(NSA_KV_HEADS, NSA_GROUP, nrow)] * 2, axis=1)
    return pl.pallas_call(
        functools.partial(_nsa_select_kernel, n_valid=n_cmp_valid, n_sel=n_sel, cur=cur),
        grid=(db,),
        in_specs=[pl.BlockSpec((1, NSA_KV_HEADS, 8, LANES), lambda i: (i, 0, 0, 0)),
                  pl.BlockSpec((1, NSA_KV_HEADS, nrow, LANES), lambda i: (i, 0, 0, 0)),
                  pl.BlockSpec(tc.shape, lambda i: (0, 0, 0)),
                  pl.BlockSpec(a.shape, lambda i: (0, 0))],
        out_specs=[pl.BlockSpec((1, NSA_KV_HEADS, 8, HEAD_DIM), lambda i: (i, 0, 0, 0)),
                   pl.BlockSpec((1, 8, LANES), lambda i: (i, 0, 0))],
        out_shape=[jax.ShapeDtypeStruct((db, NSA_KV_HEADS, 8, HEAD_DIM), F32),
                   jax.ShapeDtypeStruct((db, 8, LANES), I32)],
        compiler_params=_cparams("parallel"),
        name="nsa_select",
    )(q4, kcvc, tc, a)


def _nsa_sample_kernel(pt_ref, ids_ref, q_ref, oc_ref, gate_ref, snew_ref, wnew_ref, st_ref, *rest,
                       n_cache_blk):
    tiles = rest[:SEL_TOPK]
    tms_ref, tws_ref, eye_ref, o_ref, st_out = rest[SEL_TOPK:]
    i, g = pl.program_id(0), pl.program_id(1)
    q8 = q_ref[0, 0] * ATTN_SCALE
    q8b = q8.astype(BF16)
    halves = PAGE_SIZE // SEL_BLOCK
    last_page = n_cache_blk // halves - 1
    half_of_lane = jnp.right_shift(lax.broadcasted_iota(I32, (8, PAGE_SIZE), 1), SEL_SHIFT)
    t0 = tms_ref[0, 2][:, 0:1]

    scores, vts = [], []
    for k in range(SEL_TOPK):
        blk = ids_ref[(i * NSA_KV_HEADS + g) * SEL_TOPK + k]
        in_cache = blk < n_cache_blk
        half = jnp.where(in_cache, lax.rem(blk, halves), -1)
        near = jnp.logical_and(in_cache, blk // halves == last_page)
        tile = tiles[k][0, 0]
        vts.append(tile[HEAD_DIM:].astype(BF16))
        s = _mm(q8b, tile[:HEAD_DIM].astype(BF16)) + jnp.where(near, tms_ref[0, 0], tms_ref[0, 1])
        scores.append(jnp.where(half_of_lane == half, s, -jnp.inf))
    s = jnp.concatenate(scores, axis=1)
    snew = snew_ref[0]
    s_self = jnp.sum(q8 * snew[:, :HEAD_DIM], axis=1, keepdims=True) + t0
    m = jnp.maximum(jnp.max(s, axis=1, keepdims=True), s_self)
    p = jnp.exp(s - m)
    p_self = jnp.exp(s_self - m)
    l = jnp.sum(p, axis=1, keepdims=True) + p_self
    o_s = p_self * snew[:, HEAD_DIM:]
    for k in range(SEL_TOPK):
        o_s = o_s + _nt(p[:, k * PAGE_SIZE:(k + 1) * PAGE_SIZE].astype(BF16), vts[k])
    o_s = o_s / jnp.maximum(l, 1e-30)

    ws = st_ref[0, 0]
    wl = lax.broadcasted_iota(I32, (8, WINDOW), 1)
    s_w = jnp.where(wl >= 1, _mm(q8b, ws[:HEAD_DIM].astype(BF16)) + tws_ref[0], -jnp.inf)
    wnew = wnew_ref[0]
    w_self = jnp.sum(q8 * wnew[:, :HEAD_DIM], axis=1, keepdims=True) + t0
    m = jnp.maximum(jnp.max(s_w, axis=1, keepdims=True), w_self)
    p = jnp.exp(s_w - m)
    p_self = jnp.exp(w_self - m)
    l = jnp.sum(p, axis=1, keepdims=True) + p_self
    o_w = (_nt(p.astype(BF16), ws[HEAD_DIM:].astype(BF16)) + p_self * wnew[:, HEAD_DIM:]) / jnp.maximum(l, 1e-30)

    gate = gate_ref[0, 0]
    o_ref[0, 0] = gate[:, 0:1] * oc_ref[0, 0] + gate[:, 1:2] * o_s + gate[:, 2:3] * o_w
    new_col = sum(_nt(eye_ref[...], _pad_rows(part, 8)) for part in _split3(wnew))[:, 0:1]
    wcol = lax.broadcasted_iota(I32, ws.shape, 1)
    st_out[0, 0] = jnp.where(wcol == WINDOW - 1, new_col, pltpu.roll(ws, WINDOW - 1, 1))


def _nsa_sample(cache_t, state_t, page_table, ids, q4, o_c, gates, sel_new, win_new, tabs, past):
    db, n_pages = page_table.shape
    gw = NSA_KV_HEADS * LANES
    halves = PAGE_SIZE // SEL_BLOCK
    n_cache_blk = past // SEL_BLOCK
    assert state_t.shape[-1] == WINDOW and past >= WINDOW and past % PAGE_SIZE == 0

    def tile_spec(k):
        def imap(i, g, pt, sel):
            blk = jnp.minimum(sel[(i * NSA_KV_HEADS + g) * SEL_TOPK + k], n_cache_blk - 1)
            return (pt[i * n_pages + blk // halves], g, 0, 0)
        return pl.BlockSpec((1, 1, LANES, PAGE_SIZE), imap)

    grp = lambda n: pl.BlockSpec((1, 1, 8, n), lambda i, g, pt, sel: (i, g, 0, 0))
    new = pl.BlockSpec((1, 1, LANES), lambda i, g, pt, sel: (i, 0, g))
    st = pl.BlockSpec((1, 1, LANES, WINDOW), lambda i, g, pt, sel: (i, g, 0, 0))
    pad8 = lambda x: jnp.concatenate([x, jnp.zeros_like(x)], axis=-2)
    tms = pad8(tabs["ms"].reshape(NSA_KV_HEADS, NSA_GROUP, 3, LANES).transpose(0, 2, 1, 3))
    tws = pad8(tabs["ws"].reshape(NSA_KV_HEADS, NSA_GROUP, WINDOW))
    eye = jnp.asarray(np.eye(LANES), BF16)
    grid_spec = pltpu.PrefetchScalarGridSpec(
        num_scalar_prefetch=2,
        grid=(db, NSA_KV_HEADS),
        in_specs=[grp(HEAD_DIM), grp(HEAD_DIM), grp(LANES), new, new, st] + [tile_spec(k) for k in range(SEL_TOPK)]
                 + [pl.BlockSpec((1, 3, 8, LANES), lambda i, g, pt, sel: (g, 0, 0, 0)),
                    pl.BlockSpec((1, 8, WINDOW), lambda i, g, pt, sel: (g, 0, 0)),
                    pl.BlockSpec(eye.shape, lambda i, g, pt, sel: (0, 0))],
        out_specs=[grp(HEAD_DIM), st],
    )
    o, st_new = pl.pallas_call(
        functools.partial(_nsa_sample_kernel, n_cache_blk=n_cache_blk),
        grid_spec=grid_spec,
        out_shape=[jax.ShapeDtypeStruct((db, NSA_KV_HEADS, 8, HEAD_DIM), F32),
                   jax.ShapeDtypeStruct(state_t.shape, F32)],
        compiler_params=_cparams("parallel", "arbitrary"),
        name="nsa_sample",
    )(page_table.reshape(-1), ids[:, :NSA_KV_HEADS, :SEL_TOPK].reshape(-1), q4, o_c, gates,
      sel_new.reshape(db, 1, gw), win_new.reshape(db, 1, gw), state_t,
      *([cache_t] * SEL_TOPK), tms, tws, eye)
    return o[:, :, :NSA_GROUP, :].reshape(db, N_HEADS * HEAD_DIM), st_new


def kernel(x_prompt, x_sample, cache_moba_kv, cache_nsa_cmp_kv, cache_nsa_sel_kv, state_nsa_win_kv, page_table,
           rel_bias, attn_norm, ffn_norm, moba_w_qkv, moba_q_norm, moba_k_norm, moba_w_o, nsa_w_in,
           nsa_gate_bias, nsa_q_norm, nsa_k_norm, nsa_cmp_pos, nsa_cmp_w1, nsa_cmp_b1, nsa_cmp_w2, nsa_w_o,
           ffn_w_up, ffn_w_down):
    b, s, d = x_prompt.shape
    db = x_sample.shape[0]
    assert x_sample.shape[1] == 1 and d == N_HEADS * HEAD_DIM
    past = page_table.shape[1] * PAGE_SIZE
    depth = attn_norm.shape[0]
    tabs = _bias_tables(rel_bias, s, past)
    hp = x_prompt.reshape(b * s, d)
    hs = x_sample.reshape(db, d)
    outs = {k: [] for k in ("moba_p", "moba_s", "cmp_p", "cmp_s", "sel_p", "sel_s", "win_p", "win_s")}
    kvshape = lambda n, g: (n, -1, g, 2, HEAD_DIM)
    for i in range(depth):
        j = i // 2
        if i % 2 == 0:
            qp, kvp_t, kvpb, kmean = _proj_moba(hp, attn_norm[i], moba_w_qkv[j], moba_q_norm[j], moba_k_norm[j],
                                                qdt=BF16, seq=s)
            qs, kvs, _ = _proj_moba(hs, attn_norm[i], moba_w_qkv[j], moba_q_norm[j], moba_k_norm[j], qdt=F32)
            op = _moba_prompt(qp, kvpb, kmean, tabs["moba"], b, s)
            cache_t = _slot_minor(cache_moba_kv[j], N_HEADS)
            ids = _moba_select(cache_t, page_table, qs, past)
            osm = _moba_sample(cache_t, page_table, ids, qs, kvs, tabs["ms"], past)
            outs["moba_p"].append(_token_major(kvp_t, b, s, N_HEADS))
            outs["moba_s"].append(kvs.reshape(kvshape(db, N_HEADS)))
            w_o = moba_w_o[j]
        else:
            qp, cp, sp_t, wp_t, spb, wpb, gp, cp_t = _proj_nsa(hp, attn_norm[i], nsa_w_in[j], nsa_gate_bias[j],
                                                               nsa_q_norm[j], nsa_k_norm[j], qdt=BF16, seq=s)
            qs, cs, ss, ws, _, _, gs = _proj_nsa(hs, attn_norm[i], nsa_w_in[j], nsa_gate_bias[j],
                                                 nsa_q_norm[j], nsa_k_norm[j], qdt=F32)
            cw = _compress_weights(nsa_cmp_pos[j], nsa_cmp_w1[j], nsa_cmp_b1[j], nsa_cmp_w2[j], nsa_k_norm[j, 0])
            kcvc_p = _compress_prompt(cp, cw, b, s)
            op = _nsa_prompt(qp, gp, kcvc_p, spb, wpb, tabs, b, s)
            kcvc_s = _compress_sample(cache_nsa_cmp_kv[j], page_table, cs, cw, past)
            pad8 = lambda x: jnp.concatenate([x, jnp.zeros_like(x)], axis=2)
            q4 = pad8(qs.reshape(db, NSA_KV_HEADS, NSA_GROUP, HEAD_DIM))
            q4w = jnp.pad(q4, ((0, 0), (0, 0), (0, 0), (0, LANES - HEAD_DIM)))
            g4 = pad8(gs.reshape(db, NSA_KV_HEADS, LANES)[:, :, :NSA_GROUP * N_BRANCH]
                      .reshape(db, NSA_KV_HEADS, NSA_GROUP, N_BRANCH))
            g4 = jnp.pad(g4, ((0, 0), (0, 0), (0, 0), (0, LANES - N_BRANCH)))
            o_c, sel_ids = _nsa_select(q4w, kcvc_s, tabs["cs"], past)
            osm, st_t = _nsa_sample(_slot_minor(cache_nsa_sel_kv[j], NSA_KV_HEADS),
                                    _slot_minor(state_nsa_win_kv[j], NSA_KV_HEADS), page_table, sel_ids, q4, o_c,
                                    g4, ss, ws, tabs, past)
            ws_new = st_t.reshape(db, NSA_KV_HEADS, 2, HEAD_DIM, WINDOW).transpose(0, 4, 1, 2, 3)
            gsh = (NSA_KV_HEADS, 2, HEAD_DIM)
            outs["cmp_p"].append(_token_major(cp_t, b, s, NSA_KV_HEADS))
            outs["cmp_s"].append(cs.reshape((db, 1) + gsh))
            outs["sel_p"].append(_token_major(sp_t, b, s, NSA_KV_HEADS))
            outs["sel_s"].append(ss.reshape((db, 1) + gsh))
            outs["win_p"].append(_token_major(wp_t, b, s, NSA_KV_HEADS)[:, s - min(WINDOW, s):])
            outs["win_s"].append(ws_new.reshape((db, WINDOW) + gsh))
            w_o = nsa_w_o[j]
        hp = _attn_out_mlp(hp, op, w_o, ffn_norm[i], ffn_w_up[i], ffn_w_down[i])
        hs = _attn_out_mlp(hs, osm, w_o, ffn_norm[i], ffn_w_up[i], ffn_w_down[i])
    stack = lambda k: jnp.stack(outs[k])
    return (hp.reshape(b, s, d), hs.reshape(db, 1, d), stack("moba_p"), stack("moba_s"), stack("cmp_p"),
            stack("cmp_s"), stack("sel_p"), stack("sel_s"), stack("win_p"), stack("win_s"))
```

```python
import functools
import math

import numpy as np
import jax
import jax.numpy as jnp
from jax import lax
from jax.experimental import pallas as pl
from jax.experimental.pallas import tpu as pltpu

F32 = jnp.float32
BF16 = jnp.bfloat16
I32 = jnp.int32

N_HEADS = 16
HEAD_DIM = 64
NORM_EPS = 1e-6
ATTN_SCALE = HEAD_DIM ** -0.5
REL_BUCKETS = 32
REL_MAX_DIST = 128
PAGE_SIZE = 128
MOBA_BLOCK = 256
MOBA_TOPK = 3
NSA_KV_HEADS = 4
NSA_GROUP = N_HEADS // NSA_KV_HEADS
CMP_BLOCK = 32
CMP_STRIDE = 16
CMP_HIDDEN = 2 * HEAD_DIM
SEL_BLOCK = 64
SEL_TOPK = 16
N_LOCAL_SEL = 2
WINDOW = 512
N_BRANCH = 3

LANES = 128
LANE_SHIFT = 7
SEL_SHIFT = 6
MASK_NEG = -(2.0 ** 100)
VMEM_LIMIT = 56 * 1024 * 1024
MOBA_UNROLL = 4
NSA_UNROLL = 4
NSA_TQ = 128
NSA_TK = 256
WIN_KEYS = WINDOW + NSA_TQ
CMP_NEAR = 32


def _cparams(*sem):
    return pltpu.CompilerParams(dimension_semantics=sem, vmem_limit_bytes=VMEM_LIMIT)


def _nt(a, b):
    return lax.dot_general(a, b, (((1,), (1,)), ((), ())), preferred_element_type=F32)


def _mm(a, b):
    return jnp.dot(a, b, preferred_element_type=F32)


def _split2(x):
    hi = x.astype(BF16)
    lo = (x - hi.astype(F32)).astype(BF16)
    return hi, lo


def _split3(x):
    hi = x.astype(BF16)
    r = x - hi.astype(F32)
    mid = r.astype(BF16)
    lo = (r - mid.astype(F32)).astype(BF16)
    return hi, mid, lo


def _rms_rows(x, g):
    return x * lax.rsqrt(jnp.mean(x * x, axis=-1, keepdims=True) + NORM_EPS) * g


def _group_norm(h, gmat_ref, gain, k_lanes_only):
    hi, lo = _split2(h * h)
    gm = gmat_ref[...]
    ss = _mm(hi, gm) + _mm(lo, gm)
    y = h * lax.rsqrt(ss * (1.0 / HEAD_DIM) + NORM_EPS) * gain
    if k_lanes_only:
        lane = lax.broadcasted_iota(I32, h.shape, 1)
        y = jnp.where((lane & (LANES - 1)) < HEAD_DIM, y, h)
    return y


def _softmax_rows(s):
    m = jnp.max(s, axis=-1, keepdims=True)
    m = jnp.where(m == -jnp.inf, 0.0, m)
    e = jnp.exp(s - m)
    return e * (1.0 / jnp.maximum(jnp.sum(e, axis=-1, keepdims=True), 1e-30))


def _bucket_of_dist():
    n = np.arange(REL_MAX_DIST + 1)
    max_exact = REL_BUCKETS // 2
    nf = np.maximum(n, 1).astype(np.float32)
    large = max_exact + (np.log(nf / np.float32(max_exact)) / np.float32(math.log(REL_MAX_DIST / max_exact))
                         * np.float32(REL_BUCKETS - max_exact)).astype(np.int32)
    large = np.minimum(large, REL_BUCKETS - 1)
    return np.where(n < max_exact, n, large).astype(np.int32)


def _dist_tables(seq, past):
    c = lambda d: np.clip(d, 0, REL_MAX_DIST)
    ncp = seq // CMP_STRIDE
    i256 = np.arange(MOBA_BLOCK)[:, None]
    j256 = np.arange(MOBA_BLOCK)[None, :]
    q128 = np.arange(NSA_TQ)[None, :]
    t = {}
    t["moba"] = np.stack([c(j256 - i256), c(MOBA_BLOCK + j256 - i256),
                          np.full((MOBA_BLOCK, MOBA_BLOCK), REL_MAX_DIST)])
    t["sel"] = np.stack([c(NSA_TQ * v + q128 - np.arange(NSA_TK)[:, None]) for v in range(4)])
    t["win"] = c(q128 + WINDOW - np.arange(WIN_KEYS)[:, None])
    mrow = np.arange(ncp)[:, None]
    near = q128 - CMP_STRIDE * (mrow - CMP_NEAR // 2) - (CMP_BLOCK - 1)
    t["cmp"] = np.where(mrow < CMP_NEAR, c(near), REL_MAX_DIST)
    r128 = np.arange(LANES)
    t["ms"] = np.stack([c(LANES - r128), np.full(LANES, REL_MAX_DIST), np.zeros(LANES, np.int64)])
    nrow = _cmp_rows(past)
    t["cs"] = c(past - CMP_STRIDE * np.arange(nrow) - (CMP_BLOCK - 1))
    t["ws"] = c(WINDOW - np.arange(WINDOW))
    return t


def _cmp_rows(past):
    return -(-(past // CMP_STRIDE + 4) // 16) * 16


def _tab_kernel(idx_ref, rb_ref, o_ref):
    idx = idx_ref[...]
    b = lax.broadcasted_iota(I32, (REL_BUCKETS, idx.shape[1]), 0)
    oh = jnp.where(b == idx, 1.0, 0.0).astype(BF16)
    o_ref[...] = _mm(rb_ref[0], oh) + _mm(rb_ref[1], oh) + _mm(rb_ref[2], oh)


def _bias_tables(rel_bias, seq, past):
    pats = _dist_tables(seq, past)
    bucket = _bucket_of_dist()
    chunk = 16384
    flat, spans, off = [], {}, 0
    for name, d in pats.items():
        n = d.size
        pad = -n % LANES
        flat.append(bucket[d.reshape(-1)])
        flat.append(np.zeros(pad, np.int32))
        spans[name] = (off, n, d.shape)
        off += n + pad
    total = -(-off // chunk) * chunk
    flat.append(np.zeros(total - off, np.int32))
    idx = jnp.asarray(np.concatenate(flat).astype(np.int32)).reshape(1, total)
    rb3 = jnp.stack(_split3(rel_bias.T.astype(F32)))
    tab = pl.pallas_call(
        _tab_kernel,
        grid=(total // chunk,),
        in_specs=[pl.BlockSpec((1, chunk), lambda i: (0, i)),
                  pl.BlockSpec((3, N_HEADS, REL_BUCKETS), lambda i: (0, 0, 0))],
        out_specs=pl.BlockSpec((N_HEADS, chunk), lambda i: (0, i)),
        out_shape=jax.ShapeDtypeStruct((N_HEADS, total), F32),
        compiler_params=_cparams("parallel"),
        name="bias_tables",
    )(idx, rb3)
    return {name: tab[:, o:o + n].reshape((N_HEADS,) + shp) for name, (o, n, shp) in spans.items()}


def _group_mats():
    i = np.arange(2 * LANES)
    g64 = (i[:, None] // HEAD_DIM == i[None, :] // HEAD_DIM)
    g128 = (i[:, None] // LANES == i[None, :] // LANES) & ((i[:, None] % LANES) < HEAD_DIM)
    return jnp.asarray(g64, BF16), jnp.asarray(g128, BF16)


def _interleave_gain(g, n):
    return jnp.tile(jnp.concatenate([g.astype(F32), jnp.ones((HEAD_DIM,), F32)]), n).reshape(1, n * LANES)


def _store_rows_or_columns(ref, sl, x):
    if len(ref.shape) == 3:
        ref[0, sl, :] = x.T.astype(ref.dtype)
    else:
        ref[:, sl] = x.astype(ref.dtype)


def _token_major(x_t, b, s, n_kv):
    return x_t.reshape(b, n_kv, 2, HEAD_DIM, s).transpose(0, 4, 1, 2, 3)


def _proj_moba_kernel(x_ref, an_ref, wq_ref, wkv_ref, qg_ref, kg_ref, g64_ref, g128_ref,
                      q_ref, kv_ref, kvb_ref, *rest, n_mean):
    xn = _rms_rows(x_ref[...], an_ref[...]).astype(BF16)
    cw = 2 * LANES
    for c in range(wq_ref.shape[1] // cw):
        sl = slice(c * cw, (c + 1) * cw)
        h = _mm(xn, wq_ref[:, sl])
        q = _group_norm(h, g64_ref, qg_ref[:, sl], False)
        _store_rows_or_columns(q_ref, sl, q)
    for c in range(wkv_ref.shape[1] // cw):
        sl = slice(c * cw, (c + 1) * cw)
        h = _mm(xn, wkv_ref[:, sl])
        kv = _group_norm(h, g128_ref, kg_ref[:, sl], True)
        _store_rows_or_columns(kv_ref, sl, kv)
        kvb_ref[:, sl] = kv.astype(BF16)
        if n_mean:
            km_ref, kvtb_ref = rest
            _store_rows_or_columns(kvtb_ref, sl, kv)
            for r in range(n_mean):
                km_ref[0, r:r + 1, sl] = jnp.mean(kv[r * MOBA_BLOCK:(r + 1) * MOBA_BLOCK], axis=0, keepdims=True)


def _proj_moba(x, an, w_qkv, q_gain, k_gain, *, qdt, seq=None):
    m, d = x.shape
    tm = min(m, 512)
    with_mean = seq is not None
    g64, g128 = _group_mats()
    w3 = w_qkv.reshape(d, 3, N_HEADS, HEAD_DIM)
    wkv = jnp.stack([w3[:, 1], w3[:, 2]], axis=2).reshape(d, 2 * d).astype(BF16)
    wq = w3[:, 0].reshape(d, d).astype(BF16)
    qg = jnp.tile(q_gain.astype(F32), N_HEADS).reshape(1, d)
    kg = _interleave_gain(k_gain, N_HEADS)
    qn = wq.shape[1]
    n_mean = tm // MOBA_BLOCK if with_mean else 0
    const = lambda i: (0, 0)
    if seq is None:
        q_shape, q_spec = (m, qn), pl.BlockSpec((tm, qn), lambda i: (i, 0))
        kv_shape, kv_spec = (m, 2 * d), pl.BlockSpec((tm, 2 * d), lambda i: (i, 0))
    else:
        nt = seq // tm
        q_shape, q_spec = (m // seq, qn, seq), pl.BlockSpec((1, qn, tm), lambda i: (i // nt, 0, i % nt))
        kv_shape, kv_spec = (m // seq, 2 * d, seq), pl.BlockSpec((1, 2 * d, tm), lambda i: (i // nt, 0, i % nt))
    out_shape = [jax.ShapeDtypeStruct(q_shape, qdt), jax.ShapeDtypeStruct(kv_shape, F32),
                 jax.ShapeDtypeStruct((m, 2 * d), BF16)]
    out_specs = [q_spec, kv_spec, pl.BlockSpec((tm, 2 * d), lambda i: (i, 0))]
    if n_mean:
        out_shape += [jax.ShapeDtypeStruct((m // tm, n_mean, 2 * d), F32), jax.ShapeDtypeStruct(kv_shape, BF16)]
        out_specs += [pl.BlockSpec((1, n_mean, 2 * d), lambda i: (i, 0, 0)), kv_spec]
    return pl.pallas_call(
        functools.partial(_proj_moba_kernel, n_mean=n_mean),
        grid=(m // tm,),
        in_specs=[pl.BlockSpec((tm, d), lambda i: (i, 0)), pl.BlockSpec((1, d), const),
                  pl.BlockSpec((d, qn), const), pl.BlockSpec((d, 2 * d), const),
                  pl.BlockSpec((1, qn), const), pl.BlockSpec((1, 2 * d), const),
                  pl.BlockSpec(g64.shape, const), pl.BlockSpec(g128.shape, const)],
        out_specs=out_specs, out_shape=out_shape,
        compiler_params=_cparams("parallel"),
        name="proj_moba",
    )(x, an.reshape(1, d).astype(F32), wq, wkv, qg, kg, g64, g128)


def _proj_nsa_kernel(x_ref, an_ref, wq_ref, wkv_ref, wg_ref, gb_ref, qg_ref, kg_ref, g64_ref, g128_ref,
                     q_ref, cmp_ref, sel_ref, win_ref, selb_ref, winb_ref, gate_ref, *rest):
    cmp_t_ref, selb_t_ref, winb_t_ref = rest if rest else (None, None, None)
    xn = _rms_rows(x_ref[...], an_ref[...]).astype(BF16)
    cw = 2 * LANES
    for c in range(wq_ref.shape[1] // cw):
        sl = slice(c * cw, (c + 1) * cw)
        h = _mm(xn, wq_ref[:, sl])
        _store_rows_or_columns(q_ref, sl, _group_norm(h, g64_ref, qg_ref[:, sl], False))
    per_branch = NSA_KV_HEADS * LANES // cw
    for c in range(wkv_ref.shape[1] // cw):
        sl = slice(c * cw, (c + 1) * cw)
        br, cc = divmod(c, per_branch)
        osl = slice(cc * cw, (cc + 1) * cw)
        h = _mm(xn, wkv_ref[:, sl])
        if br == 0:
            cmp_ref[:, osl] = h
            if cmp_t_ref is not None:
                _store_rows_or_columns(cmp_t_ref, osl, h)
        else:
            kv = _group_norm(h, g128_ref, kg_ref[:, sl], True)
            o32, o16, o16t = (sel_ref, selb_ref, selb_t_ref) if br == 1 else (win_ref, winb_ref, winb_t_ref)
            _store_rows_or_columns(o32, osl, kv)
            o16[:, osl] = kv.astype(BF16)
            if o16t is not None:
                _store_rows_or_columns(o16t, osl, kv)
    hg = _mm(xn, wg_ref[...]) + gb_ref[...]
    _store_rows_or_columns(gate_ref, slice(0, hg.shape[1]), 1.0 / (1.0 + jnp.exp(-hg)))


def _proj_nsa(x, an, w_in, gate_bias, q_gain, k_gain, *, qdt, seq=None):
    m, d = x.shape
    tm = min(m, 512)
    g64, g128 = _group_mats()
    kvd = N_BRANCH * NSA_KV_HEADS * LANES
    gw = NSA_KV_HEADS * LANES
    ng = NSA_GROUP * N_BRANCH
    qg = jnp.tile(q_gain.astype(F32), N_HEADS).reshape(1, d)
    wq = w_in[:, :d].astype(BF16)
    wkv = w_in[:, d:d + kvd].astype(BF16)
    wg = jnp.zeros((d, NSA_KV_HEADS, LANES), F32).at[:, :, :ng].set(
        w_in[:, d + kvd:].reshape(d, NSA_KV_HEADS, ng)).reshape(d, gw).astype(BF16)
    gb = jnp.zeros((NSA_KV_HEADS, LANES), F32).at[:, :ng].set(
        gate_bias.astype(F32).reshape(NSA_KV_HEADS, ng)).reshape(1, gw)
    kg = jnp.concatenate([jnp.ones((1, gw), F32), _interleave_gain(k_gain[1], NSA_KV_HEADS),
                          _interleave_gain(k_gain[2], NSA_KV_HEADS)], axis=1)
    qn = wq.shape[1]
    const = lambda i: (0, 0)
    row = lambda n: pl.BlockSpec((tm, n), lambda i: (i, 0))
    rows32 = jax.ShapeDtypeStruct((m, gw), F32)
    if seq is None:
        q_shape, q_spec = jax.ShapeDtypeStruct((m, qn), qdt), row(qn)
        kv_shape, kv_spec, gate_shape, gate_spec, extra_shape, extra_spec = rows32, row(gw), rows32, row(gw), [], []
    else:
        nt = seq // tm
        cols = lambda n: pl.BlockSpec((1, n, tm), lambda i: (i // nt, 0, i % nt))
        q_shape, q_spec = jax.ShapeDtypeStruct((m // seq, qn, seq), qdt), cols(qn)
        kv_shape, kv_spec = jax.ShapeDtypeStruct((m // seq, gw, seq), F32), cols(gw)
        gate_shape, gate_spec = kv_shape, kv_spec
        extra_shape = [kv_shape] + [jax.ShapeDtypeStruct((m // seq, gw, seq), BF16)] * 2
        extra_spec = [kv_spec] * 3
    return pl.pallas_call(
        _proj_nsa_kernel,
        grid=(m // tm,),
        in_specs=[row(d), pl.BlockSpec((1, d), const), pl.BlockSpec((d, qn), const),
                  pl.BlockSpec((d, kvd), const), pl.BlockSpec((d, gw), const), pl.BlockSpec((1, gw), const),
                  pl.BlockSpec((1, qn), const), pl.BlockSpec((1, kvd), const),
                  pl.BlockSpec(g64.shape, const), pl.BlockSpec(g128.shape, const)],
        out_specs=[q_spec, row(gw), kv_spec, kv_spec, row(gw), row(gw), gate_spec] + extra_spec,
        out_shape=[q_shape, rows32, kv_shape, kv_shape]
                  + [jax.ShapeDtypeStruct((m, gw), BF16)] * 2 + [gate_shape] + extra_shape,
        compiler_params=_cparams("parallel"),
        name="proj_nsa",
    )(x, an.reshape(1, d).astype(F32), wq, wkv, wg, gb, qg, kg, g64, g128)


def _mlp_kernel(h_ref, o_ref, wo_ref, g_ref, wup_ref, wdn_ref, out_ref, h1_s, xn_s, acc_s):
    f = pl.program_id(1)

    @pl.when(f == 0)
    def _():
        h1 = h_ref[...] + _mm(o_ref[...], wo_ref[...])
        h1_s[...] = h1
        xn_s[...] = _rms_rows(h1, g_ref[...]).astype(BF16)
        acc_s[...] = jnp.zeros_like(acc_s)

    u = _mm(xn_s[...], wup_ref[...])
    u = jnp.square(jnp.maximum(u, 0.0)).astype(BF16)
    acc_s[...] += _mm(u, wdn_ref[...])

    @pl.when(f == pl.num_programs(1) - 1)
    def _():
        out_ref[...] = h1_s[...] + acc_s[...]


def _attn_out_mlp(h, o, w_o, fn, w_up, w_down):
    m, d = h.shape
    dff = w_up.shape[1]
    tm = min(m, 512)
    tf = 512
    return pl.pallas_call(
        _mlp_kernel,
        grid=(m // tm, dff // tf),
        in_specs=[pl.BlockSpec((tm, d), lambda i, f: (i, 0)), pl.BlockSpec((tm, d), lambda i, f: (i, 0)),
                  pl.BlockSpec((d, d), lambda i, f: (0, 0)), pl.BlockSpec((1, d), lambda i, f: (0, 0)),
                  pl.BlockSpec((d, tf), lambda i, f: (0, f)), pl.BlockSpec((tf, d), lambda i, f: (f, 0))],
        out_specs=pl.BlockSpec((tm, d), lambda i, f: (i, 0)),
        out_shape=jax.ShapeDtypeStruct((m, d), F32),
        scratch_shapes=[pltpu.VMEM((tm, d), F32), pltpu.VMEM((tm, d), BF16), pltpu.VMEM((tm, d), F32)],
        compiler_params=_cparams("parallel", "arbitrary"),
        name="attn_out_mlp",
    )(h, o.astype(BF16), w_o.astype(BF16), fn.reshape(1, d).astype(F32), w_up.astype(BF16), w_down.astype(BF16))


def _topk_rows(x, k, on_pick):
    n = x.shape[0]
    row = lax.broadcasted_iota(I32, x.shape, 0)
    for _ in range(k):
        m = jnp.max(x, axis=0, keepdims=True)
        idx = jnp.min(jnp.where(x == m, row, n), axis=0, keepdims=True)
        hit = row == idx
        on_pick(hit, m)
        x = jnp.where(hit, -jnp.inf, x)


def _online_softmax_step(m_ref, l_ref, acc_ref, slot, tiles, key_axis=1):
    m_old = m_ref[slot]
    mx = jnp.max(tiles[0][0], axis=key_axis, keepdims=True)
    for s, _ in tiles[1:]:
        mx = jnp.maximum(mx, jnp.max(s, axis=key_axis, keepdims=True))
    mn = jnp.maximum(m_old, mx)
    a = jnp.exp(m_old - mn)
    l = a * l_ref[slot]
    acc = a * acc_ref[slot]
    for s, v in tiles:
        p = jnp.exp(s - mn)
        l = l + jnp.sum(p, axis=key_axis, keepdims=True)
        acc = acc + (_mm(p.astype(BF16), v) if key_axis == 1 else _mm(v, p.astype(BF16)))
    m_ref[slot] = mn
    l_ref[slot] = l
    acc_ref[slot] = acc


def _unrolled_range(n, unroll, visit):
    rem = lax.rem(n, unroll)
    done = 0
    size = 1
    while size < unroll:
        take = jnp.bitwise_and(rem, size)

        @pl.when(take != 0)
        def _(done=done, size=size):
            visit([done + k for k in range(size)])

        done = done + take
        size *= 2

    def body(i, carry):
        first = rem + unroll * i
        visit([first + k for k in range(unroll)])
        return carry

    lax.fori_loop(0, n // unroll, body, 0)


def _softmax_state_init(m_ref, l_ref, acc_ref):
    m_ref[...] = jnp.full(m_ref.shape, -jnp.inf, F32)
    l_ref[...] = jnp.zeros(l_ref.shape, F32)
    acc_ref[...] = jnp.zeros(acc_ref.shape, F32)


def _moba_prompt_kernel(qt_ref, kv_ref, kvt_ref, km_ref, tb_ref, o_ref, *state, nb):
    cur = pl.program_id(2)
    tq = MOBA_BLOCK
    key_lane = lax.broadcasted_iota(I32, (tq, LANES), 1)
    causal = lax.broadcasted_iota(I32, (tq, tq), 0) <= lax.broadcasted_iota(I32, (tq, tq), 1)
    rown = lax.broadcasted_iota(I32, (nb, tq), 0)
    heads = [state[0:3], state[3:6]]
    for st in heads:
        _softmax_state_init(*st)
    qzs, qaugs = [], []
    for hh in range(2):
        qt = qt_ref[0, hh * HEAD_DIM:(hh + 1) * HEAD_DIM, :] * jnp.asarray(ATTN_SCALE, BF16)
        qz = jnp.concatenate([qt, jnp.zeros_like(qt)], axis=0)
        km_hi, km_lo = _split2(km_ref[0, :, hh * LANES:(hh + 1) * LANES])
        gate = _mm(km_hi, qz) + _mm(km_lo, qz)
        state = [jnp.full((nb, tq), MASK_NEG, F32)]

        def pick(hit, m, state=state):
            state[0] = jnp.where(hit, jnp.where(m > -jnp.inf, 0.0, state[0]), state[0])

        _topk_rows(jnp.where(rown < cur, gate, -jnp.inf), MOBA_TOPK, pick)
        negt = jnp.concatenate([state[0], jnp.zeros((LANES - nb, tq), F32)], axis=0).astype(BF16)
        qzs.append(qz)
        qaugs.append(jnp.concatenate([qz, negt], axis=0))

    def k_rows(hh, n):
        return kv_ref[0, pl.ds(pl.multiple_of(n * tq, tq), tq), hh * LANES:(hh + 1) * LANES]

    def v_cols(hh, n):
        return kvt_ref[0, hh * LANES + HEAD_DIM:(hh + 1) * LANES, pl.ds(pl.multiple_of(n * tq, tq), tq)]

    for hh in range(2):
        s = jnp.where(causal, _mm(k_rows(hh, cur), qzs[hh]) + tb_ref[hh, 0], -jnp.inf)
        _online_softmax_step(*heads[hh], 0, [(s, v_cols(hh, cur))], key_axis=0)

    def past_blocks(blocks):
        for hh in range(2):
            tiles = []
            for n in blocks:
                onehot = jnp.where(key_lane == n, 1.0, 0.0).astype(BF16)
                s = _mm(jnp.concatenate([k_rows(hh, n), onehot], axis=1), qaugs[hh])
                tiles.append((s + tb_ref[hh, jnp.where(cur - n == 1, 1, 2)], v_cols(hh, n)))
            _online_softmax_step(*heads[hh], 0, tiles, key_axis=0)

    _unrolled_range(cur, MOBA_UNROLL, past_blocks)
    o_t = jnp.concatenate([acc[0] / jnp.maximum(l[0], 1e-30) for _, l, acc in heads], axis=0)
    o_ref[...] = o_t.T.astype(o_ref.dtype)


def _moba_prompt(qt, kvb, kvt, kmean, tb, b, s):
    d = qt.shape[1]
    nb = s // MOBA_BLOCK
    assert s % MOBA_BLOCK == 0 and MOBA_TOPK <= nb <= LANES
    nq = s // MOBA_BLOCK
    return pl.pallas_call(
        functools.partial(_moba_prompt_kernel, nb=nb),
        grid=(b, N_HEADS // 2, nq),
        in_specs=[pl.BlockSpec((1, LANES, MOBA_BLOCK), lambda i, h, t: (i, h, t)),
                  pl.BlockSpec((1, s, 2 * LANES), lambda i, h, t: (i, 0, h)),
                  pl.BlockSpec((1, 2 * LANES, s), lambda i, h, t: (i, h, 0)),
                  pl.BlockSpec((1, nb, 2 * LANES), lambda i, h, t: (i, 0, h)),
                  pl.BlockSpec((2, 3, MOBA_BLOCK, MOBA_BLOCK), lambda i, h, t: (h, 0, 0, 0))],
        out_specs=pl.BlockSpec((MOBA_BLOCK, LANES), lambda i, h, t: (i * nq + t, h)),
        out_shape=jax.ShapeDtypeStruct((b * s, d), BF16),
        scratch_shapes=[pltpu.VMEM((1, 1, MOBA_BLOCK), F32), pltpu.VMEM((1, 1, MOBA_BLOCK), F32),
                        pltpu.VMEM((1, HEAD_DIM, MOBA_BLOCK), F32)] * 2,
        compiler_params=_cparams("parallel", "parallel", "arbitrary"),
        name="moba_prompt",
    )(qt, kvb.reshape(b, s, 2 * d), kvt, kmean.reshape(b, nb, 2 * d), tb)


def _gelu_tanh(x):
    return 0.5 * x * (1.0 + jnp.tanh(math.sqrt(2.0 / math.pi) * (x + 0.044715 * (x * x * x))))


def _compress_weights(cmp_pos, w1, b1, w2, k_gain0):
    half = CMP_BLOCK // 2
    z = jnp.zeros((half, HEAD_DIM, CMP_HIDDEN), F32)

    def first_layer(lo):
        wk, wv = w1[0, lo:lo + half], w1[1, lo:lo + half]
        top = jnp.concatenate([wk, z], axis=2)
        bot = jnp.concatenate([z, wv], axis=2)
        return jnp.concatenate([top, bot], axis=1).reshape(half * LANES, 2 * CMP_HIDDEN).astype(BF16)

    zz = jnp.zeros((CMP_HIDDEN, HEAD_DIM), F32)
    w2bd = jnp.concatenate([jnp.concatenate([w2[0], zz], axis=1),
                            jnp.concatenate([zz, w2[1]], axis=1)], axis=0).astype(BF16)
    pos_a = cmp_pos[:half].reshape(half, LANES).astype(F32)
    pos_b = cmp_pos[half:].reshape(half, LANES).astype(F32)
    return (pos_a, pos_b, first_layer(0), first_layer(half), b1.reshape(1, 2 * CMP_HIDDEN).astype(F32), w2bd,
            _interleave_gain(k_gain0, 1))


def _compress_tail(xa_s, xb_s, wa_ref, wb_ref, b1_ref, w2_ref, kg_ref):
    ha = _mm(xa_s[...], wa_ref[...])
    hb = _mm(xb_s[...], wb_ref[...])
    rows = ha.shape[0]
    h = ha + pltpu.roll(hb, rows - 1, 0) + b1_ref[...]
    out = _mm(_gelu_tanh(h).astype(BF16), w2_ref[...])
    lane = lax.broadcasted_iota(I32, out.shape, 1)
    is_k = lane < HEAD_DIM
    ss = jnp.sum(jnp.where(is_k, out * out, 0.0), axis=1, keepdims=True)
    kn = out * lax.rsqrt(ss * (1.0 / HEAD_DIM) + NORM_EPS) * kg_ref[...]
    return jnp.where(is_k, kn, out)


def _compress_prompt_kernel(*refs, nseg):
    x_refs = refs[:NSA_KV_HEADS]
    pa_ref, pb_ref, wa_ref, wb_ref, b1_ref, w2_ref, kg_ref, o_ref, ot_ref, xa_s, xb_s = refs[NSA_KV_HEADS:]
    half = CMP_BLOCK // 2
    for g in range(NSA_KV_HEADS):
        for p in range(half):
            v = x_refs[g][0, pl.ds(p, nseg, stride=CMP_STRIDE), :]
            xa_s[g * nseg:(g + 1) * nseg, p * LANES:(p + 1) * LANES] = (v + pa_ref[p:p + 1, :]).astype(BF16)
            xb_s[g * nseg:(g + 1) * nseg, p * LANES:(p + 1) * LANES] = (v + pb_ref[p:p + 1, :]).astype(BF16)
    res = _compress_tail(xa_s, xb_s, wa_ref, wb_ref, b1_ref, w2_ref, kg_ref)
    for g in range(NSA_KV_HEADS):
        rows = res[g * nseg:(g + 1) * nseg]
        o_ref[0, g] = rows.astype(BF16)
        ot_ref[0, g] = rows.T.astype(BF16)


def _compress_prompt(kv_cmp, cw, b, s):
    nseg = s // CMP_STRIDE
    gw = NSA_KV_HEADS * LANES
    kdim = (CMP_BLOCK // 2) * LANES
    const = lambda i: (0, 0)
    return pl.pallas_call(
        functools.partial(_compress_prompt_kernel, nseg=nseg),
        grid=(b,),
        in_specs=[pl.BlockSpec((1, s, LANES), functools.partial(lambda i, g: (i, 0, g), g=g))
                  for g in range(NSA_KV_HEADS)] + [pl.BlockSpec(w.shape, const) for w in cw],
        out_specs=[pl.BlockSpec((1, NSA_KV_HEADS, nseg, LANES), lambda i: (i, 0, 0, 0)),
                   pl.BlockSpec((1, NSA_KV_HEADS, LANES, nseg), lambda i: (i, 0, 0, 0))],
        out_shape=[jax.ShapeDtypeStruct((b, NSA_KV_HEADS, nseg, LANES), BF16),
                   jax.ShapeDtypeStruct((b, NSA_KV_HEADS, LANES, nseg), BF16)],
        scratch_shapes=[pltpu.VMEM((NSA_KV_HEADS * nseg, kdim), BF16)] * 2,
        compiler_params=_cparams("parallel"),
        name="compress_prompt",
    )(*([kv_cmp.reshape(b, s, gw)] * NSA_KV_HEADS), *cw)


CMP_PAGES = 4


def _compress_sample_kernel(pt_ref, *refs, nseg, nrow):
    x_refs = refs[:CMP_PAGES]
    (new_ref, pa_ref, pb_ref, wa_ref, wb_ref, b1_ref, w2_ref, kg_ref, o_ref,
     xa_s, xb_s, xt_s) = refs[CMP_PAGES:]
    j = pl.program_id(1)
    for pg in range(CMP_PAGES):
        for g in range(NSA_KV_HEADS):
            xt_s[pg, g] = x_refs[pg][0, g].T
    half = CMP_BLOCK // 2
    per_page = PAGE_SIZE // CMP_STRIDE
    per_step = CMP_PAGES * per_page
    tail = nrow - nseg

    @pl.when(j == 0)
    def _():
        first = lax.broadcasted_iota(I32, (tail, LANES), 0) == 0
        for g in range(NSA_KV_HEADS):
            rows = slice(g * nrow + nseg, (g + 1) * nrow)
            for p in range(half):
                v = jnp.zeros((tail, LANES), F32)
                if p == 0:
                    v = jnp.where(first, new_ref[0, :, g * LANES:(g + 1) * LANES], 0.0)
                xa_s[rows, p * LANES:(p + 1) * LANES] = (v + pa_ref[p:p + 1, :]).astype(BF16)
                xb_s[rows, p * LANES:(p + 1) * LANES] = (v + pb_ref[p:p + 1, :]).astype(BF16)

    for g in range(NSA_KV_HEADS):
        r0 = pl.multiple_of(g * nrow + j * per_step, 16)
        for p in range(half):
            v = jnp.concatenate([xt_s[pg, g, pl.ds(p, per_page, stride=CMP_STRIDE), :]
                                 for pg in range(CMP_PAGES)], axis=0)
            xa_s[pl.ds(r0, per_step), p * LANES:(p + 1) * LANES] = (v + pa_ref[p:p + 1, :]).astype(BF16)
            xb_s[pl.ds(r0, per_step), p * LANES:(p + 1) * LANES] = (v + pb_ref[p:p + 1, :]).astype(BF16)

    @pl.when(j == pl.num_programs(1) - 1)
    def _():
        res = _compress_tail(xa_s, xb_s, wa_ref, wb_ref, b1_ref, w2_ref, kg_ref)
        for g in range(NSA_KV_HEADS):
            o_ref[0, g] = res[g * nrow:(g + 1) * nrow]


def _compress_sample(cache, page_table, new_rows, cw, past):
    db, n_pages = page_table.shape
    assert n_pages % CMP_PAGES == 0 and (CMP_PAGES * PAGE_SIZE // CMP_STRIDE) % 16 == 0
    gw = NSA_KV_HEADS * LANES
    nseg = past // CMP_STRIDE
    nrow = _cmp_rows(past)
    kdim = (CMP_BLOCK // 2) * LANES
    const = lambda i, j, pt: (0, 0)
    grid_spec = pltpu.PrefetchScalarGridSpec(
        num_scalar_prefetch=1,
        grid=(db, n_pages // CMP_PAGES),
        in_specs=[pl.BlockSpec((1, NSA_KV_HEADS, LANES, PAGE_SIZE),
                               functools.partial(lambda i, j, pt, pg: (pt[i * n_pages + CMP_PAGES * j + pg], 0, 0, 0),
                                                 pg=pg))
                  for pg in range(CMP_PAGES)]
                 + [pl.BlockSpec((1, 1, gw), lambda i, j, pt: (i, 0, 0))]
                 + [pl.BlockSpec(w.shape, const) for w in cw],
        out_specs=pl.BlockSpec((1, NSA_KV_HEADS, nrow, LANES), lambda i, j, pt: (i, 0, 0, 0)),
        scratch_shapes=[pltpu.VMEM((NSA_KV_HEADS * nrow, kdim), BF16)] * 2
                       + [pltpu.VMEM((CMP_PAGES, NSA_KV_HEADS, PAGE_SIZE, LANES), F32)],
    )
    cview = _slot_minor(cache, NSA_KV_HEADS)
    return pl.pallas_call(
        functools.partial(_compress_sample_kernel, nseg=nseg, nrow=nrow),
        grid_spec=grid_spec,
        out_shape=jax.ShapeDtypeStruct((db, NSA_KV_HEADS, nrow, LANES), F32),
        compiler_params=_cparams("parallel", "arbitrary"),
        name="compress_sample",
    )(page_table.reshape(-1), *([cview] * CMP_PAGES), new_rows.reshape(db, 1, gw), *cw)


def _imp_matrix(n_sel, n_rows, n_cmp):
    ratio = SEL_BLOCK // CMP_STRIDE
    lead = CMP_BLOCK // CMP_STRIDE - 1
    j = np.arange(n_sel)[:, None]
    n = np.arange(n_rows)[None, :]
    return ((n >= ratio * j - lead) & (n <= ratio * j + ratio - 1) & (n < n_cmp))


def _nsa_prompt_kernel(qt_ref, gt_ref, kc_ref, kct_ref, ks_ref, kst_ref, kw_ref, kwt_ref, tsel_ref, tw_ref,
                       tc_ref, at_ref, o_ref, m_s, l_s, acc_s, *, n_sel, ncp):
    t = pl.program_id(2)
    tq, tk, grp = NSA_TQ, NSA_TK, NSA_GROUP
    cols = grp * tq
    p0 = t * tq
    scale = jnp.asarray(ATTN_SCALE, BF16)
    qt = jnp.concatenate([qt_ref[0, p * HEAD_DIM:(p + 1) * HEAD_DIM, :] * scale for p in range(grp)], axis=1)
    qz = jnp.concatenate([qt, jnp.zeros_like(qt)], axis=0)
    qoff = jnp.bitwise_and(lax.broadcasted_iota(I32, (1, cols), 1), tq - 1)

    shift = lax.rem(t * (tq // CMP_STRIDE) - CMP_NEAR // 2 + ncp, ncp)
    c_end = lax.broadcasted_iota(I32, (ncp, cols), 0) * CMP_STRIDE + (CMP_BLOCK - 1)
    s_c = jnp.where(p0 + qoff >= c_end, _mm(kc_ref[0, 0], qz) + pltpu.roll(tc_ref[0], shift, 0), -jnp.inf)
    m_c = jnp.max(s_c, axis=0, keepdims=True)
    e_c = jnp.exp(s_c - jnp.where(m_c == -jnp.inf, 0.0, m_c))
    p_c = e_c * (1.0 / jnp.maximum(jnp.sum(e_c, axis=0, keepdims=True), 1e-30))
    o_c = _mm(kct_ref[0, 0, HEAD_DIM:, :], p_c.astype(BF16))
    p_sum = p_c[:, 0:tq]
    for p in range(1, grp):
        p_sum = p_sum + p_c[:, p * tq:(p + 1) * tq]
    at = at_ref[...]
    imp = sum(_mm(at, part) for part in _split3(p_sum))

    blk = lax.broadcasted_iota(I32, (n_sel, tq), 0)
    cur = jnp.right_shift(p0 + lax.broadcasted_iota(I32, (n_sel, tq), 1), SEL_SHIFT)
    valid = blk <= cur
    forced = (blk == 0) | (blk >= cur - (N_LOCAL_SEL - 1))
    score = jnp.where(valid, jnp.where(forced, jnp.inf, imp), -jnp.inf)
    state = [jnp.full((n_sel, tq), MASK_NEG, F32)]

    def pick(hit, m):
        state[0] = jnp.where(hit, jnp.where(m > -jnp.inf, 0.0, state[0]), state[0])

    _topk_rows(score, SEL_TOPK, pick)
    negt = state[0]
    if n_sel < LANES:
        negt = jnp.concatenate([negt, jnp.zeros((LANES - n_sel, tq), F32)], axis=0)
    negt = negt.astype(BF16)
    qaug = jnp.concatenate([qz, jnp.concatenate([negt] * grp, axis=1)], axis=0)

    kj = lax.broadcasted_iota(I32, (tk, LANES), 0)
    kl = lax.broadcasted_iota(I32, (tk, LANES), 1)

    def sel_tile(kt, variant):
        at_kt = pl.ds(pl.multiple_of(kt * tk, tk), tk)
        onehot = jnp.where(kl == kt * (tk // SEL_BLOCK) + jnp.right_shift(kj, SEL_SHIFT), 1.0, 0.0).astype(BF16)
        s = _mm(jnp.concatenate([ks_ref[0, at_kt, :], onehot], axis=1), qaug) + tsel_ref[0, variant]
        return s, kst_ref[0, HEAD_DIM:, at_kt]

    _softmax_state_init(m_s, l_s, acc_s)
    kt_d = t // (tk // tq)
    off_d = lax.rem(t, tk // tq) * tq
    s, v_t = sel_tile(kt_d, off_d // tq)
    ok_d = lax.broadcasted_iota(I32, (tk, cols), 0) <= qoff + off_d
    _online_softmax_step(m_s, l_s, acc_s, 0, [(jnp.where(ok_d, s, -jnp.inf), v_t)], key_axis=0)

    def past_tiles(kts):
        _online_softmax_step(m_s, l_s, acc_s, 0,
                             [sel_tile(kt, jnp.minimum((p0 - kt * tk) // tq, 3)) for kt in kts], key_axis=0)

    _unrolled_range(kt_d, NSA_UNROLL, past_tiles)
    o_s = acc_s[0] / jnp.maximum(l_s[0], 1e-30)

    at_w = pl.ds(pl.multiple_of(p0, tq), WIN_KEYS)
    wj = lax.broadcasted_iota(I32, (WIN_KEYS, cols), 0)
    ok_w = (wj > qoff) & (wj <= qoff + WINDOW) & (p0 + wj >= WINDOW)
    s_w = jnp.where(ok_w, _mm(kw_ref[0, at_w, :], qz) + tw_ref[0], -jnp.inf)
    m_w = jnp.max(s_w, axis=0, keepdims=True)
    e_w = jnp.exp(s_w - jnp.where(m_w == -jnp.inf, 0.0, m_w))
    o_w = (_mm(kwt_ref[0, HEAD_DIM:, at_w], e_w.astype(BF16))
           * (1.0 / jnp.maximum(jnp.sum(e_w, axis=0, keepdims=True), 1e-30)))

    gt = gt_ref[0]
    per_head = []
    for p in range(grp):
        c = slice(p * tq, (p + 1) * tq)
        g = [gt[N_BRANCH * p + k:N_BRANCH * p + k + 1, :] for k in range(N_BRANCH)]
        per_head.append(g[0] * o_c[:, c] + g[1] * o_s[:, c] + g[2] * o_w[:, c])
    o_ref[...] = jnp.concatenate(per_head, axis=0).T.astype(o_ref.dtype)


def _heads_on_lanes(tab, lead):
    n = len(lead)
    x = tab.reshape((NSA_KV_HEADS, NSA_GROUP) + tab.shape[1:])
    perm = (0,) + tuple(range(2, 3 + n)) + (1, 3 + n)
    x = x.transpose(perm)
    return x.reshape(x.shape[:2 + n] + (NSA_GROUP * tab.shape[-1],))


def _nsa_prompt(qt, gt, kcvc, kcvc_t, selb, selb_t, winb, winb_t, tabs, b, s):
    d = qt.shape[1]
    gw = NSA_KV_HEADS * LANES
    ncp = s // CMP_STRIDE
    n_sel = s // SEL_BLOCK
    assert s % NSA_TK == 0 and SEL_TOPK <= n_sel <= LANES and ncp >= CMP_NEAR
    nq = s // NSA_TQ
    at = jnp.asarray(_imp_matrix(n_sel, ncp, ncp - 1), BF16)
    winp = jnp.pad(winb.reshape(b, s, gw), ((0, 0), (WINDOW, 0), (0, 0)))
    winp_t = jnp.pad(winb_t, ((0, 0), (0, 0), (WINDOW, 0)))
    gq = NSA_GROUP
    cols = gq * NSA_TQ
    tsel = _heads_on_lanes(tabs["sel"], (4,))
    tw = _heads_on_lanes(tabs["win"], ())
    tc = _heads_on_lanes(tabs["cmp"], ())
    return pl.pallas_call(
        functools.partial(_nsa_prompt_kernel, n_sel=n_sel, ncp=ncp),
        grid=(b, NSA_KV_HEADS, nq),
        in_specs=[pl.BlockSpec((1, gq * HEAD_DIM, NSA_TQ), lambda i, g, t: (i, g, t)),
                  pl.BlockSpec((1, LANES, NSA_TQ), lambda i, g, t: (i, g, t)),
                  pl.BlockSpec((1, 1, ncp, LANES), lambda i, g, t: (i, g, 0, 0)),
                  pl.BlockSpec((1, 1, LANES, ncp), lambda i, g, t: (i, g, 0, 0)),
                  pl.BlockSpec((1, s, LANES), lambda i, g, t: (i, 0, g)),
                  pl.BlockSpec((1, LANES, s), lambda i, g, t: (i, g, 0)),
                  pl.BlockSpec((1, s + WINDOW, LANES), lambda i, g, t: (i, 0, g)),
                  pl.BlockSpec((1, LANES, s + WINDOW), lambda i, g, t: (i, g, 0)),
                  pl.BlockSpec((1, 4, NSA_TK, cols), lambda i, g, t: (g, 0, 0, 0)),
                  pl.BlockSpec((1, WIN_KEYS, cols), lambda i, g, t: (g, 0, 0)),
                  pl.BlockSpec((1, ncp, cols), lambda i, g, t: (g, 0, 0)),
                  pl.BlockSpec(at.shape, lambda i, g, t: (0, 0))],
        out_specs=pl.BlockSpec((NSA_TQ, gq * HEAD_DIM), lambda i, g, t: (i * nq + t, g)),
        out_shape=jax.ShapeDtypeStruct((b * s, d), BF16),
        scratch_shapes=[pltpu.VMEM((1, 1, cols), F32), pltpu.VMEM((1, 1, cols), F32),
                        pltpu.VMEM((1, HEAD_DIM, cols), F32)],
        compiler_params=_cparams("parallel", "parallel", "arbitrary"),
        name="nsa_prompt",
    )(qt, gt, kcvc, kcvc_t, selb.reshape(b, s, gw), selb_t, winp, winp_t, tsel, tw, tc, at)


def _topk_lanes(x, k):
    ax = x.ndim - 1
    lane = lax.broadcasted_iota(I32, x.shape, ax)
    oshape = x.shape[:-1] + (LANES,)
    out_lane = lax.broadcasted_iota(I32, oshape, ax)
    ids = jnp.zeros(oshape, I32)
    for r in range(k):
        m = jnp.max(x, axis=ax, keepdims=True)
        idx = jnp.min(jnp.where(x == m, lane, x.shape[ax]), axis=ax, keepdims=True)
        ids = jnp.where(out_lane == r, idx, ids)
        x = jnp.where(lane == idx, -jnp.inf, x)
    return ids


def _slot_minor(cache, n_kv):
    pages, slots = cache.shape[:2]
    return jnp.transpose(cache, (0, 2, 3, 4, 1)).reshape(pages, n_kv, LANES, slots)


MOBA_SELECT_PAGES = 4


def _moba_select_kernel(pt_ref, *refs, nb):
    x_refs = refs[:MOBA_SELECT_PAGES]
    q_ref, ids_ref, ksum_s = refs[MOBA_SELECT_PAGES:]
    j = pl.program_id(1)
    ppb = MOBA_BLOCK // PAGE_SIZE
    per_step = MOBA_SELECT_PAGES // ppb

    @pl.when(j == 0)
    def _():
        ksum_s[...] = jnp.zeros_like(ksum_s)

    lane = lax.broadcasted_iota(I32, ksum_s.shape, 2)
    acc = ksum_s[...]
    for bi in range(per_step):
        x = x_refs[bi * ppb][0]
        for pg in range(1, ppb):
            x = x + x_refs[bi * ppb + pg][0]
        col = jnp.sum(x, axis=2, keepdims=True)
        acc = acc + jnp.where(lane == j * per_step + bi, col, 0.0)
    ksum_s[...] = acc

    @pl.when(j == pl.num_programs(1) - 1)
    def _():
        km = ksum_s[...] * (1.0 / MOBA_BLOCK)
        gate = sum(jnp.einsum("hqd,hdn->hqn", a, b_, preferred_element_type=F32)
                   for a in _split3(q_ref[0]) for b_ in _split3(km))
        blk = lax.broadcasted_iota(I32, gate.shape, 2)
        ids_ref[0] = _topk_lanes(jnp.where(blk < nb, gate, -jnp.inf), MOBA_TOPK)


def _moba_select(cache_t, page_table, q, past):
    db, n_pages = page_table.shape
    nb = past // MOBA_BLOCK
    assert past % MOBA_BLOCK == 0 and MOBA_TOPK <= nb <= LANES and n_pages % MOBA_SELECT_PAGES == 0
    q8 = jnp.pad(q.reshape(db, N_HEADS, 1, HEAD_DIM), ((0, 0), (0, 0), (0, 7), (0, 0)))
    grid_spec = pltpu.PrefetchScalarGridSpec(
        num_scalar_prefetch=1,
        grid=(db, n_pages // MOBA_SELECT_PAGES),
        in_specs=[pl.BlockSpec((1, N_HEADS, HEAD_DIM, PAGE_SIZE),
                               functools.partial(lambda i, j, pt, pg: (pt[i * n_pages + MOBA_SELECT_PAGES * j + pg],
                                                                       0, 0, 0), pg=pg))
                  for pg in range(MOBA_SELECT_PAGES)]
                 + [pl.BlockSpec((1, N_HEADS, 8, HEAD_DIM), lambda i, j, pt: (i, 0, 0, 0))],
        out_specs=pl.BlockSpec((1, N_HEADS, 8, LANES), lambda i, j, pt: (i, 0, 0, 0)),
        scratch_shapes=[pltpu.VMEM((N_HEADS, HEAD_DIM, LANES), F32)],
    )
    ids = pl.pallas_call(
        functools.partial(_moba_select_kernel, nb=nb),
        grid_spec=grid_spec,
        out_shape=jax.ShapeDtypeStruct((db, N_HEADS, 8, LANES), I32),
        compiler_params=_cparams("parallel", "arbitrary"),
        name="moba_select",
    )(page_table.reshape(-1), *([cache_t] * MOBA_SELECT_PAGES), q8)
    return ids[:, :, 0, :MOBA_TOPK]


def _pad_rows(x, rows):
    return jnp.concatenate([x, jnp.zeros((rows - x.shape[0],) + x.shape[1:], x.dtype)], axis=0)


def _moba_sample_kernel(pt_ref, ids_ref, q_ref, new_ref, *rest, n_tiles, last_blk, pages_per_block):
    tiles = rest[:n_tiles]
    tab_ref, o_ref = rest[n_tiles], rest[n_tiles + 1]
    i, h = pl.program_id(0), pl.program_id(1)
    q8 = q_ref[0, 0] * ATTN_SCALE
    q8b = q8.astype(BF16)
    new = new_ref[0, 0]
    k_new, v_new = new[:, :HEAD_DIM], new[:, HEAD_DIM:]
    scores, vts = [], []
    for ti in range(n_tiles):
        k, pp = divmod(ti, pages_per_block)
        blk = ids_ref[(i * N_HEADS + h) * MOBA_TOPK + k]
        near = jnp.logical_and(blk == last_blk, pp == pages_per_block - 1)
        tile = tiles[ti][0, 0]
        vts.append(tile[HEAD_DIM:].astype(BF16))
        scores.append(_mm(q8b, tile[:HEAD_DIM].astype(BF16))
                      + jnp.where(near, tab_ref[0, 0:1, :], tab_ref[0, 1:2, :]))
    s = jnp.concatenate(scores, axis=1)
    s_self = jnp.sum(q8 * k_new, axis=1, keepdims=True) + tab_ref[0, 2:3, 0:1]
    m = jnp.maximum(jnp.max(s, axis=1, keepdims=True), s_self)
    p = jnp.exp(s - m)
    p_self = jnp.exp(s_self - m)
    l = jnp.sum(p, axis=1, keepdims=True) + p_self
    o = p_self * v_new
    for ti in range(n_tiles):
        o = o + _nt(p[:, ti * PAGE_SIZE:(ti + 1) * PAGE_SIZE].astype(BF16), vts[ti])
    o_ref[0, 0] = o / jnp.maximum(l, 1e-30)


def _moba_sample(cache_t, page_table, ids, q, kv_new, tab_ms, past):
    db, n_pages = page_table.shape
    ppb = MOBA_BLOCK // PAGE_SIZE
    n_tiles = MOBA_TOPK * ppb

    def tile_spec(ti):
        k, pp = divmod(ti, ppb)
        return pl.BlockSpec(
            (1, 1, LANES, PAGE_SIZE),
            lambda i, h, pt, sel: (pt[i * n_pages + sel[(i * N_HEADS + h) * MOBA_TOPK + k] * ppb + pp], h, 0, 0))

    q8 = jnp.pad(q.reshape(db, N_HEADS, 1, HEAD_DIM), ((0, 0), (0, 0), (0, 7), (0, 0)))
    grid_spec = pltpu.PrefetchScalarGridSpec(
        num_scalar_prefetch=2,
        grid=(db, N_HEADS),
        in_specs=[pl.BlockSpec((1, 1, 8, HEAD_DIM), lambda i, h, pt, sel: (i, h, 0, 0)),
                  pl.BlockSpec((1, 1, 1, LANES), lambda i, h, pt, sel: (i, h, 0, 0))]
                 + [tile_spec(ti) for ti in range(n_tiles)]
                 + [pl.BlockSpec((1, 3, LANES), lambda i, h, pt, sel: (h, 0, 0))],
        out_specs=pl.BlockSpec((1, 1, 8, HEAD_DIM), lambda i, h, pt, sel: (i, h, 0, 0)),
    )
    out = pl.pallas_call(
        functools.partial(_moba_sample_kernel, n_tiles=n_tiles, last_blk=past // MOBA_BLOCK - 1,
                          pages_per_block=ppb),
        grid_spec=grid_spec,
        out_shape=jax.ShapeDtypeStruct((db, N_HEADS, 8, HEAD_DIM), F32),
        compiler_params=_cparams("parallel", "arbitrary"),
        name="moba_sample",
    )(page_table.reshape(-1), ids.reshape(-1), q8, kv_new.reshape(db, N_HEADS, 1, LANES),
      *([cache_t] * n_tiles), tab_ms)
    return out[:, :, 0, :].reshape(db, N_HEADS * HEAD_DIM)


def _nsa_select_kernel(q_ref, kc_ref, tc_ref, a_ref, oc_ref, ids_ref, *, n_valid, n_sel, cur):
    nrow = kc_ref.shape[2]
    lane = lax.broadcasted_iota(I32, (8, nrow), 1)
    psums = []
    for g in range(NSA_KV_HEADS):
        q8 = (q_ref[0, g] * ATTN_SCALE).astype(BF16)
        kc = kc_ref[0, g].astype(BF16)
        s = jnp.where(lane < n_valid, _nt(q8, kc) + tc_ref[g], -jnp.inf)
        p_c = _softmax_rows(s)
        oc_ref[0, g] = _mm(p_c.astype(BF16), kc)[:, HEAD_DIM:]
        psums.append(jnp.sum(p_c[0:NSA_GROUP], axis=0, keepdims=True))
    p_sum = _pad_rows(jnp.concatenate(psums, axis=0), 8)
    a = a_ref[...]
    imp = sum(_mm(part, a) for part in _split3(p_sum))
    blk = lax.broadcasted_iota(I32, imp.shape, 1)
    valid = blk <= min(cur, n_sel - 1)
    forced = (blk == 0) | (blk >= cur - (N_LOCAL_SEL - 1))
    score = jnp.where(valid, jnp.where(forced, jnp.inf, imp), -jnp.inf)
    ids_ref[0] = _topk_lanes(score, SEL_TOPK)


def _nsa_select(q4, kcvc, tab_cs, past):
    db = q4.shape[0]
    nrow = kcvc.shape[2]
    n_cmp_valid = (past - (CMP_BLOCK - 1)) // CMP_STRIDE + 1
    n_sel = max(-(-(past + 1) // SEL_BLOCK), SEL_TOPK)
    n_cmp = n_sel * SEL_BLOCK // CMP_STRIDE - CMP_BLOCK // CMP_STRIDE + 1
    cur = past // SEL_BLOCK
    assert n_sel <= 2 * LANES and cur >= SEL_TOPK - 1 and n_cmp <= nrow
    a = jnp.asarray(_imp_matrix(2 * LANES, nrow, n_cmp).T, BF16)
    tc = jnp.concatenate([tab_cs.reshape(NSA_KV_HEADS, NSA_GROUP, nrow)] * 2, axis=1)
    return pl.pallas_call(
        functools.partial(_nsa_select_kernel, n_valid=n_cmp_valid, n_sel=n_sel, cur=cur),
        grid=(db,),
        in_specs=[pl.BlockSpec((1, NSA_KV_HEADS, 8, LANES), lambda i: (i, 0, 0, 0)),
                  pl.BlockSpec((1, NSA_KV_HEADS, nrow, LANES), lambda i: (i, 0, 0, 0)),
                  pl.BlockSpec(tc.shape, lambda i: (0, 0, 0)),
                  pl.BlockSpec(a.shape, lambda i: (0, 0))],
        out_specs=[pl.BlockSpec((1, NSA_KV_HEADS, 8, HEAD_DIM), lambda i: (i, 0, 0, 0)),
                   pl.BlockSpec((1, 8, LANES), lambda i: (i, 0, 0))],
        out_shape=[jax.ShapeDtypeStruct((db, NSA_KV_HEADS, 8, HEAD_DIM), F32),
                   jax.ShapeDtypeStruct((db, 8, LANES), I32)],
        compiler_params=_cparams("parallel"),
        name="nsa_select",
    )(q4, kcvc, tc, a)


def _nsa_sample_kernel(pt_ref, ids_ref, q_ref, oc_ref, gate_ref, snew_ref, wnew_ref, st_ref, *rest,
                       n_cache_blk):
    tiles = rest[:SEL_TOPK]
    tms_ref, tws_ref, eye_ref, o_ref, st_out = rest[SEL_TOPK:]
    i, g = pl.program_id(0), pl.program_id(1)
    q8 = q_ref[0, 0] * ATTN_SCALE
    q8b = q8.astype(BF16)
    halves = PAGE_SIZE // SEL_BLOCK
    last_page = n_cache_blk // halves - 1
    half_of_lane = jnp.right_shift(lax.broadcasted_iota(I32, (8, PAGE_SIZE), 1), SEL_SHIFT)
    t0 = tms_ref[0, 2][:, 0:1]

    scores, vts = [], []
    for k in range(SEL_TOPK):
        blk = ids_ref[(i * NSA_KV_HEADS + g) * SEL_TOPK + k]
        in_cache = blk < n_cache_blk
        half = jnp.where(in_cache, lax.rem(blk, halves), -1)
        near = jnp.logical_and(in_cache, blk // halves == last_page)
        tile = tiles[k][0, 0]
        vts.append(tile[HEAD_DIM:].astype(BF16))
        s = _mm(q8b, tile[:HEAD_DIM].astype(BF16)) + jnp.where(near, tms_ref[0, 0], tms_ref[0, 1])
        scores.append(jnp.where(half_of_lane == half, s, -jnp.inf))
    s = jnp.concatenate(scores, axis=1)
    snew = snew_ref[0]
    s_self = jnp.sum(q8 * snew[:, :HEAD_DIM], axis=1, keepdims=True) + t0
    m = jnp.maximum(jnp.max(s, axis=1, keepdims=True), s_self)
    p = jnp.exp(s - m)
    p_self = jnp.exp(s_self - m)
    l = jnp.sum(p, axis=1, keepdims=True) + p_self
    o_s = p_self * snew[:, HEAD_DIM:]
    for k in range(SEL_TOPK):
        o_s = o_s + _nt(p[:, k * PAGE_SIZE:(k + 1) * PAGE_SIZE].astype(BF16), vts[k])
    o_s = o_s / jnp.maximum(l, 1e-30)

    ws = st_ref[0, 0]
    wl = lax.broadcasted_iota(I32, (8, WINDOW), 1)
    s_w = jnp.where(wl >= 1, _mm(q8b, ws[:HEAD_DIM].astype(BF16)) + tws_ref[0], -jnp.inf)
    wnew = wnew_ref[0]
    w_self = jnp.sum(q8 * wnew[:, :HEAD_DIM], axis=1, keepdims=True) + t0
    m = jnp.maximum(jnp.max(s_w, axis=1, keepdims=True), w_self)
    p = jnp.exp(s_w - m)
    p_self = jnp.exp(w_self - m)
    l = jnp.sum(p, axis=1, keepdims=True) + p_self
    o_w = (_nt(p.astype(BF16), ws[HEAD_DIM:].astype(BF16)) + p_self * wnew[:, HEAD_DIM:]) / jnp.maximum(l, 1e-30)

    gate = gate_ref[0, 0]
    o_ref[0, 0] = gate[:, 0:1] * oc_ref[0, 0] + gate[:, 1:2] * o_s + gate[:, 2:3] * o_w
    new_col = sum(_nt(eye_ref[...], _pad_rows(part, 8)) for part in _split3(wnew))[:, 0:1]
    wcol = lax.broadcasted_iota(I32, ws.shape, 1)
    st_out[0, 0] = jnp.where(wcol == WINDOW - 1, new_col, pltpu.roll(ws, WINDOW - 1, 1))


def _nsa_sample(cache_t, state_t, page_table, ids, q4, o_c, gates, sel_new, win_new, tabs, past):
    db, n_pages = page_table.shape
    gw = NSA_KV_HEADS * LANES
    halves = PAGE_SIZE // SEL_BLOCK
    n_cache_blk = past // SEL_BLOCK
    assert state_t.shape[-1] == WINDOW and past >= WINDOW and past % PAGE_SIZE == 0

    def tile_spec(k):
        def imap(i, g, pt, sel):
            blk = jnp.minimum(sel[(i * NSA_KV_HEADS + g) * SEL_TOPK + k], n_cache_blk - 1)
            return (pt[i * n_pages + blk // halves], g, 0, 0)
        return pl.BlockSpec((1, 1, LANES, PAGE_SIZE), imap)

    grp = lambda n: pl.BlockSpec((1, 1, 8, n), lambda i, g, pt, sel: (i, g, 0, 0))
    new = pl.BlockSpec((1, 1, LANES), lambda i, g, pt, sel: (i, 0, g))
    st = pl.BlockSpec((1, 1, LANES, WINDOW), lambda i, g, pt, sel: (i, g, 0, 0))
    pad8 = lambda x: jnp.concatenate([x, jnp.zeros_like(x)], axis=-2)
    tms = pad8(tabs["ms"].reshape(NSA_KV_HEADS, NSA_GROUP, 3, LANES).transpose(0, 2, 1, 3))
    tws = pad8(tabs["ws"].reshape(NSA_KV_HEADS, NSA_GROUP, WINDOW))
    eye = jnp.asarray(np.eye(LANES), BF16)
    grid_spec = pltpu.PrefetchScalarGridSpec(
        num_scalar_prefetch=2,
        grid=(db, NSA_KV_HEADS),
        in_specs=[grp(HEAD_DIM), grp(HEAD_DIM), grp(LANES), new, new, st] + [tile_spec(k) for k in range(SEL_TOPK)]
                 + [pl.BlockSpec((1, 3, 8, LANES), lambda i, g, pt, sel: (g, 0, 0, 0)),
                    pl.BlockSpec((1, 8, WINDOW), lambda i, g, pt, sel: (g, 0, 0)),
                    pl.BlockSpec(eye.shape, lambda i, g, pt, sel: (0, 0))],
        out_specs=[grp(HEAD_DIM), st],
    )
    o, st_new = pl.pallas_call(
        functools.partial(_nsa_sample_kernel, n_cache_blk=n_cache_blk),
        grid_spec=grid_spec,
        out_shape=[jax.ShapeDtypeStruct((db, NSA_KV_HEADS, 8, HEAD_DIM), F32),
                   jax.ShapeDtypeStruct(state_t.shape, F32)],
        compiler_params=_cparams("parallel", "arbitrary"),
        name="nsa_sample",
    )(page_table.reshape(-1), ids[:, :NSA_KV_HEADS, :SEL_TOPK].reshape(-1), q4, o_c, gates,
      sel_new.reshape(db, 1, gw), win_new.reshape(db, 1, gw), state_t,
      *([cache_t] * SEL_TOPK), tms, tws, eye)
    return o[:, :, :NSA_GROUP, :].reshape(db, N_HEADS * HEAD_DIM), st_new


def kernel(x_prompt, x_sample, cache_moba_kv, cache_nsa_cmp_kv, cache_nsa_sel_kv, state_nsa_win_kv, page_table,
           rel_bias, attn_norm, ffn_norm, moba_w_qkv, moba_q_norm, moba_k_norm, moba_w_o, nsa_w_in,
           nsa_gate_bias, nsa_q_norm, nsa_k_norm, nsa_cmp_pos, nsa_cmp_w1, nsa_cmp_b1, nsa_cmp_w2, nsa_w_o,
           ffn_w_up, ffn_w_down):
    b, s, d = x_prompt.shape
    db = x_sample.shape[0]
    assert x_sample.shape[1] == 1 and d == N_HEADS * HEAD_DIM
    past = page_table.shape[1] * PAGE_SIZE
    depth = attn_norm.shape[0]
    tabs = _bias_tables(rel_bias, s, past)
    hp = x_prompt.reshape(b * s, d)
    hs = x_sample.reshape(db, d)
    outs = {k: [] for k in ("moba_p", "moba_s", "cmp_p", "cmp_s", "sel_p", "sel_s", "win_p", "win_s")}
    kvshape = lambda n, g: (n, -1, g, 2, HEAD_DIM)
    for i in range(depth):
        j = i // 2
        if i % 2 == 0:
            qp_t, kvp_t, kvpb, kmean, kvpb_t = _proj_moba(hp, attn_norm[i], moba_w_qkv[j], moba_q_norm[j],
                                                          moba_k_norm[j], qdt=BF16, seq=s)
            qs, kvs, _ = _proj_moba(hs, attn_norm[i], moba_w_qkv[j], moba_q_norm[j], moba_k_norm[j], qdt=F32)
            op = _moba_prompt(qp_t, kvpb, kvpb_t, kmean, tabs["moba"], b, s)
            cache_t = _slot_minor(cache_moba_kv[j], N_HEADS)
            ids = _moba_select(cache_t, page_table, qs, past)
            osm = _moba_sample(cache_t, page_table, ids, qs, kvs, tabs["ms"], past)
            outs["moba_p"].append(_token_major(kvp_t, b, s, N_HEADS))
            outs["moba_s"].append(kvs.reshape(kvshape(db, N_HEADS)))
            w_o = moba_w_o[j]
        else:
            qp_t, cp, sp_t, wp_t, spb, wpb, gp_t, cp_t, spb_t, wpb_t = _proj_nsa(
                hp, attn_norm[i], nsa_w_in[j], nsa_gate_bias[j], nsa_q_norm[j], nsa_k_norm[j], qdt=BF16, seq=s)
            qs, cs, ss, ws, _, _, gs = _proj_nsa(hs, attn_norm[i], nsa_w_in[j], nsa_gate_bias[j],
                                                 nsa_q_norm[j], nsa_k_norm[j], qdt=F32)
            cw = _compress_weights(nsa_cmp_pos[j], nsa_cmp_w1[j], nsa_cmp_b1[j], nsa_cmp_w2[j], nsa_k_norm[j, 0])
            kcvc_p, kcvc_pt = _compress_prompt(cp, cw, b, s)
            op = _nsa_prompt(qp_t, gp_t, kcvc_p, kcvc_pt, spb, spb_t, wpb, wpb_t, tabs, b, s)
            kcvc_s = _compress_sample(cache_nsa_cmp_kv[j], page_table, cs, cw, past)
            pad8 = lambda x: jnp.concatenate([x, jnp.zeros_like(x)], axis=2)
            q4 = pad8(qs.reshape(db, NSA_KV_HEADS, NSA_GROUP, HEAD_DIM))
            q4w = jnp.pad(q4, ((0, 0), (0, 0), (0, 0), (0, LANES - HEAD_DIM)))
            g4 = pad8(gs.reshape(db, NSA_KV_HEADS, LANES)[:, :, :NSA_GROUP * N_BRANCH]
                      .reshape(db, NSA_KV_HEADS, NSA_GROUP, N_BRANCH))
            g4 = jnp.pad(g4, ((0, 0), (0, 0), (0, 0), (0, LANES - N_BRANCH)))
            o_c, sel_ids = _nsa_select(q4w, kcvc_s, tabs["cs"], past)
            osm, st_t = _nsa_sample(_slot_minor(cache_nsa_sel_kv[j], NSA_KV_HEADS),
                                    _slot_minor(state_nsa_win_kv[j], NSA_KV_HEADS), page_table, sel_ids, q4, o_c,
                                    g4, ss, ws, tabs, past)
            ws_new = st_t.reshape(db, NSA_KV_HEADS, 2, HEAD_DIM, WINDOW).transpose(0, 4, 1, 2, 3)
            gsh = (NSA_KV_HEADS, 2, HEAD_DIM)
            outs["cmp_p"].append(_token_major(cp_t, b, s, NSA_KV_HEADS))
            outs["cmp_s"].append(cs.reshape((db, 1) + gsh))
            outs["sel_p"].append(_token_major(sp_t, b, s, NSA_KV_HEADS))
            outs["sel_s"].append(ss.reshape((db, 1) + gsh))
            outs["win_p"].append(_token_major(wp_t, b, s, NSA_KV_HEADS)[:, s - min(WINDOW, s):])
            outs["win_s"].append(ws_new.reshape((db, WINDOW) + gsh))
            w_o = nsa_w_o[j]
        hp = _attn_out_mlp(hp, op, w_o, ffn_norm[i], ffn_w_up[i], ffn_w_down[i])
        hs = _attn_out_mlp(hs, osm, w_o, ffn_norm[i], ffn_w_up[i], ffn_w_down[i])
    stack = lambda k: jnp.stack(outs[k])
    return (hp.reshape(b, s, d), hs.reshape(db, 1, d), stack("moba_p"), stack("moba_s"), stack("cmp_p"),
            stack("cmp_s"), stack("sel_p"), stack("sel_s"), stack("win_p"), stack("win_s"))
```

```python
import functools
import math

import numpy as np
import jax
import jax.numpy as jnp
from jax import lax
from jax.experimental import pallas as pl
from jax.experimental.pallas import tpu as pltpu

F32 = jnp.float32
BF16 = jnp.bfloat16
I32 = jnp.int32

N_HEADS = 16
HEAD_DIM = 64
NORM_EPS = 1e-6
ATTN_SCALE = HEAD_DIM ** -0.5
REL_BUCKETS = 32
REL_MAX_DIST = 128
PAGE_SIZE = 128
MOBA_BLOCK = 256
MOBA_TOPK = 3
NSA_KV_HEADS = 4
NSA_GROUP = N_HEADS // NSA_KV_HEADS
CMP_BLOCK = 32
CMP_STRIDE = 16
CMP_HIDDEN = 2 * HEAD_DIM
SEL_BLOCK = 64
SEL_TOPK = 16
N_LOCAL_SEL = 2
WINDOW = 512
N_BRANCH = 3

LANES = 128
LANE_SHIFT = 7
SEL_SHIFT = 6
MASK_NEG = -(2.0 ** 100)
VMEM_LIMIT = 56 * 1024 * 1024
MOBA_UNROLL = 4
NSA_UNROLL = 4
NSA_TQ = 128
NSA_TK = 256
WIN_KEYS = WINDOW + NSA_TQ
CMP_NEAR = 32


def _cparams(*sem):
    return pltpu.CompilerParams(dimension_semantics=sem, vmem_limit_bytes=VMEM_LIMIT)


def _nt(a, b):
    return lax.dot_general(a, b, (((1,), (1,)), ((), ())), preferred_element_type=F32)


def _mm(a, b):
    return jnp.dot(a, b, preferred_element_type=F32)


def _split2(x):
    hi = x.astype(BF16)
    lo = (x - hi.astype(F32)).astype(BF16)
    return hi, lo


def _split3(x):
    hi = x.astype(BF16)
    r = x - hi.astype(F32)
    mid = r.astype(BF16)
    lo = (r - mid.astype(F32)).astype(BF16)
    return hi, mid, lo


def _rms_rows(x, g):
    return x * lax.rsqrt(jnp.mean(x * x, axis=-1, keepdims=True) + NORM_EPS) * g


def _group_norm(h, gmat_ref, gain, k_lanes_only):
    hi, lo = _split2(h * h)
    gm = gmat_ref[...]
    ss = _mm(hi, gm) + _mm(lo, gm)
    y = h * lax.rsqrt(ss * (1.0 / HEAD_DIM) + NORM_EPS) * gain
    if k_lanes_only:
        lane = lax.broadcasted_iota(I32, h.shape, 1)
        y = jnp.where((lane & (LANES - 1)) < HEAD_DIM, y, h)
    return y


def _softmax_rows(s):
    m = jnp.max(s, axis=-1, keepdims=True)
    m = jnp.where(m == -jnp.inf, 0.0, m)
    e = jnp.exp(s - m)
    return e * (1.0 / jnp.maximum(jnp.sum(e, axis=-1, keepdims=True), 1e-30))


def _bucket_of_dist():
    n = np.arange(REL_MAX_DIST + 1)
    max_exact = REL_BUCKETS // 2
    nf = np.maximum(n, 1).astype(np.float32)
    large = max_exact + (np.log(nf / np.float32(max_exact)) / np.float32(math.log(REL_MAX_DIST / max_exact))
                         * np.float32(REL_BUCKETS - max_exact)).astype(np.int32)
    large = np.minimum(large, REL_BUCKETS - 1)
    return np.where(n < max_exact, n, large).astype(np.int32)


def _dist_tables(seq, past):
    c = lambda d: np.clip(d, 0, REL_MAX_DIST)
    ncp = seq // CMP_STRIDE
    i256 = np.arange(MOBA_BLOCK)[:, None]
    j256 = np.arange(MOBA_BLOCK)[None, :]
    q128 = np.arange(NSA_TQ)[None, :]
    t = {}
    t["moba"] = np.stack([c(j256 - i256), c(MOBA_BLOCK + j256 - i256),
                          np.full((MOBA_BLOCK, MOBA_BLOCK), REL_MAX_DIST)])
    t["sel"] = np.stack([c(NSA_TQ * v + q128 - np.arange(NSA_TK)[:, None]) for v in range(4)])
    t["win"] = c(q128 + WINDOW - np.arange(WIN_KEYS)[:, None])
    mrow = np.arange(ncp)[:, None]
    near = q128 - CMP_STRIDE * (mrow - CMP_NEAR // 2) - (CMP_BLOCK - 1)
    t["cmp"] = np.where(mrow < CMP_NEAR, c(near), REL_MAX_DIST)
    r128 = np.arange(LANES)
    t["ms"] = np.stack([c(LANES - r128), np.full(LANES, REL_MAX_DIST), np.zeros(LANES, np.int64)])
    nrow = _cmp_rows(past)
    t["cs"] = c(past - CMP_STRIDE * np.arange(nrow) - (CMP_BLOCK - 1))
    t["ws"] = c(WINDOW - np.arange(WINDOW))
    return t


def _cmp_rows(past):
    return -(-(past // CMP_STRIDE + 4) // 16) * 16


def _tab_kernel(idx_ref, rb_ref, o_ref):
    idx = idx_ref[...]
    b = lax.broadcasted_iota(I32, (REL_BUCKETS, idx.shape[1]), 0)
    oh = jnp.where(b == idx, 1.0, 0.0).astype(BF16)
    o_ref[...] = _mm(rb_ref[0], oh) + _mm(rb_ref[1], oh) + _mm(rb_ref[2], oh)


def _bias_tables(rel_bias, seq, past):
    pats = _dist_tables(seq, past)
    bucket = _bucket_of_dist()
    chunk = 16384
    flat, spans, off = [], {}, 0
    for name, d in pats.items():
        n = d.size
        pad = -n % LANES
        flat.append(bucket[d.reshape(-1)])
        flat.append(np.zeros(pad, np.int32))
        spans[name] = (off, n, d.shape)
        off += n + pad
    total = -(-off // chunk) * chunk
    flat.append(np.zeros(total - off, np.int32))
    idx = jnp.asarray(np.concatenate(flat).astype(np.int32)).reshape(1, total)
    rb3 = jnp.stack(_split3(rel_bias.T.astype(F32)))
    tab = pl.pallas_call(
        _tab_kernel,
        grid=(total // chunk,),
        in_specs=[pl.BlockSpec((1, chunk), lambda i: (0, i)),
                  pl.BlockSpec((3, N_HEADS, REL_BUCKETS), lambda i: (0, 0, 0))],
        out_specs=pl.BlockSpec((N_HEADS, chunk), lambda i: (0, i)),
        out_shape=jax.ShapeDtypeStruct((N_HEADS, total), F32),
        compiler_params=_cparams("parallel"),
        name="bias_tables",
    )(idx, rb3)
    return {name: tab[:, o:o + n].reshape((N_HEADS,) + shp) for name, (o, n, shp) in spans.items()}


def _group_mats():
    i = np.arange(2 * LANES)
    g64 = (i[:, None] // HEAD_DIM == i[None, :] // HEAD_DIM)
    g128 = (i[:, None] // LANES == i[None, :] // LANES) & ((i[:, None] % LANES) < HEAD_DIM)
    return jnp.asarray(g64, BF16), jnp.asarray(g128, BF16)


def _interleave_gain(g, n):
    return jnp.tile(jnp.concatenate([g.astype(F32), jnp.ones((HEAD_DIM,), F32)]), n).reshape(1, n * LANES)


def _store_rows_or_columns(ref, sl, x):
    if len(ref.shape) == 3:
        ref[0, sl, :] = x.T.astype(ref.dtype)
    else:
        ref[:, sl] = x.astype(ref.dtype)


def _token_major(x_t, b, s, n_kv):
    return x_t.reshape(b, n_kv, 2, HEAD_DIM, s).transpose(0, 4, 1, 2, 3)


def _proj_moba_kernel(x_ref, an_ref, wq_ref, wkv_ref, qg_ref, kg_ref, g64_ref, g128_ref,
                      q_ref, kv_ref, kvb_ref, *rest, n_mean):
    xn = _rms_rows(x_ref[...], an_ref[...]).astype(BF16)
    cw = 2 * LANES
    for c in range(wq_ref.shape[1] // cw):
        sl = slice(c * cw, (c + 1) * cw)
        h = _mm(xn, wq_ref[:, sl])
        q = _group_norm(h, g64_ref, qg_ref[:, sl], False)
        _store_rows_or_columns(q_ref, sl, q)
    for c in range(wkv_ref.shape[1] // cw):
        sl = slice(c * cw, (c + 1) * cw)
        h = _mm(xn, wkv_ref[:, sl])
        kv = _group_norm(h, g128_ref, kg_ref[:, sl], True)
        _store_rows_or_columns(kv_ref, sl, kv)
        kvb_ref[:, sl] = kv.astype(BF16)
        if n_mean:
            km_ref, kvtb_ref = rest
            _store_rows_or_columns(kvtb_ref, sl, kv)
            for r in range(n_mean):
                km_ref[0, r:r + 1, sl] = jnp.mean(kv[r * MOBA_BLOCK:(r + 1) * MOBA_BLOCK], axis=0, keepdims=True)


def _proj_moba(x, an, w_qkv, q_gain, k_gain, *, qdt, seq=None):
    m, d = x.shape
    tm = min(m, 512)
    with_mean = seq is not None
    g64, g128 = _group_mats()
    w3 = w_qkv.reshape(d, 3, N_HEADS, HEAD_DIM)
    wkv = jnp.stack([w3[:, 1], w3[:, 2]], axis=2).reshape(d, 2 * d).astype(BF16)
    wq = w3[:, 0].reshape(d, d).astype(BF16)
    qg = jnp.tile(q_gain.astype(F32), N_HEADS).reshape(1, d)
    kg = _interleave_gain(k_gain, N_HEADS)
    qn = wq.shape[1]
    n_mean = tm // MOBA_BLOCK if with_mean else 0
    const = lambda i: (0, 0)
    if seq is None:
        q_shape, q_spec = (m, qn), pl.BlockSpec((tm, qn), lambda i: (i, 0))
        kv_shape, kv_spec = (m, 2 * d), pl.BlockSpec((tm, 2 * d), lambda i: (i, 0))
    else:
        nt = seq // tm
        q_shape, q_spec = (m // seq, qn, seq), pl.BlockSpec((1, qn, tm), lambda i: (i // nt, 0, i % nt))
        kv_shape, kv_spec = (m // seq, 2 * d, seq), pl.BlockSpec((1, 2 * d, tm), lambda i: (i // nt, 0, i % nt))
    out_shape = [jax.ShapeDtypeStruct(q_shape, qdt), jax.ShapeDtypeStruct(kv_shape, F32),
                 jax.ShapeDtypeStruct((m, 2 * d), BF16)]
    out_specs = [q_spec, kv_spec, pl.BlockSpec((tm, 2 * d), lambda i: (i, 0))]
    if n_mean:
        out_shape += [jax.ShapeDtypeStruct((m // tm, n_mean, 2 * d), F32), jax.ShapeDtypeStruct(kv_shape, BF16)]
        out_specs += [pl.BlockSpec((1, n_mean, 2 * d), lambda i: (i, 0, 0)), kv_spec]
    return pl.pallas_call(
        functools.partial(_proj_moba_kernel, n_mean=n_mean),
        grid=(m // tm,),
        in_specs=[pl.BlockSpec((tm, d), lambda i: (i, 0)), pl.BlockSpec((1, d), const),
                  pl.BlockSpec((d, qn), const), pl.BlockSpec((d, 2 * d), const),
                  pl.BlockSpec((1, qn), const), pl.BlockSpec((1, 2 * d), const),
                  pl.BlockSpec(g64.shape, const), pl.BlockSpec(g128.shape, const)],
        out_specs=out_specs, out_shape=out_shape,
        compiler_params=_cparams("parallel"),
        name="proj_moba",
    )(x, an.reshape(1, d).astype(F32), wq, wkv, qg, kg, g64, g128)


def _proj_nsa_kernel(x_ref, an_ref, wq_ref, wkv_ref, wg_ref, gb_ref, qg_ref, kg_ref, g64_ref, g128_ref,
                     q_ref, cmp_ref, sel_ref, win_ref, selb_ref, winb_ref, gate_ref, *rest):
    cmp_t_ref, selb_t_ref, winb_t_ref = rest if rest else (None, None, None)
    xn = _rms_rows(x_ref[...], an_ref[...]).astype(BF16)
    cw = 2 * LANES
    for c in range(wq_ref.shape[1] // cw):
        sl = slice(c * cw, (c + 1) * cw)
        h = _mm(xn, wq_ref[:, sl])
        _store_rows_or_columns(q_ref, sl, _group_norm(h, g64_ref, qg_ref[:, sl], False))
    per_branch = NSA_KV_HEADS * LANES // cw
    for c in range(wkv_ref.shape[1] // cw):
        sl = slice(c * cw, (c + 1) * cw)
        br, cc = divmod(c, per_branch)
        osl = slice(cc * cw, (cc + 1) * cw)
        h = _mm(xn, wkv_ref[:, sl])
        if br == 0:
            cmp_ref[:, osl] = h
            if cmp_t_ref is not None:
                _store_rows_or_columns(cmp_t_ref, osl, h)
        else:
            kv = _group_norm(h, g128_ref, kg_ref[:, sl], True)
            o32, o16, o16t = (sel_ref, selb_ref, selb_t_ref) if br == 1 else (win_ref, winb_ref, winb_t_ref)
            _store_rows_or_columns(o32, osl, kv)
            o16[:, osl] = kv.astype(BF16)
            if o16t is not None:
                _store_rows_or_columns(o16t, osl, kv)
    hg = _mm(xn, wg_ref[...]) + gb_ref[...]
    _store_rows_or_columns(gate_ref, slice(0, hg.shape[1]), 1.0 / (1.0 + jnp.exp(-hg)))


def _proj_nsa(x, an, w_in, gate_bias, q_gain, k_gain, *, qdt, seq=None):
    m, d = x.shape
    tm = min(m, 512)
    g64, g128 = _group_mats()
    kvd = N_BRANCH * NSA_KV_HEADS * LANES
    gw = NSA_KV_HEADS * LANES
    ng = NSA_GROUP * N_BRANCH
    qg = jnp.tile(q_gain.astype(F32), N_HEADS).reshape(1, d)
    wq = w_in[:, :d].astype(BF16)
    wkv = w_in[:, d:d + kvd].astype(BF16)
    wg = jnp.zeros((d, NSA_KV_HEADS, LANES), F32).at[:, :, :ng].set(
        w_in[:, d + kvd:].reshape(d, NSA_KV_HEADS, ng)).reshape(d, gw).astype(BF16)
    gb = jnp.zeros((NSA_KV_HEADS, LANES), F32).at[:, :ng].set(
        gate_bias.astype(F32).reshape(NSA_KV_HEADS, ng)).reshape(1, gw)
    kg = jnp.concatenate([jnp.ones((1, gw), F32), _interleave_gain(k_gain[1], NSA_KV_HEADS),
                          _interleave_gain(k_gain[2], NSA_KV_HEADS)], axis=1)
    qn = wq.shape[1]
    const = lambda i: (0, 0)
    row = lambda n: pl.BlockSpec((tm, n), lambda i: (i, 0))
    rows32 = jax.ShapeDtypeStruct((m, gw), F32)
    if seq is None:
        q_shape, q_spec = jax.ShapeDtypeStruct((m, qn), qdt), row(qn)
        kv_shape, kv_spec, gate_shape, gate_spec, extra_shape, extra_spec = rows32, row(gw), rows32, row(gw), [], []
    else:
        nt = seq // tm
        cols = lambda n: pl.BlockSpec((1, n, tm), lambda i: (i // nt, 0, i % nt))
        q_shape, q_spec = jax.ShapeDtypeStruct((m // seq, qn, seq), qdt), cols(qn)
        kv_shape, kv_spec = jax.ShapeDtypeStruct((m // seq, gw, seq), F32), cols(gw)
        gate_shape, gate_spec = kv_shape, kv_spec
        extra_shape = [kv_shape] + [jax.ShapeDtypeStruct((m // seq, gw, seq), BF16)] * 2
        extra_spec = [kv_spec] * 3
    return pl.pallas_call(
        _proj_nsa_kernel,
        grid=(m // tm,),
        in_specs=[row(d), pl.BlockSpec((1, d), const), pl.BlockSpec((d, qn), const),
                  pl.BlockSpec((d, kvd), const), pl.BlockSpec((d, gw), const), pl.BlockSpec((1, gw), const),
                  pl.BlockSpec((1, qn), const), pl.BlockSpec((1, kvd), const),
                  pl.BlockSpec(g64.shape, const), pl.BlockSpec(g128.shape, const)],
        out_specs=[q_spec, row(gw), kv_spec, kv_spec, row(gw), row(gw), gate_spec] + extra_spec,
        out_shape=[q_shape, rows32, kv_shape, kv_shape]
                  + [jax.ShapeDtypeStruct((m, gw), BF16)] * 2 + [gate_shape] + extra_shape,
        compiler_params=_cparams("parallel"),
        name="proj_nsa",
    )(x, an.reshape(1, d).astype(F32), wq, wkv, wg, gb, qg, kg, g64, g128)


def _mlp_kernel(h_ref, o_ref, wo_ref, g_ref, wup_ref, wdn_ref, out_ref, h1_s, xn_s, acc_s):
    f = pl.program_id(1)

    @pl.when(f == 0)
    def _():
        h1 = h_ref[...] + _mm(o_ref[...], wo_ref[...])
        h1_s[...] = h1
        xn_s[...] = _rms_rows(h1, g_ref[...]).astype(BF16)
        acc_s[...] = jnp.zeros_like(acc_s)

    u = _mm(xn_s[...], wup_ref[...])
    u = jnp.square(jnp.maximum(u, 0.0)).astype(BF16)
    acc_s[...] += _mm(u, wdn_ref[...])

    @pl.when(f == pl.num_programs(1) - 1)
    def _():
        out_ref[...] = h1_s[...] + acc_s[...]


def _attn_out_mlp(h, o, w_o, fn, w_up, w_down):
    m, d = h.shape
    dff = w_up.shape[1]
    tm = min(m, 1024)
    tf = 512
    return pl.pallas_call(
        _mlp_kernel,
        grid=(m // tm, dff // tf),
        in_specs=[pl.BlockSpec((tm, d), lambda i, f: (i, 0)), pl.BlockSpec((tm, d), lambda i, f: (i, 0)),
                  pl.BlockSpec((d, d), lambda i, f: (0, 0)), pl.BlockSpec((1, d), lambda i, f: (0, 0)),
                  pl.BlockSpec((d, tf), lambda i, f: (0, f)), pl.BlockSpec((tf, d), lambda i, f: (f, 0))],
        out_specs=pl.BlockSpec((tm, d), lambda i, f: (i, 0)),
        out_shape=jax.ShapeDtypeStruct((m, d), F32),
        scratch_shapes=[pltpu.VMEM((tm, d), F32), pltpu.VMEM((tm, d), BF16), pltpu.VMEM((tm, d), F32)],
        compiler_params=_cparams("parallel", "arbitrary"),
        name="attn_out_mlp",
    )(h, o.astype(BF16), w_o.astype(BF16), fn.reshape(1, d).astype(F32), w_up.astype(BF16), w_down.astype(BF16))


def _topk_rows(x, k, on_pick):
    n = x.shape[0]
    row = lax.broadcasted_iota(I32, x.shape, 0)
    for _ in range(k):
        m = jnp.max(x, axis=0, keepdims=True)
        idx = jnp.min(jnp.where(x == m, row, n), axis=0, keepdims=True)
        hit = row == idx
        on_pick(hit, m)
        x = jnp.where(hit, -jnp.inf, x)


SUM_ROWS = 16


def _with_sum_rows(v_t):
    return jnp.concatenate([v_t, jnp.ones((SUM_ROWS, v_t.shape[1]), v_t.dtype)], axis=0)


def _online_softmax_step(m_ref, acc_ref, tiles):
    m_old = m_ref[0]
    mx = jnp.max(tiles[0][0], axis=0, keepdims=True)
    for s, _ in tiles[1:]:
        mx = jnp.maximum(mx, jnp.max(s, axis=0, keepdims=True))
    mn = jnp.maximum(m_old, mx)
    acc = jnp.exp(m_old - mn) * acc_ref[0]
    for s, v_aug in tiles:
        acc = acc + _mm(v_aug, jnp.exp(s - mn).astype(BF16))
    m_ref[0] = mn
    acc_ref[0] = acc


def _softmax_finish(acc):
    return acc[:HEAD_DIM] * (1.0 / jnp.maximum(acc[HEAD_DIM:HEAD_DIM + 1], 1e-30))


def _unrolled_range(n, unroll, visit):
    rem = lax.rem(n, unroll)
    done = 0
    size = 1
    while size < unroll:
        take = jnp.bitwise_and(rem, size)

        @pl.when(take != 0)
        def _(done=done, size=size):
            visit([done + k for k in range(size)])

        done = done + take
        size *= 2

    def body(i, carry):
        first = rem + unroll * i
        visit([first + k for k in range(unroll)])
        return carry

    lax.fori_loop(0, n // unroll, body, 0)


def _softmax_state_init(m_ref, acc_ref):
    m_ref[...] = jnp.full(m_ref.shape, -jnp.inf, F32)
    acc_ref[...] = jnp.zeros(acc_ref.shape, F32)


def _moba_prompt_kernel(qt_ref, kv_ref, kvt_ref, km_ref, tb_ref, o_ref, *state, nb):
    cur = pl.program_id(2)
    tq = MOBA_BLOCK
    key_lane = lax.broadcasted_iota(I32, (tq, LANES), 1)
    rown = lax.broadcasted_iota(I32, (nb, tq), 0)
    heads = [state[0:2], state[2:4]]
    for st in heads:
        _softmax_state_init(*st)
    qzs, qaugs = [], []
    for hh in range(2):
        qt = qt_ref[0, hh * HEAD_DIM:(hh + 1) * HEAD_DIM, :] * jnp.asarray(ATTN_SCALE, BF16)
        qz = jnp.concatenate([qt, jnp.zeros_like(qt)], axis=0)
        km_hi, km_lo = _split2(km_ref[0, :, hh * LANES:(hh + 1) * LANES])
        gate = _mm(km_hi, qz) + _mm(km_lo, qz)
        state = [jnp.full((nb, tq), MASK_NEG, F32)]

        def pick(hit, m, state=state):
            state[0] = jnp.where(hit, jnp.where(m > -jnp.inf, 0.0, state[0]), state[0])

        _topk_rows(jnp.where(rown < cur, gate, -jnp.inf), MOBA_TOPK, pick)
        negt = jnp.concatenate([state[0], jnp.zeros((LANES - nb, tq), F32)], axis=0).astype(BF16)
        qzs.append(qz)
        qaugs.append(jnp.concatenate([qz, negt], axis=0))

    def k_rows(hh, n):
        return kv_ref[0, pl.ds(pl.multiple_of(n * tq, tq), tq), hh * LANES:(hh + 1) * LANES]

    def v_cols(hh, n):
        return _with_sum_rows(
            kvt_ref[0, hh * LANES + HEAD_DIM:(hh + 1) * LANES, pl.ds(pl.multiple_of(n * tq, tq), tq)])

    for hh in range(2):
        _online_softmax_step(*heads[hh], [(_mm(k_rows(hh, cur), qzs[hh]) + tb_ref[hh, 0], v_cols(hh, cur))])

    def past_blocks(blocks):
        for hh in range(2):
            tiles = []
            for n in blocks:
                onehot = jnp.where(key_lane == n, 1.0, 0.0).astype(BF16)
                s = _mm(jnp.concatenate([k_rows(hh, n), onehot], axis=1), qaugs[hh])
                tiles.append((s + tb_ref[hh, jnp.where(cur - n == 1, 1, 2)], v_cols(hh, n)))
            _online_softmax_step(*heads[hh], tiles)

    _unrolled_range(cur, MOBA_UNROLL, past_blocks)
    o_t = jnp.concatenate([_softmax_finish(acc[0]) for _, acc in heads], axis=0)
    o_ref[...] = o_t.T.astype(o_ref.dtype)


def _moba_prompt(qt, kvb, kvt, kmean, tb, b, s):
    d = qt.shape[1]
    nb = s // MOBA_BLOCK
    assert s % MOBA_BLOCK == 0 and MOBA_TOPK <= nb <= LANES
    nq = s // MOBA_BLOCK
    r = np.arange(MOBA_BLOCK)
    causal = np.zeros((3, MOBA_BLOCK, MOBA_BLOCK), np.float32)
    causal[0] = np.where(r[:, None] <= r[None, :], 0.0, -np.inf)
    tb = tb + jnp.asarray(causal)
    return pl.pallas_call(
        functools.partial(_moba_prompt_kernel, nb=nb),
        grid=(b, N_HEADS // 2, nq),
        in_specs=[pl.BlockSpec((1, LANES, MOBA_BLOCK), lambda i, h, t: (i, h, t)),
                  pl.BlockSpec((1, s, 2 * LANES), lambda i, h, t: (i, 0, h)),
                  pl.BlockSpec((1, 2 * LANES, s), lambda i, h, t: (i, h, 0)),
                  pl.BlockSpec((1, nb, 2 * LANES), lambda i, h, t: (i, 0, h)),
                  pl.BlockSpec((2, 3, MOBA_BLOCK, MOBA_BLOCK), lambda i, h, t: (h, 0, 0, 0))],
        out_specs=pl.BlockSpec((MOBA_BLOCK, LANES), lambda i, h, t: (i * nq + t, h)),
        out_shape=jax.ShapeDtypeStruct((b * s, d), BF16),
        scratch_shapes=[pltpu.VMEM((1, 1, MOBA_BLOCK), F32),
                        pltpu.VMEM((1, HEAD_DIM + SUM_ROWS, MOBA_BLOCK), F32)] * 2,
        compiler_params=_cparams("parallel", "parallel", "arbitrary"),
        name="moba_prompt",
    )(qt, kvb.reshape(b, s, 2 * d), kvt, kmean.reshape(b, nb, 2 * d), tb)


def _gelu_tanh(x):
    return 0.5 * x * (1.0 + jnp.tanh(math.sqrt(2.0 / math.pi) * (x + 0.044715 * (x * x * x))))


def _compress_weights(cmp_pos, w1, b1, w2, k_gain0):
    half = CMP_BLOCK // 2
    z = jnp.zeros((half, HEAD_DIM, CMP_HIDDEN), F32)

    def first_layer(lo):
        wk, wv = w1[0, lo:lo + half], w1[1, lo:lo + half]
        top = jnp.concatenate([wk, z], axis=2)
        bot = jnp.concatenate([z, wv], axis=2)
        return jnp.concatenate([top, bot], axis=1).reshape(half * LANES, 2 * CMP_HIDDEN).astype(BF16)

    zz = jnp.zeros((CMP_HIDDEN, HEAD_DIM), F32)
    w2bd = jnp.concatenate([jnp.concatenate([w2[0], zz], axis=1),
                            jnp.concatenate([zz, w2[1]], axis=1)], axis=0).astype(BF16)
    pos_a = cmp_pos[:half].reshape(half, LANES).astype(F32)
    pos_b = cmp_pos[half:].reshape(half, LANES).astype(F32)
    return (pos_a, pos_b, first_layer(0), first_layer(half), b1.reshape(1, 2 * CMP_HIDDEN).astype(F32), w2bd,
            _interleave_gain(k_gain0, 1))


def _compress_tail(xa_s, xb_s, wa_ref, wb_ref, b1_ref, w2_ref, kg_ref):
    ha = _mm(xa_s[...], wa_ref[...])
    hb = _mm(xb_s[...], wb_ref[...])
    rows = ha.shape[0]
    h = ha + pltpu.roll(hb, rows - 1, 0) + b1_ref[...]
    out = _mm(_gelu_tanh(h).astype(BF16), w2_ref[...])
    lane = lax.broadcasted_iota(I32, out.shape, 1)
    is_k = lane < HEAD_DIM
    ss = jnp.sum(jnp.where(is_k, out * out, 0.0), axis=1, keepdims=True)
    kn = out * lax.rsqrt(ss * (1.0 / HEAD_DIM) + NORM_EPS) * kg_ref[...]
    return jnp.where(is_k, kn, out)


def _compress_prompt_kernel(*refs, nseg):
    x_refs = refs[:NSA_KV_HEADS]
    pa_ref, pb_ref, wa_ref, wb_ref, b1_ref, w2_ref, kg_ref, o_ref, ot_ref, xa_s, xb_s = refs[NSA_KV_HEADS:]
    half = CMP_BLOCK // 2
    for g in range(NSA_KV_HEADS):
        for p in range(half):
            v = x_refs[g][0, pl.ds(p, nseg, stride=CMP_STRIDE), :]
            xa_s[g * nseg:(g + 1) * nseg, p * LANES:(p + 1) * LANES] = (v + pa_ref[p:p + 1, :]).astype(BF16)
            xb_s[g * nseg:(g + 1) * nseg, p * LANES:(p + 1) * LANES] = (v + pb_ref[p:p + 1, :]).astype(BF16)
    res = _compress_tail(xa_s, xb_s, wa_ref, wb_ref, b1_ref, w2_ref, kg_ref)
    for g in range(NSA_KV_HEADS):
        rows = res[g * nseg:(g + 1) * nseg]
        o_ref[0, g] = rows.astype(BF16)
        ot_ref[0, g] = rows.T.astype(BF16)


def _compress_prompt(kv_cmp, cw, b, s):
    nseg = s // CMP_STRIDE
    gw = NSA_KV_HEADS * LANES
    kdim = (CMP_BLOCK // 2) * LANES
    const = lambda i: (0, 0)
    return pl.pallas_call(
        functools.partial(_compress_prompt_kernel, nseg=nseg),
        grid=(b,),
        in_specs=[pl.BlockSpec((1, s, LANES), functools.partial(lambda i, g: (i, 0, g), g=g))
                  for g in range(NSA_KV_HEADS)] + [pl.BlockSpec(w.shape, const) for w in cw],
        out_specs=[pl.BlockSpec((1, NSA_KV_HEADS, nseg, LANES), lambda i: (i, 0, 0, 0)),
                   pl.BlockSpec((1, NSA_KV_HEADS, LANES, nseg), lambda i: (i, 0, 0, 0))],
        out_shape=[jax.ShapeDtypeStruct((b, NSA_KV_HEADS, nseg, LANES), BF16),
                   jax.ShapeDtypeStruct((b, NSA_KV_HEADS, LANES, nseg), BF16)],
        scratch_shapes=[pltpu.VMEM((NSA_KV_HEADS * nseg, kdim), BF16)] * 2,
        compiler_params=_cparams("parallel"),
        name="compress_prompt",
    )(*([kv_cmp.reshape(b, s, gw)] * NSA_KV_HEADS), *cw)


CMP_PAGES = 4


def _compress_sample_kernel(pt_ref, *refs, nseg, nrow):
    x_refs = refs[:CMP_PAGES]
    (new_ref, pa_ref, pb_ref, wa_ref, wb_ref, b1_ref, w2_ref, kg_ref, o_ref,
     xa_s, xb_s, xt_s) = refs[CMP_PAGES:]
    j = pl.program_id(1)
    for pg in range(CMP_PAGES):
        for g in range(NSA_KV_HEADS):
            xt_s[pg, g] = x_refs[pg][0, g].T
    half = CMP_BLOCK // 2
    per_page = PAGE_SIZE // CMP_STRIDE
    per_step = CMP_PAGES * per_page
    tail = nrow - nseg

    @pl.when(j == 0)
    def _():
        first = lax.broadcasted_iota(I32, (tail, LANES), 0) == 0
        for g in range(NSA_KV_HEADS):
            rows = slice(g * nrow + nseg, (g + 1) * nrow)
            for p in range(half):
                v = jnp.zeros((tail, LANES), F32)
                if p == 0:
                    v = jnp.where(first, new_ref[0, :, g * LANES:(g + 1) * LANES], 0.0)
                xa_s[rows, p * LANES:(p + 1) * LANES] = (v + pa_ref[p:p + 1, :]).astype(BF16)
                xb_s[rows, p * LANES:(p + 1) * LANES] = (v + pb_ref[p:p + 1, :]).astype(BF16)

    for g in range(NSA_KV_HEADS):
        r0 = pl.multiple_of(g * nrow + j * per_step, 16)
        for p in range(half):
            v = jnp.concatenate([xt_s[pg, g, pl.ds(p, per_page, stride=CMP_STRIDE), :]
                                 for pg in range(CMP_PAGES)], axis=0)
            xa_s[pl.ds(r0, per_step), p * LANES:(p + 1) * LANES] = (v + pa_ref[p:p + 1, :]).astype(BF16)
            xb_s[pl.ds(r0, per_step), p * LANES:(p + 1) * LANES] = (v + pb_ref[p:p + 1, :]).astype(BF16)

    @pl.when(j == pl.num_programs(1) - 1)
    def _():
        res = _compress_tail(xa_s, xb_s, wa_ref, wb_ref, b1_ref, w2_ref, kg_ref)
        for g in range(NSA_KV_HEADS):
            o_ref[0, g] = res[g * nrow:(g + 1) * nrow]


def _compress_sample(cache, page_table, new_rows, cw, past):
    db, n_pages = page_table.shape
    assert n_pages % CMP_PAGES == 0 and (CMP_PAGES * PAGE_SIZE // CMP_STRIDE) % 16 == 0
    gw = NSA_KV_HEADS * LANES
    nseg = past // CMP_STRIDE
    nrow = _cmp_rows(past)
    kdim = (CMP_BLOCK // 2) * LANES
    const = lambda i, j, pt: (0, 0)
    grid_spec = pltpu.PrefetchScalarGridSpec(
        num_scalar_prefetch=1,
        grid=(db, n_pages // CMP_PAGES),
        in_specs=[pl.BlockSpec((1, NSA_KV_HEADS, LANES, PAGE_SIZE),
                               functools.partial(lambda i, j, pt, pg: (pt[i * n_pages + CMP_PAGES * j + pg], 0, 0, 0),
                                                 pg=pg))
                  for pg in range(CMP_PAGES)]
                 + [pl.BlockSpec((1, 1, gw), lambda i, j, pt: (i, 0, 0))]
                 + [pl.BlockSpec(w.shape, const) for w in cw],
        out_specs=pl.BlockSpec((1, NSA_KV_HEADS, nrow, LANES), lambda i, j, pt: (i, 0, 0, 0)),
        scratch_shapes=[pltpu.VMEM((NSA_KV_HEADS * nrow, kdim), BF16)] * 2
                       + [pltpu.VMEM((CMP_PAGES, NSA_KV_HEADS, PAGE_SIZE, LANES), F32)],
    )
    cview = _slot_minor(cache, NSA_KV_HEADS)
    return pl.pallas_call(
        functools.partial(_compress_sample_kernel, nseg=nseg, nrow=nrow),
        grid_spec=grid_spec,
        out_shape=jax.ShapeDtypeStruct((db, NSA_KV_HEADS, nrow, LANES), F32),
        compiler_params=_cparams("parallel", "arbitrary"),
        name="compress_sample",
    )(page_table.reshape(-1), *([cview] * CMP_PAGES), new_rows.reshape(db, 1, gw), *cw)


def _imp_matrix(n_sel, n_rows, n_cmp):
    ratio = SEL_BLOCK // CMP_STRIDE
    lead = CMP_BLOCK // CMP_STRIDE - 1
    j = np.arange(n_sel)[:, None]
    n = np.arange(n_rows)[None, :]
    return ((n >= ratio * j - lead) & (n <= ratio * j + ratio - 1) & (n < n_cmp))


def _nsa_prompt_kernel(qt_ref, gt_ref, kc_ref, kct_ref, ks_ref, kst_ref, kw_ref, kwt_ref, tsel_ref, tw_ref,
                       tc_ref, at_ref, o_ref, m_s, acc_s, *, n_sel, ncp):
    t = pl.program_id(2)
    tq, tk, grp = NSA_TQ, NSA_TK, NSA_GROUP
    cols = grp * tq
    p0 = t * tq
    scale = jnp.asarray(ATTN_SCALE, BF16)
    qt = jnp.concatenate([qt_ref[0, p * HEAD_DIM:(p + 1) * HEAD_DIM, :] * scale for p in range(grp)], axis=1)
    qz = jnp.concatenate([qt, jnp.zeros_like(qt)], axis=0)
    qoff = jnp.bitwise_and(lax.broadcasted_iota(I32, (1, cols), 1), tq - 1)

    shift = lax.rem(t * (tq // CMP_STRIDE) - CMP_NEAR // 2 + ncp, ncp)
    c_end = lax.broadcasted_iota(I32, (ncp, cols), 0) * CMP_STRIDE + (CMP_BLOCK - 1)
    s_c = jnp.where(p0 + qoff >= c_end, _mm(kc_ref[0, 0], qz) + pltpu.roll(tc_ref[0], shift, 0), -jnp.inf)
    m_c = jnp.max(s_c, axis=0, keepdims=True)
    e_c = jnp.exp(s_c - jnp.where(m_c == -jnp.inf, 0.0, m_c))
    p_c = e_c * (1.0 / jnp.maximum(jnp.sum(e_c, axis=0, keepdims=True), 1e-30))
    o_c = _mm(kct_ref[0, 0, HEAD_DIM:, :], p_c.astype(BF16))
    p_sum = p_c[:, 0:tq]
    for p in range(1, grp):
        p_sum = p_sum + p_c[:, p * tq:(p + 1) * tq]
    at = at_ref[...]
    imp = sum(_mm(at, part) for part in _split3(p_sum))

    blk = lax.broadcasted_iota(I32, (n_sel, tq), 0)
    cur = jnp.right_shift(p0 + lax.broadcasted_iota(I32, (n_sel, tq), 1), SEL_SHIFT)
    valid = blk <= cur
    forced = (blk == 0) | (blk >= cur - (N_LOCAL_SEL - 1))
    score = jnp.where(valid, jnp.where(forced, jnp.inf, imp), -jnp.inf)
    state = [jnp.full((n_sel, tq), MASK_NEG, F32)]

    def pick(hit, m):
        state[0] = jnp.where(hit, jnp.where(m > -jnp.inf, 0.0, state[0]), state[0])

    _topk_rows(score, SEL_TOPK, pick)
    negt = state[0]
    if n_sel < LANES:
        negt = jnp.concatenate([negt, jnp.zeros((LANES - n_sel, tq), F32)], axis=0)
    negt = negt.astype(BF16)
    qaug = jnp.concatenate([qz, jnp.concatenate([negt] * grp, axis=1)], axis=0)

    kj = lax.broadcasted_iota(I32, (tk, LANES), 0)
    kl = lax.broadcasted_iota(I32, (tk, LANES), 1)

    def sel_tile(kt, variant):
        at_kt = pl.ds(pl.multiple_of(kt * tk, tk), tk)
        onehot = jnp.where(kl == kt * (tk // SEL_BLOCK) + jnp.right_shift(kj, SEL_SHIFT), 1.0, 0.0).astype(BF16)
        s = _mm(jnp.concatenate([ks_ref[0, at_kt, :], onehot], axis=1), qaug) + tsel_ref[0, variant]
        return s, _with_sum_rows(kst_ref[0, HEAD_DIM:, at_kt])

    _softmax_state_init(m_s, acc_s)
    kt_d = t // (tk // tq)
    _online_softmax_step(m_s, acc_s, [sel_tile(kt_d, lax.rem(t, tk // tq))])

    def past_tiles(kts):
        _online_softmax_step(m_s, acc_s, [sel_tile(kt, jnp.minimum((p0 - kt * tk) // tq, 3)) for kt in kts])

    _unrolled_range(kt_d, NSA_UNROLL, past_tiles)
    o_s = _softmax_finish(acc_s[0])

    at_w = pl.ds(pl.multiple_of(p0, tq), WIN_KEYS)
    wj = lax.broadcasted_iota(I32, (WIN_KEYS, cols), 0)
    s_w = jnp.where(p0 + wj >= WINDOW, _mm(kw_ref[0, at_w, :], qz) + tw_ref[0], -jnp.inf)
    m_w = jnp.max(s_w, axis=0, keepdims=True)
    e_w = jnp.exp(s_w - jnp.where(m_w == -jnp.inf, 0.0, m_w))
    o_w = _softmax_finish(_mm(_with_sum_rows(kwt_ref[0, HEAD_DIM:, at_w]), e_w.astype(BF16)))

    gt = gt_ref[0]
    per_head = []
    for p in range(grp):
        c = slice(p * tq, (p + 1) * tq)
        g = [gt[N_BRANCH * p + k:N_BRANCH * p + k + 1, :] for k in range(N_BRANCH)]
        per_head.append(g[0] * o_c[:, c] + g[1] * o_s[:, c] + g[2] * o_w[:, c])
    o_ref[...] = jnp.concatenate(per_head, axis=0).T.astype(o_ref.dtype)


def _heads_on_lanes(tab, lead):
    n = len(lead)
    x = tab.reshape((NSA_KV_HEADS, NSA_GROUP) + tab.shape[1:])
    perm = (0,) + tuple(range(2, 3 + n)) + (1, 3 + n)
    x = x.transpose(perm)
    return x.reshape(x.shape[:2 + n] + (NSA_GROUP * tab.shape[-1],))


def _nsa_prompt(qt, gt, kcvc, kcvc_t, selb, selb_t, winb, winb_t, tabs, b, s):
    d = qt.shape[1]
    gw = NSA_KV_HEADS * LANES
    ncp = s // CMP_STRIDE
    n_sel = s // SEL_BLOCK
    assert s % NSA_TK == 0 and SEL_TOPK <= n_sel <= LANES and ncp >= CMP_NEAR
    nq = s // NSA_TQ
    at = jnp.asarray(_imp_matrix(n_sel, ncp, ncp - 1), BF16)
    winp = jnp.pad(winb.reshape(b, s, gw), ((0, 0), (WINDOW, 0), (0, 0)))
    winp_t = jnp.pad(winb_t, ((0, 0), (0, 0), (WINDOW, 0)))
    gq = NSA_GROUP
    cols = gq * NSA_TQ
    kr = np.arange(NSA_TK)[:, None]
    wr = np.arange(WIN_KEYS)[:, None]
    qc = np.arange(cols)[None, :] % NSA_TQ
    sel_mask = np.zeros((4, NSA_TK, cols), np.float32)
    for v in range(NSA_TK // NSA_TQ):
        sel_mask[v] = np.where(kr <= qc + v * NSA_TQ, 0.0, -np.inf)
    win_mask = np.where((wr > qc) & (wr <= qc + WINDOW), 0.0, -np.inf).astype(np.float32)
    tsel = _heads_on_lanes(tabs["sel"], (4,)) + jnp.asarray(sel_mask)
    tw = _heads_on_lanes(tabs["win"], ()) + jnp.asarray(win_mask)
    tc = _heads_on_lanes(tabs["cmp"], ())
    return pl.pallas_call(
        functools.partial(_nsa_prompt_kernel, n_sel=n_sel, ncp=ncp),
        grid=(b, NSA_KV_HEADS, nq),
        in_specs=[pl.BlockSpec((1, gq * HEAD_DIM, NSA_TQ), lambda i, g, t: (i, g, t)),
                  pl.BlockSpec((1, LANES, NSA_TQ), lambda i, g, t: (i, g, t)),
                  pl.BlockSpec((1, 1, ncp, LANES), lambda i, g, t: (i, g, 0, 0)),
                  pl.BlockSpec((1, 1, LANES, ncp), lambda i, g, t: (i, g, 0, 0)),
                  pl.BlockSpec((1, s, LANES), lambda i, g, t: (i, 0, g)),
                  pl.BlockSpec((1, LANES, s), lambda i, g, t: (i, g, 0)),
                  pl.BlockSpec((1, s + WINDOW, LANES), lambda i, g, t: (i, 0, g)),
                  pl.BlockSpec((1, LANES, s + WINDOW), lambda i, g, t: (i, g, 0)),
                  pl.BlockSpec((1, 4, NSA_TK, cols), lambda i, g, t: (g, 0, 0, 0)),
                  pl.BlockSpec((1, WIN_KEYS, cols), lambda i, g, t: (g, 0, 0)),
                  pl.BlockSpec((1, ncp, cols), lambda i, g, t: (g, 0, 0)),
                  pl.BlockSpec(at.shape, lambda i, g, t: (0, 0))],
        out_specs=pl.BlockSpec((NSA_TQ, gq * HEAD_DIM), lambda i, g, t: (i * nq + t, g)),
        out_shape=jax.ShapeDtypeStruct((b * s, d), BF16),
        scratch_shapes=[pltpu.VMEM((1, 1, cols), F32), pltpu.VMEM((1, HEAD_DIM + SUM_ROWS, cols), F32)],
        compiler_params=_cparams("parallel", "parallel", "arbitrary"),
        name="nsa_prompt",
    )(qt, gt, kcvc, kcvc_t, selb.reshape(b, s, gw), selb_t, winp, winp_t, tsel, tw, tc, at)


def _topk_lanes(x, k):
    ax = x.ndim - 1
    lane = lax.broadcasted_iota(I32, x.shape, ax)
    oshape = x.shape[:-1] + (LANES,)
    out_lane = lax.broadcasted_iota(I32, oshape, ax)
    ids = jnp.zeros(oshape, I32)
    for r in range(k):
        m = jnp.max(x, axis=ax, keepdims=True)
        idx = jnp.min(jnp.where(x == m, lane, x.shape[ax]), axis=ax, keepdims=True)
        ids = jnp.where(out_lane == r, idx, ids)
        x = jnp.where(lane == idx, -jnp.inf, x)
    return ids


def _slot_minor(cache, n_kv):
    pages, slots = cache.shape[:2]
    return jnp.transpose(cache, (0, 2, 3, 4, 1)).reshape(pages, n_kv, LANES, slots)


MOBA_SELECT_PAGES = 8


def _moba_select_kernel(pt_ref, *refs, nb):
    x_refs = refs[:MOBA_SELECT_PAGES]
    q_ref, ids_ref, ksum_s = refs[MOBA_SELECT_PAGES:]
    j = pl.program_id(1)
    ppb = MOBA_BLOCK // PAGE_SIZE
    per_step = MOBA_SELECT_PAGES // ppb

    @pl.when(j == 0)
    def _():
        ksum_s[...] = jnp.zeros_like(ksum_s)

    lane = lax.broadcasted_iota(I32, ksum_s.shape, 2)
    acc = ksum_s[...]
    for bi in range(per_step):
        x = x_refs[bi * ppb][0]
        for pg in range(1, ppb):
            x = x + x_refs[bi * ppb + pg][0]
        col = jnp.sum(x, axis=2, keepdims=True)
        acc = acc + jnp.where(lane == j * per_step + bi, col, 0.0)
    ksum_s[...] = acc

    @pl.when(j == pl.num_programs(1) - 1)
    def _():
        km = ksum_s[...] * (1.0 / MOBA_BLOCK)
        gate = sum(jnp.einsum("hqd,hdn->hqn", a, b_, preferred_element_type=F32)
                   for a in _split3(q_ref[0]) for b_ in _split3(km))
        blk = lax.broadcasted_iota(I32, gate.shape, 2)
        ids_ref[0] = _topk_lanes(jnp.where(blk < nb, gate, -jnp.inf), MOBA_TOPK)


def _moba_select(cache_t, page_table, q, past):
    db, n_pages = page_table.shape
    nb = past // MOBA_BLOCK
    assert past % MOBA_BLOCK == 0 and MOBA_TOPK <= nb <= LANES and n_pages % MOBA_SELECT_PAGES == 0
    q8 = jnp.pad(q.reshape(db, N_HEADS, 1, HEAD_DIM), ((0, 0), (0, 0), (0, 7), (0, 0)))
    grid_spec = pltpu.PrefetchScalarGridSpec(
        num_scalar_prefetch=1,
        grid=(db, n_pages // MOBA_SELECT_PAGES),
        in_specs=[pl.BlockSpec((1, N_HEADS, HEAD_DIM, PAGE_SIZE),
                               functools.partial(lambda i, j, pt, pg: (pt[i * n_pages + MOBA_SELECT_PAGES * j + pg],
                                                                       0, 0, 0), pg=pg))
                  for pg in range(MOBA_SELECT_PAGES)]
                 + [pl.BlockSpec((1, N_HEADS, 8, HEAD_DIM), lambda i, j, pt: (i, 0, 0, 0))],
        out_specs=pl.BlockSpec((1, N_HEADS, 8, LANES), lambda i, j, pt: (i, 0, 0, 0)),
        scratch_shapes=[pltpu.VMEM((N_HEADS, HEAD_DIM, LANES), F32)],
    )
    ids = pl.pallas_call(
        functools.partial(_moba_select_kernel, nb=nb),
        grid_spec=grid_spec,
        out_shape=jax.ShapeDtypeStruct((db, N_HEADS, 8, LANES), I32),
        compiler_params=_cparams("parallel", "arbitrary"),
        name="moba_select",
    )(page_table.reshape(-1), *([cache_t] * MOBA_SELECT_PAGES), q8)
    return ids[:, :, 0, :MOBA_TOPK]


def _pad_rows(x, rows):
    return jnp.concatenate([x, jnp.zeros((rows - x.shape[0],) + x.shape[1:], x.dtype)], axis=0)


MOBA_SAMPLE_HEADS = 4


def _moba_sample_kernel(pt_ref, ids_ref, q_ref, new_ref, *rest, n_tiles, last_blk, pages_per_block):
    tiles = rest[:MOBA_SAMPLE_HEADS * n_tiles]
    tab_ref, o_ref = rest[MOBA_SAMPLE_HEADS * n_tiles:]
    i, hg = pl.program_id(0), pl.program_id(1)
    for hh in range(MOBA_SAMPLE_HEADS):
        h = hg * MOBA_SAMPLE_HEADS + hh
        q8 = q_ref[0, hh] * ATTN_SCALE
        q8b = q8.astype(BF16)
        new = new_ref[0, hh]
        k_new, v_new = new[:, :HEAD_DIM], new[:, HEAD_DIM:]
        scores, vts = [], []
        for ti in range(n_tiles):
            k, pp = divmod(ti, pages_per_block)
            blk = ids_ref[(i * N_HEADS + h) * MOBA_TOPK + k]
            near = jnp.logical_and(blk == last_blk, pp == pages_per_block - 1)
            tile = tiles[hh * n_tiles + ti][0, 0]
            vts.append(tile[HEAD_DIM:].astype(BF16))
            scores.append(_mm(q8b, tile[:HEAD_DIM].astype(BF16))
                          + jnp.where(near, tab_ref[hh, 0:1, :], tab_ref[hh, 1:2, :]))
        s = jnp.concatenate(scores, axis=1)
        s_self = jnp.sum(q8 * k_new, axis=1, keepdims=True) + tab_ref[hh, 2:3, 0:1]
        m = jnp.maximum(jnp.max(s, axis=1, keepdims=True), s_self)
        p = jnp.exp(s - m)
        p_self = jnp.exp(s_self - m)
        l = jnp.sum(p, axis=1, keepdims=True) + p_self
        o = p_self * v_new
        for ti in range(n_tiles):
            o = o + _nt(p[:, ti * PAGE_SIZE:(ti + 1) * PAGE_SIZE].astype(BF16), vts[ti])
        o_ref[0, hh] = o / jnp.maximum(l, 1e-30)


def _moba_sample(cache_t, page_table, ids, q, kv_new, tab_ms, past):
    db, n_pages = page_table.shape
    ppb = MOBA_BLOCK // PAGE_SIZE
    n_tiles = MOBA_TOPK * ppb

    hps = MOBA_SAMPLE_HEADS

    def tile_spec(hh, ti):
        k, pp = divmod(ti, ppb)

        def imap(i, hg, pt, sel):
            h = hg * hps + hh
            return (pt[i * n_pages + sel[(i * N_HEADS + h) * MOBA_TOPK + k] * ppb + pp], h, 0, 0)
        return pl.BlockSpec((1, 1, LANES, PAGE_SIZE), imap)

    q8 = jnp.pad(q.reshape(db, N_HEADS, 1, HEAD_DIM), ((0, 0), (0, 0), (0, 7), (0, 0)))
    grid_spec = pltpu.PrefetchScalarGridSpec(
        num_scalar_prefetch=2,
        grid=(db, N_HEADS // hps),
        in_specs=[pl.BlockSpec((1, hps, 8, HEAD_DIM), lambda i, h, pt, sel: (i, h, 0, 0)),
                  pl.BlockSpec((1, hps, 1, LANES), lambda i, h, pt, sel: (i, h, 0, 0))]
                 + [tile_spec(hh, ti) for hh in range(hps) for ti in range(n_tiles)]
                 + [pl.BlockSpec((hps, 3, LANES), lambda i, h, pt, sel: (h, 0, 0))],
        out_specs=pl.BlockSpec((1, hps, 8, HEAD_DIM), lambda i, h, pt, sel: (i, h, 0, 0)),
    )
    out = pl.pallas_call(
        functools.partial(_moba_sample_kernel, n_tiles=n_tiles, last_blk=past // MOBA_BLOCK - 1,
                          pages_per_block=ppb),
        grid_spec=grid_spec,
        out_shape=jax.ShapeDtypeStruct((db, N_HEADS, 8, HEAD_DIM), F32),
        compiler_params=_cparams("parallel", "arbitrary"),
        name="moba_sample",
    )(page_table.reshape(-1), ids.reshape(-1), q8, kv_new.reshape(db, N_HEADS, 1, LANES),
      *([cache_t] * (hps * n_tiles)), tab_ms)
    return out[:, :, 0, :].reshape(db, N_HEADS * HEAD_DIM)


def _nsa_select_kernel(q_ref, kc_ref, tc_ref, a_ref, oc_ref, ids_ref, *, n_valid, n_sel, cur):
    nrow = kc_ref.shape[2]
    lane = lax.broadcasted_iota(I32, (8, nrow), 1)
    psums = []
    for g in range(NSA_KV_HEADS):
        q8 = (q_ref[0, g] * ATTN_SCALE).astype(BF16)
        kc = kc_ref[0, g].astype(BF16)
        s = jnp.where(lane < n_valid, _nt(q8, kc) + tc_ref[g], -jnp.inf)
        p_c = _softmax_rows(s)
        oc_ref[0, g] = _mm(p_c.astype(BF16), kc)[:, HEAD_DIM:]
        psums.append(jnp.sum(p_c[0:NSA_GROUP], axis=0, keepdims=True))
    p_sum = _pad_rows(jnp.concatenate(psums, axis=0), 8)
    a = a_ref[...]
    imp = sum(_mm(part, a) for part in _split3(p_sum))
    blk = lax.broadcasted_iota(I32, imp.shape, 1)
    valid = blk <= min(cur, n_sel - 1)
    forced = (blk == 0) | (blk >= cur - (N_LOCAL_SEL - 1))
    score = jnp.where(valid, jnp.where(forced, jnp.inf, imp), -jnp.inf)
    ids_ref[0] = _topk_lanes(score, SEL_TOPK)


def _nsa_select(q4, kcvc, tab_cs, past):
    db = q4.shape[0]
    nrow = kcvc.shape[2]
    n_cmp_valid = (past - (CMP_BLOCK - 1)) // CMP_STRIDE + 1
    n_sel = max(-(-(past + 1) // SEL_BLOCK), SEL_TOPK)
    n_cmp = n_sel * SEL_BLOCK // CMP_STRIDE - CMP_BLOCK // CMP_STRIDE + 1
    cur = past // SEL_BLOCK
    assert n_sel <= 2 * LANES and cur >= SEL_TOPK - 1 and n_cmp <= nrow
    a = jnp.asarray(_imp_matrix(2 * LANES, nrow, n_cmp).T, BF16)
    tc = jnp.concatenate([tab_cs.reshape(NSA_KV_HEADS, NSA_GROUP, nrow)] * 2, axis=1)
    return pl.pallas_call(
        functools.partial(_nsa_select_kernel, n_valid=n_cmp_valid, n_sel=n_sel, cur=cur),
        grid=(db,),
        in_specs=[pl.BlockSpec((1, NSA_KV_HEADS, 8, LANES), lambda i: (i, 0, 0, 0)),
                  pl.BlockSpec((1, NSA_KV_HEADS, nrow, LANES), lambda i: (i, 0, 0, 0)),
                  pl.BlockSpec(tc.shape, lambda i: (0, 0, 0)),
                  pl.BlockSpec(a.shape, lambda i: (0, 0))],
        out_specs=[pl.BlockSpec((1, NSA_KV_HEADS, 8, HEAD_DIM), lambda i: (i, 0, 0, 0)),
                   pl.BlockSpec((1, 8, LANES), lambda i: (i, 0, 0))],
        out_shape=[jax.ShapeDtypeStruct((db, NSA_KV_HEADS, 8, HEAD_DIM), F32),
                   jax.ShapeDtypeStruct((db, 8, LANES), I32)],
        compiler_params=_cparams("parallel"),
        name="nsa_select",
    )(q4, kcvc, tc, a)


def _nsa_sample_kernel(pt_ref, ids_ref, q_ref, oc_ref, gate_ref, snew_ref, wnew_ref, st_ref, *rest,
                       n_cache_blk):
    tiles = rest[:SEL_TOPK]
    tms_ref, tws_ref, eye_ref, o_ref, st_out = rest[SEL_TOPK:]
    i, g = pl.program_id(0), pl.program_id(1)
    q8 = q_ref[0, 0] * ATTN_SCALE
    q8b = q8.astype(BF16)
    halves = PAGE_SIZE // SEL_BLOCK
    last_page = n_cache_blk // halves - 1
    half_of_lane = jnp.right_shift(lax.broadcasted_iota(I32, (8, PAGE_SIZE), 1), SEL_SHIFT)
    t0 = tms_ref[0, 2][:, 0:1]

    scores, vts = [], []
    for k in range(SEL_TOPK):
        blk = ids_ref[(i * NSA_KV_HEADS + g) * SEL_TOPK + k]
        in_cache = blk < n_cache_blk
        half = jnp.where(in_cache, lax.rem(blk, halves), -1)
        near = jnp.logical_and(in_cache, blk // halves == last_page)
        tile = tiles[k][0, 0]
        vts.append(tile[HEAD_DIM:].astype(BF16))
        s = _mm(q8b, tile[:HEAD_DIM].astype(BF16)) + jnp.where(near, tms_ref[0, 0], tms_ref[0, 1])
        scores.append(jnp.where(half_of_lane == half, s, -jnp.inf))
    s = jnp.concatenate(scores, axis=1)
    snew = snew_ref[0]
    s_self = jnp.sum(q8 * snew[:, :HEAD_DIM], axis=1, keepdims=True) + t0
    m = jnp.maximum(jnp.max(s, axis=1, keepdims=True), s_self)
    p = jnp.exp(s - m)
    p_self = jnp.exp(s_self - m)
    l = jnp.sum(p, axis=1, keepdims=True) + p_self
    o_s = p_self * snew[:, HEAD_DIM:]
    for k in range(SEL_TOPK):
        o_s = o_s + _nt(p[:, k * PAGE_SIZE:(k + 1) * PAGE_SIZE].astype(BF16), vts[k])
    o_s = o_s / jnp.maximum(l, 1e-30)

    ws = st_ref[0, 0]
    wl = lax.broadcasted_iota(I32, (8, WINDOW), 1)
    s_w = jnp.where(wl >= 1, _mm(q8b, ws[:HEAD_DIM].astype(BF16)) + tws_ref[0], -jnp.inf)
    wnew = wnew_ref[0]
    w_self = jnp.sum(q8 * wnew[:, :HEAD_DIM], axis=1, keepdims=True) + t0
    m = jnp.maximum(jnp.max(s_w, axis=1, keepdims=True), w_self)
    p = jnp.exp(s_w - m)
    p_self = jnp.exp(w_self - m)
    l = jnp.sum(p, axis=1, keepdims=True) + p_self
    o_w = (_nt(p.astype(BF16), ws[HEAD_DIM:].astype(BF16)) + p_self * wnew[:, HEAD_DIM:]) / jnp.maximum(l, 1e-30)

    gate = gate_ref[0, 0]
    o_ref[0, 0] = gate[:, 0:1] * oc_ref[0, 0] + gate[:, 1:2] * o_s + gate[:, 2:3] * o_w
    new_col = sum(_nt(eye_ref[...], _pad_rows(part, 8)) for part in _split3(wnew))[:, 0:1]
    wcol = lax.broadcasted_iota(I32, ws.shape, 1)
    st_out[0, 0] = jnp.where(wcol == WINDOW - 1, new_col, pltpu.roll(ws, WINDOW - 1, 1))


def _nsa_sample(cache_t, state_t, page_table, ids, q4, o_c, gates, sel_new, win_new, tabs, past):
    db, n_pages = page_table.shape
    gw = NSA_KV_HEADS * LANES
    halves = PAGE_SIZE // SEL_BLOCK
    n_cache_blk = past // SEL_BLOCK
    assert state_t.shape[-1] == WINDOW and past >= WINDOW and past % PAGE_SIZE == 0

    def tile_spec(k):
        def imap(i, g, pt, sel):
            blk = jnp.minimum(sel[(i * NSA_KV_HEADS + g) * SEL_TOPK + k], n_cache_blk - 1)
            return (pt[i * n_pages + blk // halves], g, 0, 0)
        return pl.BlockSpec((1, 1, LANES, PAGE_SIZE), imap)

    grp = lambda n: pl.BlockSpec((1, 1, 8, n), lambda i, g, pt, sel: (i, g, 0, 0))
    new = pl.BlockSpec((1, 1, LANES), lambda i, g, pt, sel: (i, 0, g))
    st = pl.BlockSpec((1, 1, LANES, WINDOW), lambda i, g, pt, sel: (i, g, 0, 0))
    pad8 = lambda x: jnp.concatenate([x, jnp.zeros_like(x)], axis=-2)
    tms = pad8(tabs["ms"].reshape(NSA_KV_HEADS, NSA_GROUP, 3, LANES).transpose(0, 2, 1, 3))
    tws = pad8(tabs["ws"].reshape(NSA_KV_HEADS, NSA_GROUP, WINDOW))
    eye = jnp.asarray(np.eye(LANES), BF16)
    grid_spec = pltpu.PrefetchScalarGridSpec(
        num_scalar_prefetch=2,
        grid=(db, NSA_KV_HEADS),
        in_specs=[grp(HEAD_DIM), grp(HEAD_DIM), grp(LANES), new, new, st] + [tile_spec(k) for k in range(SEL_TOPK)]
                 + [pl.BlockSpec((1, 3, 8, LANES), lambda i, g, pt, sel: (g, 0, 0, 0)),
                    pl.BlockSpec((1, 8, WINDOW), lambda i, g, pt, sel: (g, 0, 0)),
                    pl.BlockSpec(eye.shape, lambda i, g, pt, sel: (0, 0))],
        out_specs=[grp(HEAD_DIM), st],
    )
    o, st_new = pl.pallas_call(
        functools.partial(_nsa_sample_kernel, n_cache_blk=n_cache_blk),
        grid_spec=grid_spec,
        out_shape=[jax.ShapeDtypeStruct((db, NSA_KV_HEADS, 8, HEAD_DIM), F32),
                   jax.ShapeDtypeStruct(state_t.shape, F32)],
        compiler_params=_cparams("parallel", "arbitrary"),
        name="nsa_sample",
    )(page_table.reshape(-1), ids[:, :NSA_KV_HEADS, :SEL_TOPK].reshape(-1), q4, o_c, gates,
      sel_new.reshape(db, 1, gw), win_new.reshape(db, 1, gw), state_t,
      *([cache_t] * SEL_TOPK), tms, tws, eye)
    return o[:, :, :NSA_GROUP, :].reshape(db, N_HEADS * HEAD_DIM), st_new


def kernel(x_prompt, x_sample, cache_moba_kv, cache_nsa_cmp_kv, cache_nsa_sel_kv, state_nsa_win_kv, page_table,
           rel_bias, attn_norm, ffn_norm, moba_w_qkv, moba_q_norm, moba_k_norm, moba_w_o, nsa_w_in,
           nsa_gate_bias, nsa_q_norm, nsa_k_norm, nsa_cmp_pos, nsa_cmp_w1, nsa_cmp_b1, nsa_cmp_w2, nsa_w_o,
           ffn_w_up, ffn_w_down):
    b, s, d = x_prompt.shape
    db = x_sample.shape[0]
    assert x_sample.shape[1] == 1 and d == N_HEADS * HEAD_DIM
    past = page_table.shape[1] * PAGE_SIZE
    depth = attn_norm.shape[0]
    tabs = _bias_tables(rel_bias, s, past)
    hp = x_prompt.reshape(b * s, d)
    hs = x_sample.reshape(db, d)
    outs = {k: [] for k in ("moba_p", "moba_s", "cmp_p", "cmp_s", "sel_p", "sel_s", "win_p", "win_s")}
    kvshape = lambda n, g: (n, -1, g, 2, HEAD_DIM)
    for i in range(depth):
        j = i // 2
        if i % 2 == 0:
            qp_t, kvp_t, kvpb, kmean, kvpb_t = _proj_moba(hp, attn_norm[i], moba_w_qkv[j], moba_q_norm[j],
                                                          moba_k_norm[j], qdt=BF16, seq=s)
            qs, kvs, _ = _proj_moba(hs, attn_norm[i], moba_w_qkv[j], moba_q_norm[j], moba_k_norm[j], qdt=F32)
            op = _moba_prompt(qp_t, kvpb, kvpb_t, kmean, tabs["moba"], b, s)
            cache_t = _slot_minor(cache_moba_kv[j], N_HEADS)
            ids = _moba_select(cache_t, page_table, qs, past)
            osm = _moba_sample(cache_t, page_table, ids, qs, kvs, tabs["ms"], past)
            outs["moba_p"].append(_token_major(kvp_t, b, s, N_HEADS))
            outs["moba_s"].append(kvs.reshape(kvshape(db, N_HEADS)))
            w_o = moba_w_o[j]
        else:
            qp_t, cp, sp_t, wp_t, spb, wpb, gp_t, cp_t, spb_t, wpb_t = _proj_nsa(
                hp, attn_norm[i], nsa_w_in[j], nsa_gate_bias[j], nsa_q_norm[j], nsa_k_norm[j], qdt=BF16, seq=s)
            qs, cs, ss, ws, _, _, gs = _proj_nsa(hs, attn_norm[i], nsa_w_in[j], nsa_gate_bias[j],
                                                 nsa_q_norm[j], nsa_k_norm[j], qdt=F32)
            cw = _compress_weights(nsa_cmp_pos[j], nsa_cmp_w1[j], nsa_cmp_b1[j], nsa_cmp_w2[j], nsa_k_norm[j, 0])
            kcvc_p, kcvc_pt = _compress_prompt(cp, cw, b, s)
            op = _nsa_prompt(qp_t, gp_t, kcvc_p, kcvc_pt, spb, spb_t, wpb, wpb_t, tabs, b, s)
            kcvc_s = _compress_sample(cache_nsa_cmp_kv[j], page_table, cs, cw, past)
            pad8 = lambda x: jnp.concatenate([x, jnp.zeros_like(x)], axis=2)
            q4 = pad8(qs.reshape(db, NSA_KV_HEADS, NSA_GROUP, HEAD_DIM))
            q4w = jnp.pad(q4, ((0, 0), (0, 0), (0, 0), (0, LANES - HEAD_DIM)))
            g4 = pad8(gs.reshape(db, NSA_KV_HEADS, LANES)[:, :, :NSA_GROUP * N_BRANCH]
                      .reshape(db, NSA_KV_HEADS, NSA_GROUP, N_BRANCH))
            g4 = jnp.pad(g4, ((0, 0), (0, 0), (0, 0), (0, LANES - N_BRANCH)))
            o_c, sel_ids = _nsa_select(q4w, kcvc_s, tabs["cs"], past)
            osm, st_t = _nsa_sample(_slot_minor(cache_nsa_sel_kv[j], NSA_KV_HEADS),
                                    _slot_minor(state_nsa_win_kv[j], NSA_KV_HEADS), page_table, sel_ids, q4, o_c,
                                    g4, ss, ws, tabs, past)
            ws_new = st_t.reshape(db, NSA_KV_HEADS, 2, HEAD_DIM, WINDOW).transpose(0, 4, 1, 2, 3)
            gsh = (NSA_KV_HEADS, 2, HEAD_DIM)
            outs["cmp_p"].append(_token_major(cp_t, b, s, NSA_KV_HEADS))
            outs["cmp_s"].append(cs.reshape((db, 1) + gsh))
            outs["sel_p"].append(_token_major(sp_t, b, s, NSA_KV_HEADS))
            outs["sel_s"].append(ss.reshape((db, 1) + gsh))
            outs["win_p"].append(_token_major(wp_t, b, s, NSA_KV_HEADS)[:, s - min(WINDOW, s):])
            outs["win_s"].append(ws_new.reshape((db, WINDOW) + gsh))
            w_o = nsa_w_o[j]
        hp = _attn_out_mlp(hp, op, w_o, ffn_norm[i], ffn_w_up[i], ffn_w_down[i])
        hs = _attn_out_mlp(hs, osm, w_o, ffn_norm[i], ffn_w_up[i], ffn_w_down[i])
    stack = lambda k: jnp.stack(outs[k])
    return (hp.reshape(b, s, d), hs.reshape(db, 1, d), stack("moba_p"), stack("moba_s"), stack("cmp_p"),
            stack("cmp_s"), stack("sel_p"), stack("sel_s"), stack("win_p"), stack("win_s"))
```

```python
import functools
import math

import numpy as np
import jax
import jax.numpy as jnp
from jax import lax
from jax.experimental import pallas as pl
from jax.experimental.pallas import tpu as pltpu

F32 = jnp.float32
BF16 = jnp.bfloat16
I32 = jnp.int32

N_HEADS = 16
HEAD_DIM = 64
NORM_EPS = 1e-6
ATTN_SCALE = HEAD_DIM ** -0.5
REL_BUCKETS = 32
REL_MAX_DIST = 128
PAGE_SIZE = 128
MOBA_BLOCK = 256
MOBA_TOPK = 3
NSA_KV_HEADS = 4
NSA_GROUP = N_HEADS // NSA_KV_HEADS
CMP_BLOCK = 32
CMP_STRIDE = 16
CMP_HIDDEN = 2 * HEAD_DIM
SEL_BLOCK = 64
SEL_TOPK = 16
N_LOCAL_SEL = 2
WINDOW = 512
N_BRANCH = 3

LANES = 128
LANE_SHIFT = 7
SEL_SHIFT = 6
MASK_NEG = -(2.0 ** 100)
VMEM_LIMIT = 56 * 1024 * 1024
MOBA_UNROLL = 4
NSA_UNROLL = 4
NSA_TQ = 256
NSA_TK = 256
WIN_KEYS = WINDOW + NSA_TQ
CMP_NEAR = 32


def _cparams(*sem):
    return pltpu.CompilerParams(dimension_semantics=sem, vmem_limit_bytes=VMEM_LIMIT)


def _nt(a, b):
    return lax.dot_general(a, b, (((1,), (1,)), ((), ())), preferred_element_type=F32)


def _mm(a, b):
    return jnp.dot(a, b, preferred_element_type=F32)


def _split2(x):
    hi = x.astype(BF16)
    lo = (x - hi.astype(F32)).astype(BF16)
    return hi, lo


def _split3(x):
    hi = x.astype(BF16)
    r = x - hi.astype(F32)
    mid = r.astype(BF16)
    lo = (r - mid.astype(F32)).astype(BF16)
    return hi, mid, lo


def _rms_rows(x, g):
    return x * lax.rsqrt(jnp.mean(x * x, axis=-1, keepdims=True) + NORM_EPS) * g


def _group_norm(h, gmat_ref, gain, k_lanes_only):
    hi, lo = _split2(h * h)
    gm = gmat_ref[...]
    ss = _mm(hi, gm) + _mm(lo, gm)
    y = h * lax.rsqrt(ss * (1.0 / HEAD_DIM) + NORM_EPS) * gain
    if k_lanes_only:
        lane = lax.broadcasted_iota(I32, h.shape, 1)
        y = jnp.where((lane & (LANES - 1)) < HEAD_DIM, y, h)
    return y


def _softmax_rows(s):
    m = jnp.max(s, axis=-1, keepdims=True)
    m = jnp.where(m == -jnp.inf, 0.0, m)
    e = jnp.exp(s - m)
    return e * (1.0 / jnp.maximum(jnp.sum(e, axis=-1, keepdims=True), 1e-30))


def _bucket_of_dist():
    n = np.arange(REL_MAX_DIST + 1)
    max_exact = REL_BUCKETS // 2
    nf = np.maximum(n, 1).astype(np.float32)
    large = max_exact + (np.log(nf / np.float32(max_exact)) / np.float32(math.log(REL_MAX_DIST / max_exact))
                         * np.float32(REL_BUCKETS - max_exact)).astype(np.int32)
    large = np.minimum(large, REL_BUCKETS - 1)
    return np.where(n < max_exact, n, large).astype(np.int32)


def _dist_tables(seq, past):
    c = lambda d: np.clip(d, 0, REL_MAX_DIST)
    ncp = seq // CMP_STRIDE
    i256 = np.arange(MOBA_BLOCK)[:, None]
    j256 = np.arange(MOBA_BLOCK)[None, :]
    q128 = np.arange(NSA_TQ)[None, :]
    t = {}
    t["moba"] = np.stack([c(j256 - i256), c(MOBA_BLOCK + j256 - i256),
                          np.full((MOBA_BLOCK, MOBA_BLOCK), REL_MAX_DIST)])
    t["sel"] = np.stack([c(NSA_TQ * v + q128 - np.arange(NSA_TK)[:, None]) for v in range(4)])
    t["win"] = c(q128 + WINDOW - np.arange(WIN_KEYS)[:, None])
    mrow = np.arange(ncp)[:, None]
    near = q128 - CMP_STRIDE * (mrow - CMP_NEAR // 2) - (CMP_BLOCK - 1)
    t["cmp"] = np.where(mrow < CMP_NEAR, c(near), REL_MAX_DIST)
    r128 = np.arange(LANES)
    t["ms"] = np.stack([c(LANES - r128), np.full(LANES, REL_MAX_DIST), np.zeros(LANES, np.int64)])
    nrow = _cmp_rows(past)
    t["cs"] = c(past - CMP_STRIDE * np.arange(nrow) - (CMP_BLOCK - 1))
    t["ws"] = c(WINDOW - np.arange(WINDOW))
    return t


def _cmp_rows(past):
    return -(-(past // CMP_STRIDE + 4) // 16) * 16


def _tab_kernel(idx_ref, rb_ref, o_ref):
    idx = idx_ref[...]
    b = lax.broadcasted_iota(I32, (REL_BUCKETS, idx.shape[1]), 0)
    oh = jnp.where(b == idx, 1.0, 0.0).astype(BF16)
    o_ref[...] = _mm(rb_ref[0], oh) + _mm(rb_ref[1], oh) + _mm(rb_ref[2], oh)


def _bias_tables(rel_bias, seq, past):
    pats = _dist_tables(seq, past)
    bucket = _bucket_of_dist()
    chunk = 16384
    flat, spans, off = [], {}, 0
    for name, d in pats.items():
        n = d.size
        pad = -n % LANES
        flat.append(bucket[d.reshape(-1)])
        flat.append(np.zeros(pad, np.int32))
        spans[name] = (off, n, d.shape)
        off += n + pad
    total = -(-off // chunk) * chunk
    flat.append(np.zeros(total - off, np.int32))
    idx = jnp.asarray(np.concatenate(flat).astype(np.int32)).reshape(1, total)
    rb3 = jnp.stack(_split3(rel_bias.T.astype(F32)))
    tab = pl.pallas_call(
        _tab_kernel,
        grid=(total // chunk,),
        in_specs=[pl.BlockSpec((1, chunk), lambda i: (0, i)),
                  pl.BlockSpec((3, N_HEADS, REL_BUCKETS), lambda i: (0, 0, 0))],
        out_specs=pl.BlockSpec((N_HEADS, chunk), lambda i: (0, i)),
        out_shape=jax.ShapeDtypeStruct((N_HEADS, total), F32),
        compiler_params=_cparams("parallel"),
        name="bias_tables",
    )(idx, rb3)
    return {name: tab[:, o:o + n].reshape((N_HEADS,) + shp) for name, (o, n, shp) in spans.items()}


def _group_mats():
    i = np.arange(2 * LANES)
    g64 = (i[:, None] // HEAD_DIM == i[None, :] // HEAD_DIM)
    g128 = (i[:, None] // LANES == i[None, :] // LANES) & ((i[:, None] % LANES) < HEAD_DIM)
    return jnp.asarray(g64, BF16), jnp.asarray(g128, BF16)


def _interleave_gain(g, n):
    return jnp.tile(jnp.concatenate([g.astype(F32), jnp.ones((HEAD_DIM,), F32)]), n).reshape(1, n * LANES)


def _store_rows_or_columns(ref, sl, x):
    if len(ref.shape) == 3:
        ref[0, sl, :] = x.T.astype(ref.dtype)
    else:
        ref[:, sl] = x.astype(ref.dtype)


def _token_major(x_t, b, s, n_kv):
    return x_t.reshape(b, n_kv, 2, HEAD_DIM, s).transpose(0, 4, 1, 2, 3)


def _proj_moba_kernel(x_ref, an_ref, wq_ref, wkv_ref, qg_ref, kg_ref, g64_ref, g128_ref,
                      q_ref, kv_ref, kvb_ref, *rest, n_mean):
    xn = _rms_rows(x_ref[...], an_ref[...]).astype(BF16)
    cw = 2 * LANES
    for c in range(wq_ref.shape[1] // cw):
        sl = slice(c * cw, (c + 1) * cw)
        h = _mm(xn, wq_ref[:, sl])
        q = _group_norm(h, g64_ref, qg_ref[:, sl], False)
        _store_rows_or_columns(q_ref, sl, q)
    for c in range(wkv_ref.shape[1] // cw):
        sl = slice(c * cw, (c + 1) * cw)
        h = _mm(xn, wkv_ref[:, sl])
        kv = _group_norm(h, g128_ref, kg_ref[:, sl], True)
        _store_rows_or_columns(kv_ref, sl, kv)
        kvb_ref[:, sl] = kv.astype(BF16)
        if n_mean:
            km_ref, kvtb_ref = rest
            _store_rows_or_columns(kvtb_ref, sl, kv)
            for r in range(n_mean):
                km_ref[0, r:r + 1, sl] = jnp.mean(kv[r * MOBA_BLOCK:(r + 1) * MOBA_BLOCK], axis=0, keepdims=True)


def _proj_moba(x, an, w_qkv, q_gain, k_gain, *, qdt, seq=None):
    m, d = x.shape
    tm = min(m, 512)
    with_mean = seq is not None
    g64, g128 = _group_mats()
    w3 = w_qkv.reshape(d, 3, N_HEADS, HEAD_DIM)
    wkv = jnp.stack([w3[:, 1], w3[:, 2]], axis=2).reshape(d, 2 * d).astype(BF16)
    wq = w3[:, 0].reshape(d, d).astype(BF16)
    qg = jnp.tile(q_gain.astype(F32), N_HEADS).reshape(1, d)
    kg = _interleave_gain(k_gain, N_HEADS)
    qn = wq.shape[1]
    n_mean = tm // MOBA_BLOCK if with_mean else 0
    const = lambda i: (0, 0)
    if seq is None:
        q_shape, q_spec = (m, qn), pl.BlockSpec((tm, qn), lambda i: (i, 0))
        kv_shape, kv_spec = (m, 2 * d), pl.BlockSpec((tm, 2 * d), lambda i: (i, 0))
    else:
        nt = seq // tm
        q_shape, q_spec = (m // seq, qn, seq), pl.BlockSpec((1, qn, tm), lambda i: (i // nt, 0, i % nt))
        kv_shape, kv_spec = (m // seq, 2 * d, seq), pl.BlockSpec((1, 2 * d, tm), lambda i: (i // nt, 0, i % nt))
    out_shape = [jax.ShapeDtypeStruct(q_shape, qdt), jax.ShapeDtypeStruct(kv_shape, F32),
                 jax.ShapeDtypeStruct((m, 2 * d), BF16)]
    out_specs = [q_spec, kv_spec, pl.BlockSpec((tm, 2 * d), lambda i: (i, 0))]
    if n_mean:
        out_shape += [jax.ShapeDtypeStruct((m // tm, n_mean, 2 * d), F32), jax.ShapeDtypeStruct(kv_shape, BF16)]
        out_specs += [pl.BlockSpec((1, n_mean, 2 * d), lambda i: (i, 0, 0)), kv_spec]
    return pl.pallas_call(
        functools.partial(_proj_moba_kernel, n_mean=n_mean),
        grid=(m // tm,),
        in_specs=[pl.BlockSpec((tm, d), lambda i: (i, 0)), pl.BlockSpec((1, d), const),
                  pl.BlockSpec((d, qn), const), pl.BlockSpec((d, 2 * d), const),
                  pl.BlockSpec((1, qn), const), pl.BlockSpec((1, 2 * d), const),
                  pl.BlockSpec(g64.shape, const), pl.BlockSpec(g128.shape, const)],
        out_specs=out_specs, out_shape=out_shape,
        compiler_params=_cparams("parallel"),
        name="proj_moba",
    )(x, an.reshape(1, d).astype(F32), wq, wkv, qg, kg, g64, g128)


def _proj_nsa_kernel(x_ref, an_ref, wq_ref, wkv_ref, wg_ref, gb_ref, qg_ref, kg_ref, g64_ref, g128_ref,
                     q_ref, cmp_ref, sel_ref, win_ref, selb_ref, winb_ref, gate_ref, *rest):
    cmp_t_ref, selb_t_ref, winb_t_ref = rest if rest else (None, None, None)
    xn = _rms_rows(x_ref[...], an_ref[...]).astype(BF16)
    cw = 2 * LANES
    for c in range(wq_ref.shape[1] // cw):
        sl = slice(c * cw, (c + 1) * cw)
        h = _mm(xn, wq_ref[:, sl])
        _store_rows_or_columns(q_ref, sl, _group_norm(h, g64_ref, qg_ref[:, sl], False))
    per_branch = NSA_KV_HEADS * LANES // cw
    for c in range(wkv_ref.shape[1] // cw):
        sl = slice(c * cw, (c + 1) * cw)
        br, cc = divmod(c, per_branch)
        osl = slice(cc * cw, (cc + 1) * cw)
        h = _mm(xn, wkv_ref[:, sl])
        if br == 0:
            cmp_ref[:, osl] = h
            if cmp_t_ref is not None:
                _store_rows_or_columns(cmp_t_ref, osl, h)
        else:
            kv = _group_norm(h, g128_ref, kg_ref[:, sl], True)
            o32, o16, o16t = (sel_ref, selb_ref, selb_t_ref) if br == 1 else (win_ref, winb_ref, winb_t_ref)
            _store_rows_or_columns(o32, osl, kv)
            o16[:, osl] = kv.astype(BF16)
            if o16t is not None:
                _store_rows_or_columns(o16t, osl, kv)
    hg = _mm(xn, wg_ref[...]) + gb_ref[...]
    _store_rows_or_columns(gate_ref, slice(0, hg.shape[1]), 1.0 / (1.0 + jnp.exp(-hg)))


def _proj_nsa(x, an, w_in, gate_bias, q_gain, k_gain, *, qdt, seq=None):
    m, d = x.shape
    tm = min(m, 512)
    g64, g128 = _group_mats()
    kvd = N_BRANCH * NSA_KV_HEADS * LANES
    gw = NSA_KV_HEADS * LANES
    ng = NSA_GROUP * N_BRANCH
    qg = jnp.tile(q_gain.astype(F32), N_HEADS).reshape(1, d)
    wq = w_in[:, :d].astype(BF16)
    wkv = w_in[:, d:d + kvd].astype(BF16)
    wg = jnp.zeros((d, NSA_KV_HEADS, LANES), F32).at[:, :, :ng].set(
        w_in[:, d + kvd:].reshape(d, NSA_KV_HEADS, ng)).reshape(d, gw).astype(BF16)
    gb = jnp.zeros((NSA_KV_HEADS, LANES), F32).at[:, :ng].set(
        gate_bias.astype(F32).reshape(NSA_KV_HEADS, ng)).reshape(1, gw)
    kg = jnp.concatenate([jnp.ones((1, gw), F32), _interleave_gain(k_gain[1], NSA_KV_HEADS),
                          _interleave_gain(k_gain[2], NSA_KV_HEADS)], axis=1)
    qn = wq.shape[1]
    const = lambda i: (0, 0)
    row = lambda n: pl.BlockSpec((tm, n), lambda i: (i, 0))
    rows32 = jax.ShapeDtypeStruct((m, gw), F32)
    if seq is None:
        q_shape, q_spec = jax.ShapeDtypeStruct((m, qn), qdt), row(qn)
        kv_shape, kv_spec, gate_shape, gate_spec, extra_shape, extra_spec = rows32, row(gw), rows32, row(gw), [], []
    else:
        nt = seq // tm
        cols = lambda n: pl.BlockSpec((1, n, tm), lambda i: (i // nt, 0, i % nt))
        q_shape, q_spec = jax.ShapeDtypeStruct((m // seq, qn, seq), qdt), cols(qn)
        kv_shape, kv_spec = jax.ShapeDtypeStruct((m // seq, gw, seq), F32), cols(gw)
        gate_shape, gate_spec = kv_shape, kv_spec
        extra_shape = [kv_shape] + [jax.ShapeDtypeStruct((m // seq, gw, seq), BF16)] * 2
        extra_spec = [kv_spec] * 3
    return pl.pallas_call(
        _proj_nsa_kernel,
        grid=(m // tm,),
        in_specs=[row(d), pl.BlockSpec((1, d), const), pl.BlockSpec((d, qn), const),
                  pl.BlockSpec((d, kvd), const), pl.BlockSpec((d, gw), const), pl.BlockSpec((1, gw), const),
                  pl.BlockSpec((1, qn), const), pl.BlockSpec((1, kvd), const),
                  pl.BlockSpec(g64.shape, const), pl.BlockSpec(g128.shape, const)],
        out_specs=[q_spec, row(gw), kv_spec, kv_spec, row(gw), row(gw), gate_spec] + extra_spec,
        out_shape=[q_shape, rows32, kv_shape, kv_shape]
                  + [jax.ShapeDtypeStruct((m, gw), BF16)] * 2 + [gate_shape] + extra_shape,
        compiler_params=_cparams("parallel"),
        name="proj_nsa",
    )(x, an.reshape(1, d).astype(F32), wq, wkv, wg, gb, qg, kg, g64, g128)


def _mlp_kernel(h_ref, o_ref, wo_ref, g_ref, wup_ref, wdn_ref, out_ref, xn_s):
    @pl.when(pl.program_id(1) == 0)
    def _():
        h1 = h_ref[...] + _mm(o_ref[...], wo_ref[...])
        out_ref[...] = h1
        xn_s[...] = _rms_rows(h1, g_ref[...]).astype(BF16)

    u = _mm(xn_s[...], wup_ref[...])
    u = jnp.square(jnp.maximum(u, 0.0)).astype(BF16)
    out_ref[...] += _mm(u, wdn_ref[...])


def _attn_out_mlp(h, o, w_o, fn, w_up, w_down):
    m, d = h.shape
    dff = w_up.shape[1]
    tm = min(m, 1024)
    tf = 1024
    return pl.pallas_call(
        _mlp_kernel,
        grid=(m // tm, dff // tf),
        in_specs=[pl.BlockSpec((tm, d), lambda i, f: (i, 0)), pl.BlockSpec((tm, d), lambda i, f: (i, 0)),
                  pl.BlockSpec((d, d), lambda i, f: (0, 0)), pl.BlockSpec((1, d), lambda i, f: (0, 0)),
                  pl.BlockSpec((d, tf), lambda i, f: (0, f)), pl.BlockSpec((tf, d), lambda i, f: (f, 0))],
        out_specs=pl.BlockSpec((tm, d), lambda i, f: (i, 0)),
        out_shape=jax.ShapeDtypeStruct((m, d), F32),
        scratch_shapes=[pltpu.VMEM((tm, d), BF16)],
        compiler_params=_cparams("parallel", "arbitrary"),
        name="attn_out_mlp",
    )(h, o.astype(BF16), w_o.astype(BF16), fn.reshape(1, d).astype(F32), w_up.astype(BF16), w_down.astype(BF16))


def _topk_rows(x, k, on_pick):
    n = x.shape[0]
    row = lax.broadcasted_iota(I32, x.shape, 0)
    for _ in range(k):
        m = jnp.max(x, axis=0, keepdims=True)
        idx = jnp.min(jnp.where(x == m, row, n), axis=0, keepdims=True)
        hit = row == idx
        on_pick(hit, m)
        x = jnp.where(hit, -jnp.inf, x)


SUM_ROWS = 16


def _with_sum_rows(v_t):
    return jnp.concatenate([v_t, jnp.ones((SUM_ROWS, v_t.shape[1]), v_t.dtype)], axis=0)


def _online_softmax_step(m_ref, acc_ref, tiles):
    m_old = m_ref[0]
    mx = jnp.max(tiles[0][0], axis=0, keepdims=True)
    for s, _ in tiles[1:]:
        mx = jnp.maximum(mx, jnp.max(s, axis=0, keepdims=True))
    mn = jnp.maximum(m_old, mx)
    acc = jnp.exp(m_old - mn) * acc_ref[0]
    for s, v_aug in tiles:
        acc = acc + _mm(v_aug, jnp.exp(s - mn).astype(BF16))
    m_ref[0] = mn
    acc_ref[0] = acc


def _softmax_finish(acc):
    return acc[:HEAD_DIM] * (1.0 / jnp.maximum(acc[HEAD_DIM:HEAD_DIM + 1], 1e-30))


def _unrolled_range(n, unroll, visit):
    rem = lax.rem(n, unroll)
    done = 0
    size = 1
    while size < unroll:
        take = jnp.bitwise_and(rem, size)

        @pl.when(take != 0)
        def _(done=done, size=size):
            visit([done + k for k in range(size)])

        done = done + take
        size *= 2

    def body(i, carry):
        first = rem + unroll * i
        visit([first + k for k in range(unroll)])
        return carry

    lax.fori_loop(0, n // unroll, body, 0)


def _softmax_state_init(m_ref, acc_ref):
    m_ref[...] = jnp.full(m_ref.shape, -jnp.inf, F32)
    acc_ref[...] = jnp.zeros(acc_ref.shape, F32)


def _moba_prompt_kernel(qt_ref, kv_ref, kvt_ref, km_ref, tb_ref, o_ref, *state, nb):
    cur = pl.program_id(2)
    tq = MOBA_BLOCK
    key_lane = lax.broadcasted_iota(I32, (tq, LANES), 1)
    rown = lax.broadcasted_iota(I32, (nb, tq), 0)
    heads = [state[0:2], state[2:4]]
    for st in heads:
        _softmax_state_init(*st)
    qzs, qaugs = [], []
    for hh in range(2):
        qt = qt_ref[0, hh * HEAD_DIM:(hh + 1) * HEAD_DIM, :] * jnp.asarray(ATTN_SCALE, BF16)
        qz = jnp.concatenate([qt, jnp.zeros_like(qt)], axis=0)
        km_hi, km_lo = _split2(km_ref[0, :, hh * LANES:(hh + 1) * LANES])
        gate = _mm(km_hi, qz) + _mm(km_lo, qz)
        state = [jnp.full((nb, tq), MASK_NEG, F32)]

        def pick(hit, m, state=state):
            state[0] = jnp.where(hit, jnp.where(m > -jnp.inf, 0.0, state[0]), state[0])

        _topk_rows(jnp.where(rown < cur, gate, -jnp.inf), MOBA_TOPK, pick)
        negt = jnp.concatenate([state[0], jnp.zeros((LANES - nb, tq), F32)], axis=0).astype(BF16)
        qzs.append(qz)
        qaugs.append(jnp.concatenate([qz, negt], axis=0))

    def k_rows(hh, n):
        return kv_ref[0, pl.ds(pl.multiple_of(n * tq, tq), tq), hh * LANES:(hh + 1) * LANES]

    def v_cols(hh, n):
        return _with_sum_rows(
            kvt_ref[0, hh * LANES + HEAD_DIM:(hh + 1) * LANES, pl.ds(pl.multiple_of(n * tq, tq), tq)])

    for hh in range(2):
        _online_softmax_step(*heads[hh], [(_mm(k_rows(hh, cur), qzs[hh]) + tb_ref[hh, 0], v_cols(hh, cur))])

    def past_blocks(blocks):
        for hh in range(2):
            tiles = []
            for n in blocks:
                onehot = jnp.where(key_lane == n, 1.0, 0.0).astype(BF16)
                s = _mm(jnp.concatenate([k_rows(hh, n), onehot], axis=1), qaugs[hh])
                tiles.append((s + tb_ref[hh, jnp.where(cur - n == 1, 1, 2)], v_cols(hh, n)))
            _online_softmax_step(*heads[hh], tiles)

    _unrolled_range(cur, MOBA_UNROLL, past_blocks)
    o_t = jnp.concatenate([_softmax_finish(acc[0]) for _, acc in heads], axis=0)
    o_ref[...] = o_t.T.astype(o_ref.dtype)


def _moba_prompt(qt, kvb, kvt, kmean, tb, b, s):
    d = qt.shape[1]
    nb = s // MOBA_BLOCK
    assert s % MOBA_BLOCK == 0 and MOBA_TOPK <= nb <= LANES
    nq = s // MOBA_BLOCK
    r = np.arange(MOBA_BLOCK)
    causal = np.zeros((3, MOBA_BLOCK, MOBA_BLOCK), np.float32)
    causal[0] = np.where(r[:, None] <= r[None, :], 0.0, -np.inf)
    tb = tb + jnp.asarray(causal)
    return pl.pallas_call(
        functools.partial(_moba_prompt_kernel, nb=nb),
        grid=(b, N_HEADS // 2, nq),
        in_specs=[pl.BlockSpec((1, LANES, MOBA_BLOCK), lambda i, h, t: (i, h, t)),
                  pl.BlockSpec((1, s, 2 * LANES), lambda i, h, t: (i, 0, h)),
                  pl.BlockSpec((1, 2 * LANES, s), lambda i, h, t: (i, h, 0)),
                  pl.BlockSpec((1, nb, 2 * LANES), lambda i, h, t: (i, 0, h)),
                  pl.BlockSpec((2, 3, MOBA_BLOCK, MOBA_BLOCK), lambda i, h, t: (h, 0, 0, 0))],
        out_specs=pl.BlockSpec((MOBA_BLOCK, LANES), lambda i, h, t: (i * nq + t, h)),
        out_shape=jax.ShapeDtypeStruct((b * s, d), BF16),
        scratch_shapes=[pltpu.VMEM((1, 1, MOBA_BLOCK), F32),
                        pltpu.VMEM((1, HEAD_DIM + SUM_ROWS, MOBA_BLOCK), F32)] * 2,
        compiler_params=_cparams("parallel", "parallel", "arbitrary"),
        name="moba_prompt",
    )(qt, kvb.reshape(b, s, 2 * d), kvt, kmean.reshape(b, nb, 2 * d), tb)


def _gelu_tanh(x):
    return 0.5 * x * (1.0 + jnp.tanh(math.sqrt(2.0 / math.pi) * (x + 0.044715 * (x * x * x))))


def _compress_weights(cmp_pos, w1, b1, w2, k_gain0):
    half = CMP_BLOCK // 2
    z = jnp.zeros((half, HEAD_DIM, CMP_HIDDEN), F32)

    def first_layer(lo):
        wk, wv = w1[0, lo:lo + half], w1[1, lo:lo + half]
        top = jnp.concatenate([wk, z], axis=2)
        bot = jnp.concatenate([z, wv], axis=2)
        return jnp.concatenate([top, bot], axis=1).reshape(half * LANES, 2 * CMP_HIDDEN).astype(BF16)

    zz = jnp.zeros((CMP_HIDDEN, HEAD_DIM), F32)
    w2bd = jnp.concatenate([jnp.concatenate([w2[0], zz], axis=1),
                            jnp.concatenate([zz, w2[1]], axis=1)], axis=0).astype(BF16)
    pos_a = cmp_pos[:half].reshape(half, LANES).astype(F32)
    pos_b = cmp_pos[half:].reshape(half, LANES).astype(F32)
    return (pos_a, pos_b, first_layer(0), first_layer(half), b1.reshape(1, 2 * CMP_HIDDEN).astype(F32), w2bd,
            _interleave_gain(k_gain0, 1))


def _compress_tail(xa_s, xb_s, wa_ref, wb_ref, b1_ref, w2_ref, kg_ref):
    ha = _mm(xa_s[...], wa_ref[...])
    hb = _mm(xb_s[...], wb_ref[...])
    rows = ha.shape[0]
    h = ha + pltpu.roll(hb, rows - 1, 0) + b1_ref[...]
    out = _mm(_gelu_tanh(h).astype(BF16), w2_ref[...])
    lane = lax.broadcasted_iota(I32, out.shape, 1)
    is_k = lane < HEAD_DIM
    ss = jnp.sum(jnp.where(is_k, out * out, 0.0), axis=1, keepdims=True)
    kn = out * lax.rsqrt(ss * (1.0 / HEAD_DIM) + NORM_EPS) * kg_ref[...]
    return jnp.where(is_k, kn, out)


def _compress_prompt_kernel(*refs, nseg):
    x_refs = refs[:NSA_KV_HEADS]
    pa_ref, pb_ref, wa_ref, wb_ref, b1_ref, w2_ref, kg_ref, o_ref, ot_ref, xa_s, xb_s = refs[NSA_KV_HEADS:]
    half = CMP_BLOCK // 2
    for g in range(NSA_KV_HEADS):
        for p in range(half):
            v = x_refs[g][0, pl.ds(p, nseg, stride=CMP_STRIDE), :]
            xa_s[g * nseg:(g + 1) * nseg, p * LANES:(p + 1) * LANES] = (v + pa_ref[p:p + 1, :]).astype(BF16)
            xb_s[g * nseg:(g + 1) * nseg, p * LANES:(p + 1) * LANES] = (v + pb_ref[p:p + 1, :]).astype(BF16)
    res = _compress_tail(xa_s, xb_s, wa_ref, wb_ref, b1_ref, w2_ref, kg_ref)
    for g in range(NSA_KV_HEADS):
        rows = res[g * nseg:(g + 1) * nseg]
        o_ref[0, g] = rows.astype(BF16)
        ot_ref[0, g] = rows.T.astype(BF16)


def _compress_prompt(kv_cmp, cw, b, s):
    nseg = s // CMP_STRIDE
    gw = NSA_KV_HEADS * LANES
    kdim = (CMP_BLOCK // 2) * LANES
    const = lambda i: (0, 0)
    return pl.pallas_call(
        functools.partial(_compress_prompt_kernel, nseg=nseg),
        grid=(b,),
        in_specs=[pl.BlockSpec((1, s, LANES), functools.partial(lambda i, g: (i, 0, g), g=g))
                  for g in range(NSA_KV_HEADS)] + [pl.BlockSpec(w.shape, const) for w in cw],
        out_specs=[pl.BlockSpec((1, NSA_KV_HEADS, nseg, LANES), lambda i: (i, 0, 0, 0)),
                   pl.BlockSpec((1, NSA_KV_HEADS, LANES, nseg), lambda i: (i, 0, 0, 0))],
        out_shape=[jax.ShapeDtypeStruct((b, NSA_KV_HEADS, nseg, LANES), BF16),
                   jax.ShapeDtypeStruct((b, NSA_KV_HEADS, LANES, nseg), BF16)],
        scratch_shapes=[pltpu.VMEM((NSA_KV_HEADS * nseg, kdim), BF16)] * 2,
        compiler_params=_cparams("parallel"),
        name="compress_prompt",
    )(*([kv_cmp.reshape(b, s, gw)] * NSA_KV_HEADS), *cw)


CMP_PAGES = 4


def _compress_sample_kernel(pt_ref, *refs, nseg, nrow):
    x_refs = refs[:CMP_PAGES]
    (new_ref, pa_ref, pb_ref, wa_ref, wb_ref, b1_ref, w2_ref, kg_ref, o_ref,
     xa_s, xb_s, xt_s) = refs[CMP_PAGES:]
    j = pl.program_id(1)
    for pg in range(CMP_PAGES):
        for g in range(NSA_KV_HEADS):
            xt_s[pg, g] = x_refs[pg][0, g].T
    half = CMP_BLOCK // 2
    per_page = PAGE_SIZE // CMP_STRIDE
    per_step = CMP_PAGES * per_page
    tail = nrow - nseg

    @pl.when(j == 0)
    def _():
        first = lax.broadcasted_iota(I32, (tail, LANES), 0) == 0
        for g in range(NSA_KV_HEADS):
            rows = slice(g * nrow + nseg, (g + 1) * nrow)
            for p in range(half):
                v = jnp.zeros((tail, LANES), F32)
                if p == 0:
                    v = jnp.where(first, new_ref[0, :, g * LANES:(g + 1) * LANES], 0.0)
                xa_s[rows, p * LANES:(p + 1) * LANES] = (v + pa_ref[p:p + 1, :]).astype(BF16)
                xb_s[rows, p * LANES:(p + 1) * LANES] = (v + pb_ref[p:p + 1, :]).astype(BF16)

    for g in range(NSA_KV_HEADS):
        r0 = pl.multiple_of(g * nrow + j * per_step, 16)
        for p in range(half):
            v = jnp.concatenate([xt_s[pg, g, pl.ds(p, per_page, stride=CMP_STRIDE), :]
                                 for pg in range(CMP_PAGES)], axis=0)
            xa_s[pl.ds(r0, per_step), p * LANES:(p + 1) * LANES] = (v + pa_ref[p:p + 1, :]).astype(BF16)
            xb_s[pl.ds(r0, per_step), p * LANES:(p + 1) * LANES] = (v + pb_ref[p:p + 1, :]).astype(BF16)

    @pl.when(j == pl.num_programs(1) - 1)
    def _():
        res = _compress_tail(xa_s, xb_s, wa_ref, wb_ref, b1_ref, w2_ref, kg_ref)
        for g in range(NSA_KV_HEADS):
            o_ref[0, g] = res[g * nrow:(g + 1) * nrow]


def _compress_sample(cache, page_table, new_rows, cw, past):
    db, n_pages = page_table.shape
    assert n_pages % CMP_PAGES == 0 and (CMP_PAGES * PAGE_SIZE // CMP_STRIDE) % 16 == 0
    gw = NSA_KV_HEADS * LANES
    nseg = past // CMP_STRIDE
    nrow = _cmp_rows(past)
    kdim = (CMP_BLOCK // 2) * LANES
    const = lambda i, j, pt: (0, 0)
    grid_spec = pltpu.PrefetchScalarGridSpec(
        num_scalar_prefetch=1,
        grid=(db, n_pages // CMP_PAGES),
        in_specs=[pl.BlockSpec((1, NSA_KV_HEADS, LANES, PAGE_SIZE),
                               functools.partial(lambda i, j, pt, pg: (pt[i * n_pages + CMP_PAGES * j + pg], 0, 0, 0),
                                                 pg=pg))
                  for pg in range(CMP_PAGES)]
                 + [pl.BlockSpec((1, 1, gw), lambda i, j, pt: (i, 0, 0))]
                 + [pl.BlockSpec(w.shape, const) for w in cw],
        out_specs=pl.BlockSpec((1, NSA_KV_HEADS, nrow, LANES), lambda i, j, pt: (i, 0, 0, 0)),
        scratch_shapes=[pltpu.VMEM((NSA_KV_HEADS * nrow, kdim), BF16)] * 2
                       + [pltpu.VMEM((CMP_PAGES, NSA_KV_HEADS, PAGE_SIZE, LANES), F32)],
    )
    cview = _slot_minor(cache, NSA_KV_HEADS)
    return pl.pallas_call(
        functools.partial(_compress_sample_kernel, nseg=nseg, nrow=nrow),
        grid_spec=grid_spec,
        out_shape=jax.ShapeDtypeStruct((db, NSA_KV_HEADS, nrow, LANES), F32),
        compiler_params=_cparams("parallel", "arbitrary"),
        name="compress_sample",
    )(page_table.reshape(-1), *([cview] * CMP_PAGES), new_rows.reshape(db, 1, gw), *cw)


def _imp_matrix(n_sel, n_rows, n_cmp):
    ratio = SEL_BLOCK // CMP_STRIDE
    lead = CMP_BLOCK // CMP_STRIDE - 1
    j = np.arange(n_sel)[:, None]
    n = np.arange(n_rows)[None, :]
    return ((n >= ratio * j - lead) & (n <= ratio * j + ratio - 1) & (n < n_cmp))


def _nsa_prompt_kernel(qt_ref, gt_ref, kc_ref, kct_ref, ks_ref, kst_ref, kw_ref, kwt_ref, tsel_ref, tw_ref,
                       tc_ref, at_ref, o_ref, m_s, acc_s, *, n_sel, ncp):
    t = pl.program_id(2)
    tq, tk, grp = NSA_TQ, NSA_TK, NSA_GROUP
    cols = grp * tq
    p0 = t * tq
    scale = jnp.asarray(ATTN_SCALE, BF16)
    qt = jnp.concatenate([qt_ref[0, p * HEAD_DIM:(p + 1) * HEAD_DIM, :] * scale for p in range(grp)], axis=1)
    qz = jnp.concatenate([qt, jnp.zeros_like(qt)], axis=0)
    qoff = jnp.bitwise_and(lax.broadcasted_iota(I32, (1, cols), 1), tq - 1)

    shift = lax.rem(t * (tq // CMP_STRIDE) - CMP_NEAR // 2 + ncp, ncp)
    c_end = lax.broadcasted_iota(I32, (ncp, cols), 0) * CMP_STRIDE + (CMP_BLOCK - 1)
    s_c = jnp.where(p0 + qoff >= c_end, _mm(kc_ref[0, 0], qz) + pltpu.roll(tc_ref[0], shift, 0), -jnp.inf)
    m_c = jnp.max(s_c, axis=0, keepdims=True)
    e_c = jnp.exp(s_c - jnp.where(m_c == -jnp.inf, 0.0, m_c))
    p_c = e_c * (1.0 / jnp.maximum(jnp.sum(e_c, axis=0, keepdims=True), 1e-30))
    o_c = _mm(kct_ref[0, 0, HEAD_DIM:, :], p_c.astype(BF16))
    p_sum = p_c[:, 0:tq]
    for p in range(1, grp):
        p_sum = p_sum + p_c[:, p * tq:(p + 1) * tq]
    at = at_ref[...]
    imp = sum(_mm(at, part) for part in _split3(p_sum))

    blk = lax.broadcasted_iota(I32, (n_sel, tq), 0)
    cur = jnp.right_shift(p0 + lax.broadcasted_iota(I32, (n_sel, tq), 1), SEL_SHIFT)
    valid = blk <= cur
    forced = (blk == 0) | (blk >= cur - (N_LOCAL_SEL - 1))
    score = jnp.where(valid, jnp.where(forced, jnp.inf, imp), -jnp.inf)
    state = [jnp.full((n_sel, tq), MASK_NEG, F32)]

    def pick(hit, m):
        state[0] = jnp.where(hit, jnp.where(m > -jnp.inf, 0.0, state[0]), state[0])

    _topk_rows(score, SEL_TOPK, pick)
    negt = state[0]
    if n_sel < LANES:
        negt = jnp.concatenate([negt, jnp.zeros((LANES - n_sel, tq), F32)], axis=0)
    negt = negt.astype(BF16)
    qaug = jnp.concatenate([qz, jnp.concatenate([negt] * grp, axis=1)], axis=0)

    kj = lax.broadcasted_iota(I32, (tk, LANES), 0)
    kl = lax.broadcasted_iota(I32, (tk, LANES), 1)

    def sel_tile(kt, variant):
        at_kt = pl.ds(pl.multiple_of(kt * tk, tk), tk)
        onehot = jnp.where(kl == kt * (tk // SEL_BLOCK) + jnp.right_shift(kj, SEL_SHIFT), 1.0, 0.0).astype(BF16)
        s = _mm(jnp.concatenate([ks_ref[0, at_kt, :], onehot], axis=1), qaug) + tsel_ref[0, variant]
        return s, _with_sum_rows(kst_ref[0, HEAD_DIM:, at_kt])

    _softmax_state_init(m_s, acc_s)
    kt_d = t // (tk // tq)
    _online_softmax_step(m_s, acc_s, [sel_tile(kt_d, lax.rem(t, tk // tq))])

    def past_tiles(kts):
        _online_softmax_step(m_s, acc_s, [sel_tile(kt, jnp.minimum((p0 - kt * tk) // tq, 3)) for kt in kts])

    _unrolled_range(kt_d, NSA_UNROLL, past_tiles)
    o_s = _softmax_finish(acc_s[0])

    at_w = pl.ds(pl.multiple_of(p0, tq), WIN_KEYS)
    wj = lax.broadcasted_iota(I32, (WIN_KEYS, cols), 0)
    s_w = jnp.where(p0 + wj >= WINDOW, _mm(kw_ref[0, at_w, :], qz) + tw_ref[0], -jnp.inf)
    m_w = jnp.max(s_w, axis=0, keepdims=True)
    e_w = jnp.exp(s_w - jnp.where(m_w == -jnp.inf, 0.0, m_w))
    o_w = _softmax_finish(_mm(_with_sum_rows(kwt_ref[0, HEAD_DIM:, at_w]), e_w.astype(BF16)))

    gt = gt_ref[0]
    per_head = []
    for p in range(grp):
        c = slice(p * tq, (p + 1) * tq)
        g = [gt[N_BRANCH * p + k:N_BRANCH * p + k + 1, :] for k in range(N_BRANCH)]
        per_head.append(g[0] * o_c[:, c] + g[1] * o_s[:, c] + g[2] * o_w[:, c])
    o_ref[...] = jnp.concatenate(per_head, axis=0).T.astype(o_ref.dtype)


def _heads_on_lanes(tab, lead):
    n = len(lead)
    x = tab.reshape((NSA_KV_HEADS, NSA_GROUP) + tab.shape[1:])
    perm = (0,) + tuple(range(2, 3 + n)) + (1, 3 + n)
    x = x.transpose(perm)
    return x.reshape(x.shape[:2 + n] + (NSA_GROUP * tab.shape[-1],))


def _nsa_prompt(qt, gt, kcvc, kcvc_t, selb, selb_t, winb, winb_t, tabs, b, s):
    d = qt.shape[1]
    gw = NSA_KV_HEADS * LANES
    ncp = s // CMP_STRIDE
    n_sel = s // SEL_BLOCK
    assert s % NSA_TK == 0 and SEL_TOPK <= n_sel <= LANES and ncp >= CMP_NEAR
    nq = s // NSA_TQ
    at = jnp.asarray(_imp_matrix(n_sel, ncp, ncp - 1), BF16)
    winp = jnp.pad(winb.reshape(b, s, gw), ((0, 0), (WINDOW, 0), (0, 0)))
    winp_t = jnp.pad(winb_t, ((0, 0), (0, 0), (WINDOW, 0)))
    gq = NSA_GROUP
    cols = gq * NSA_TQ
    kr = np.arange(NSA_TK)[:, None]
    wr = np.arange(WIN_KEYS)[:, None]
    qc = np.arange(cols)[None, :] % NSA_TQ
    sel_mask = np.zeros((4, NSA_TK, cols), np.float32)
    for v in range(NSA_TK // NSA_TQ):
        sel_mask[v] = np.where(kr <= qc + v * NSA_TQ, 0.0, -np.inf)
    win_mask = np.where((wr > qc) & (wr <= qc + WINDOW), 0.0, -np.inf).astype(np.float32)
    tsel = _heads_on_lanes(tabs["sel"], (4,)) + jnp.asarray(sel_mask)
    tw = _heads_on_lanes(tabs["win"], ()) + jnp.asarray(win_mask)
    tc = _heads_on_lanes(tabs["cmp"], ())
    return pl.pallas_call(
        functools.partial(_nsa_prompt_kernel, n_sel=n_sel, ncp=ncp),
        grid=(b, NSA_KV_HEADS, nq),
        in_specs=[pl.BlockSpec((1, gq * HEAD_DIM, NSA_TQ), lambda i, g, t: (i, g, t)),
                  pl.BlockSpec((1, LANES, NSA_TQ), lambda i, g, t: (i, g, t)),
                  pl.BlockSpec((1, 1, ncp, LANES), lambda i, g, t: (i, g, 0, 0)),
                  pl.BlockSpec((1, 1, LANES, ncp), lambda i, g, t: (i, g, 0, 0)),
                  pl.BlockSpec((1, s, LANES), lambda i, g, t: (i, 0, g)),
                  pl.BlockSpec((1, LANES, s), lambda i, g, t: (i, g, 0)),
                  pl.BlockSpec((1, s + WINDOW, LANES), lambda i, g, t: (i, 0, g)),
                  pl.BlockSpec((1, LANES, s + WINDOW), lambda i, g, t: (i, g, 0)),
                  pl.BlockSpec((1, 4, NSA_TK, cols), lambda i, g, t: (g, 0, 0, 0)),
                  pl.BlockSpec((1, WIN_KEYS, cols), lambda i, g, t: (g, 0, 0)),
                  pl.BlockSpec((1, ncp, cols), lambda i, g, t: (g, 0, 0)),
                  pl.BlockSpec(at.shape, lambda i, g, t: (0, 0))],
        out_specs=pl.BlockSpec((NSA_TQ, gq * HEAD_DIM), lambda i, g, t: (i * nq + t, g)),
        out_shape=jax.ShapeDtypeStruct((b * s, d), BF16),
        scratch_shapes=[pltpu.VMEM((1, 1, cols), F32), pltpu.VMEM((1, HEAD_DIM + SUM_ROWS, cols), F32)],
        compiler_params=_cparams("parallel", "parallel", "arbitrary"),
        name="nsa_prompt",
    )(qt, gt, kcvc, kcvc_t, selb.reshape(b, s, gw), selb_t, winp, winp_t, tsel, tw, tc, at)


def _topk_lanes(x, k):
    ax = x.ndim - 1
    lane = lax.broadcasted_iota(I32, x.shape, ax)
    oshape = x.shape[:-1] + (LANES,)
    out_lane = lax.broadcasted_iota(I32, oshape, ax)
    ids = jnp.zeros(oshape, I32)
    for r in range(k):
        m = jnp.max(x, axis=ax, keepdims=True)
        idx = jnp.min(jnp.where(x == m, lane, x.shape[ax]), axis=ax, keepdims=True)
        ids = jnp.where(out_lane == r, idx, ids)
        x = jnp.where(lane == idx, -jnp.inf, x)
    return ids


def _slot_minor(cache, n_kv):
    pages, slots = cache.shape[:2]
    return jnp.transpose(cache, (0, 2, 3, 4, 1)).reshape(pages, n_kv, LANES, slots)


MOBA_SELECT_PAGES = 8


def _moba_select_kernel(pt_ref, *refs, nb):
    x_refs = refs[:MOBA_SELECT_PAGES]
    q_ref, ids_ref, ksum_s = refs[MOBA_SELECT_PAGES:]
    j = pl.program_id(1)
    ppb = MOBA_BLOCK // PAGE_SIZE
    per_step = MOBA_SELECT_PAGES // ppb

    @pl.when(j == 0)
    def _():
        ksum_s[...] = jnp.zeros_like(ksum_s)

    lane = lax.broadcasted_iota(I32, ksum_s.shape, 2)
    acc = ksum_s[...]
    for bi in range(per_step):
        x = x_refs[bi * ppb][0]
        for pg in range(1, ppb):
            x = x + x_refs[bi * ppb + pg][0]
        col = jnp.sum(x, axis=2, keepdims=True)
        acc = acc + jnp.where(lane == j * per_step + bi, col, 0.0)
    ksum_s[...] = acc

    @pl.when(j == pl.num_programs(1) - 1)
    def _():
        km = ksum_s[...] * (1.0 / MOBA_BLOCK)
        gate = sum(jnp.einsum("hqd,hdn->hqn", a, b_, preferred_element_type=F32)
                   for a in _split3(q_ref[0]) for b_ in _split3(km))
        blk = lax.broadcasted_iota(I32, gate.shape, 2)
        ids_ref[0] = _topk_lanes(jnp.where(blk < nb, gate, -jnp.inf), MOBA_TOPK)


def _moba_select(cache_t, page_table, q, past):
    db, n_pages = page_table.shape
    nb = past // MOBA_BLOCK
    assert past % MOBA_BLOCK == 0 and MOBA_TOPK <= nb <= LANES and n_pages % MOBA_SELECT_PAGES == 0
    q8 = jnp.pad(q.reshape(db, N_HEADS, 1, HEAD_DIM), ((0, 0), (0, 0), (0, 7), (0, 0)))
    grid_spec = pltpu.PrefetchScalarGridSpec(
        num_scalar_prefetch=1,
        grid=(db, n_pages // MOBA_SELECT_PAGES),
        in_specs=[pl.BlockSpec((1, N_HEADS, HEAD_DIM, PAGE_SIZE),
                               functools.partial(lambda i, j, pt, pg: (pt[i * n_pages + MOBA_SELECT_PAGES * j + pg],
                                                                       0, 0, 0), pg=pg))
                  for pg in range(MOBA_SELECT_PAGES)]
                 + [pl.BlockSpec((1, N_HEADS, 8, HEAD_DIM), lambda i, j, pt: (i, 0, 0, 0))],
        out_specs=pl.BlockSpec((1, N_HEADS, 8, LANES), lambda i, j, pt: (i, 0, 0, 0)),
        scratch_shapes=[pltpu.VMEM((N_HEADS, HEAD_DIM, LANES), F32)],
    )
    ids = pl.pallas_call(
        functools.partial(_moba_select_kernel, nb=nb),
        grid_spec=grid_spec,
        out_shape=jax.ShapeDtypeStruct((db, N_HEADS, 8, LANES), I32),
        compiler_params=_cparams("parallel", "arbitrary"),
        name="moba_select",
    )(page_table.reshape(-1), *([cache_t] * MOBA_SELECT_PAGES), q8)
    return ids[:, :, 0, :MOBA_TOPK]


def _pad_rows(x, rows):
    return jnp.concatenate([x, jnp.zeros((rows - x.shape[0],) + x.shape[1:], x.dtype)], axis=0)


MOBA_SAMPLE_HEADS = 4


def _moba_sample_kernel(pt_ref, ids_ref, q_ref, new_ref, *rest, n_tiles, last_blk, pages_per_block):
    tiles = rest[:MOBA_SAMPLE_HEADS * n_tiles]
    tab_ref, o_ref = rest[MOBA_SAMPLE_HEADS * n_tiles:]
    i, hg = pl.program_id(0), pl.program_id(1)
    for hh in range(MOBA_SAMPLE_HEADS):
        h = hg * MOBA_SAMPLE_HEADS + hh
        q8 = q_ref[0, hh] * ATTN_SCALE
        q8b = q8.astype(BF16)
        new = new_ref[0, hh]
        k_new, v_new = new[:, :HEAD_DIM], new[:, HEAD_DIM:]
        scores, vts = [], []
        for ti in range(n_tiles):
            k, pp = divmod(ti, pages_per_block)
            blk = ids_ref[(i * N_HEADS + h) * MOBA_TOPK + k]
            near = jnp.logical_and(blk == last_blk, pp == pages_per_block - 1)
            tile = tiles[hh * n_tiles + ti][0, 0]
            vts.append(tile[HEAD_DIM:].astype(BF16))
            scores.append(_mm(q8b, tile[:HEAD_DIM].astype(BF16))
                          + jnp.where(near, tab_ref[hh, 0:1, :], tab_ref[hh, 1:2, :]))
        s = jnp.concatenate(scores, axis=1)
        s_self = jnp.sum(q8 * k_new, axis=1, keepdims=True) + tab_ref[hh, 2:3, 0:1]
        m = jnp.maximum(jnp.max(s, axis=1, keepdims=True), s_self)
        p = jnp.exp(s - m)
        p_self = jnp.exp(s_self - m)
        l = jnp.sum(p, axis=1, keepdims=True) + p_self
        o = p_self * v_new
        for ti in range(n_tiles):
            o = o + _nt(p[:, ti * PAGE_SIZE:(ti + 1) * PAGE_SIZE].astype(BF16), vts[ti])
        o_ref[0, hh] = o / jnp.maximum(l, 1e-30)


def _moba_sample(cache_t, page_table, ids, q, kv_new, tab_ms, past):
    db, n_pages = page_table.shape
    ppb = MOBA_BLOCK // PAGE_SIZE
    n_tiles = MOBA_TOPK * ppb

    hps = MOBA_SAMPLE_HEADS

    def tile_spec(hh, ti):
        k, pp = divmod(ti, ppb)

        def imap(i, hg, pt, sel):
            h = hg * hps + hh
            return (pt[i * n_pages + sel[(i * N_HEADS + h) * MOBA_TOPK + k] * ppb + pp], h, 0, 0)
        return pl.BlockSpec((1, 1, LANES, PAGE_SIZE), imap)

    q8 = jnp.pad(q.reshape(db, N_HEADS, 1, HEAD_DIM), ((0, 0), (0, 0), (0, 7), (0, 0)))
    grid_spec = pltpu.PrefetchScalarGridSpec(
        num_scalar_prefetch=2,
        grid=(db, N_HEADS // hps),
        in_specs=[pl.BlockSpec((1, hps, 8, HEAD_DIM), lambda i, h, pt, sel: (i, h, 0, 0)),
                  pl.BlockSpec((1, hps, 1, LANES), lambda i, h, pt, sel: (i, h, 0, 0))]
                 + [tile_spec(hh, ti) for hh in range(hps) for ti in range(n_tiles)]
                 + [pl.BlockSpec((hps, 3, LANES), lambda i, h, pt, sel: (h, 0, 0))],
        out_specs=pl.BlockSpec((1, hps, 8, HEAD_DIM), lambda i, h, pt, sel: (i, h, 0, 0)),
    )
    out = pl.pallas_call(
        functools.partial(_moba_sample_kernel, n_tiles=n_tiles, last_blk=past // MOBA_BLOCK - 1,
                          pages_per_block=ppb),
        grid_spec=grid_spec,
        out_shape=jax.ShapeDtypeStruct((db, N_HEADS, 8, HEAD_DIM), F32),
        compiler_params=_cparams("parallel", "arbitrary"),
        name="moba_sample",
    )(page_table.reshape(-1), ids.reshape(-1), q8, kv_new.reshape(db, N_HEADS, 1, LANES),
      *([cache_t] * (hps * n_tiles)), tab_ms)
    return out[:, :, 0, :].reshape(db, N_HEADS * HEAD_DIM)


def _nsa_select_kernel(q_ref, kc_ref, tc_ref, a_ref, oc_ref, ids_ref, *, n_valid, n_sel, cur):
    nrow = kc_ref.shape[2]
    lane = lax.broadcasted_iota(I32, (8, nrow), 1)
    psums = []
    for g in range(NSA_KV_HEADS):
        q8 = (q_ref[0, g] * ATTN_SCALE).astype(BF16)
        kc = kc_ref[0, g].astype(BF16)
        s = jnp.where(lane < n_valid, _nt(q8, kc) + tc_ref[g], -jnp.inf)
        p_c = _softmax_rows(s)
        oc_ref[0, g] = _mm(p_c.astype(BF16), kc)[:, HEAD_DIM:]
        psums.append(jnp.sum(p_c[0:NSA_GROUP], axis=0, keepdims=True))
    p_sum = _pad_rows(jnp.concatenate(psums, axis=0), 8)
    a = a_ref[...]
    imp = sum(_mm(part, a) for part in _split3(p_sum))
    blk = lax.broadcasted_iota(I32, imp.shape, 1)
    valid = blk <= min(cur, n_sel - 1)
    forced = (blk == 0) | (blk >= cur - (N_LOCAL_SEL - 1))
    score = jnp.where(valid, jnp.where(forced, jnp.inf, imp), -jnp.inf)
    ids_ref[0] = _topk_lanes(score, SEL_TOPK)


def _nsa_select(q4, kcvc, tab_cs, past):
    db = q4.shape[0]
    nrow = kcvc.shape[2]
    n_cmp_valid = (past - (CMP_BLOCK - 1)) // CMP_STRIDE + 1
    n_sel = max(-(-(past + 1) // SEL_BLOCK), SEL_TOPK)
    n_cmp = n_sel * SEL_BLOCK // CMP_STRIDE - CMP_BLOCK // CMP_STRIDE + 1
    cur = past // SEL_BLOCK
    assert n_sel <= 2 * LANES and cur >= SEL_TOPK - 1 and n_cmp <= nrow
    a = jnp.asarray(_imp_matrix(2 * LANES, nrow, n_cmp).T, BF16)
    tc = jnp.concatenate([tab_cs.reshape(NSA_KV_HEADS, NSA_GROUP, nrow)] * 2, axis=1)
    return pl.pallas_call(
        functools.partial(_nsa_select_kernel, n_valid=n_cmp_valid, n_sel=n_sel, cur=cur),
        grid=(db,),
        in_specs=[pl.BlockSpec((1, NSA_KV_HEADS, 8, LANES), lambda i: (i, 0, 0, 0)),
                  pl.BlockSpec((1, NSA_KV_HEADS, nrow, LANES), lambda i: (i, 0, 0, 0)),
                  pl.BlockSpec(tc.shape, lambda i: (0, 0, 0)),
                  pl.BlockSpec(a.shape, lambda i: (0, 0))],
        out_specs=[pl.BlockSpec((1, NSA_KV_HEADS, 8, HEAD_DIM), lambda i: (i, 0, 0, 0)),
                   pl.BlockSpec((1, 8, LANES), lambda i: (i, 0, 0))],
        out_shape=[jax.ShapeDtypeStruct((db, NSA_KV_HEADS, 8, HEAD_DIM), F32),
                   jax.ShapeDtypeStruct((db, 8, LANES), I32)],
        compiler_params=_cparams("parallel"),
        name="nsa_select",
    )(q4, kcvc, tc, a)


def _nsa_sample_kernel(pt_ref, ids_ref, q_ref, oc_ref, gate_ref, snew_ref, wnew_ref, st_ref, *rest,
                       n_cache_blk):
    tiles = rest[:SEL_TOPK]
    tms_ref, tws_ref, eye_ref, o_ref, st_out = rest[SEL_TOPK:]
    i, g = pl.program_id(0), pl.program_id(1)
    q8 = q_ref[0, 0] * ATTN_SCALE
    q8b = q8.astype(BF16)
    halves = PAGE_SIZE // SEL_BLOCK
    last_page = n_cache_blk // halves - 1
    half_of_lane = jnp.right_shift(lax.broadcasted_iota(I32, (8, PAGE_SIZE), 1), SEL_SHIFT)
    t0 = tms_ref[0, 2][:, 0:1]

    scores, vts = [], []
    for k in range(SEL_TOPK):
        blk = ids_ref[(i * NSA_KV_HEADS + g) * SEL_TOPK + k]
        in_cache = blk < n_cache_blk
        half = jnp.where(in_cache, lax.rem(blk, halves), -1)
        near = jnp.logical_and(in_cache, blk // halves == last_page)
        tile = tiles[k][0, 0]
        vts.append(tile[HEAD_DIM:].astype(BF16))
        s = _mm(q8b, tile[:HEAD_DIM].astype(BF16)) + jnp.where(near, tms_ref[0, 0], tms_ref[0, 1])
        scores.append(jnp.where(half_of_lane == half, s, -jnp.inf))
    s = jnp.concatenate(scores, axis=1)
    snew = snew_ref[0]
    s_self = jnp.sum(q8 * snew[:, :HEAD_DIM], axis=1, keepdims=True) + t0
    m = jnp.maximum(jnp.max(s, axis=1, keepdims=True), s_self)
    p = jnp.exp(s - m)
    p_self = jnp.exp(s_self - m)
    l = jnp.sum(p, axis=1, keepdims=True) + p_self
    o_s = p_self * snew[:, HEAD_DIM:]
    for k in range(SEL_TOPK):
        o_s = o_s + _nt(p[:, k * PAGE_SIZE:(k + 1) * PAGE_SIZE].astype(BF16), vts[k])
    o_s = o_s / jnp.maximum(l, 1e-30)

    ws = st_ref[0, 0]
    wl = lax.broadcasted_iota(I32, (8, WINDOW), 1)
    s_w = jnp.where(wl >= 1, _mm(q8b, ws[:HEAD_DIM].astype(BF16)) + tws_ref[0], -jnp.inf)
    wnew = wnew_ref[0]
    w_self = jnp.sum(q8 * wnew[:, :HEAD_DIM], axis=1, keepdims=True) + t0
    m = jnp.maximum(jnp.max(s_w, axis=1, keepdims=True), w_self)
    p = jnp.exp(s_w - m)
    p_self = jnp.exp(w_self - m)
    l = jnp.sum(p, axis=1, keepdims=True) + p_self
    o_w = (_nt(p.astype(BF16), ws[HEAD_DIM:].astype(BF16)) + p_self * wnew[:, HEAD_DIM:]) / jnp.maximum(l, 1e-30)

    gate = gate_ref[0, 0]
    o_ref[0, 0] = gate[:, 0:1] * oc_ref[0, 0] + gate[:, 1:2] * o_s + gate[:, 2:3] * o_w
    new_col = sum(_nt(eye_ref[...], _pad_rows(part, 8)) for part in _split3(wnew))[:, 0:1]
    wcol = lax.broadcasted_iota(I32, ws.shape, 1)
    st_out[0, 0] = jnp.where(wcol == WINDOW - 1, new_col, pltpu.roll(ws, WINDOW - 1, 1))


def _nsa_sample(cache_t, state_t, page_table, ids, q4, o_c, gates, sel_new, win_new, tabs, past):
    db, n_pages = page_table.shape
    gw = NSA_KV_HEADS * LANES
    halves = PAGE_SIZE // SEL_BLOCK
    n_cache_blk = past // SEL_BLOCK
    assert state_t.shape[-1] == WINDOW and past >= WINDOW and past % PAGE_SIZE == 0

    def tile_spec(k):
        def imap(i, g, pt, sel):
            blk = jnp.minimum(sel[(i * NSA_KV_HEADS + g) * SEL_TOPK + k], n_cache_blk - 1)
            return (pt[i * n_pages + blk // halves], g, 0, 0)
        return pl.BlockSpec((1, 1, LANES, PAGE_SIZE), imap)

    grp = lambda n: pl.BlockSpec((1, 1, 8, n), lambda i, g, pt, sel: (i, g, 0, 0))
    new = pl.BlockSpec((1, 1, LANES), lambda i, g, pt, sel: (i, 0, g))
    st = pl.BlockSpec((1, 1, LANES, WINDOW), lambda i, g, pt, sel: (i, g, 0, 0))
    pad8 = lambda x: jnp.concatenate([x, jnp.zeros_like(x)], axis=-2)
    tms = pad8(tabs["ms"].reshape(NSA_KV_HEADS, NSA_GROUP, 3, LANES).transpose(0, 2, 1, 3))
    tws = pad8(tabs["ws"].reshape(NSA_KV_HEADS, NSA_GROUP, WINDOW))
    eye = jnp.asarray(np.eye(LANES), BF16)
    grid_spec = pltpu.PrefetchScalarGridSpec(
        num_scalar_prefetch=2,
        grid=(db, NSA_KV_HEADS),
        in_specs=[grp(HEAD_DIM), grp(HEAD_DIM), grp(LANES), new, new, st] + [tile_spec(k) for k in range(SEL_TOPK)]
                 + [pl.BlockSpec((1, 3, 8, LANES), lambda i, g, pt, sel: (g, 0, 0, 0)),
                    pl.BlockSpec((1, 8, WINDOW), lambda i, g, pt, sel: (g, 0, 0)),
                    pl.BlockSpec(eye.shape, lambda i, g, pt, sel: (0, 0))],
        out_specs=[grp(HEAD_DIM), st],
    )
    o, st_new = pl.pallas_call(
        functools.partial(_nsa_sample_kernel, n_cache_blk=n_cache_blk),
        grid_spec=grid_spec,
        out_shape=[jax.ShapeDtypeStruct((db, NSA_KV_HEADS, 8, HEAD_DIM), F32),
                   jax.ShapeDtypeStruct(state_t.shape, F32)],
        compiler_params=_cparams("parallel", "arbitrary"),
        name="nsa_sample",
    )(page_table.reshape(-1), ids[:, :NSA_KV_HEADS, :SEL_TOPK].reshape(-1), q4, o_c, gates,
      sel_new.reshape(db, 1, gw), win_new.reshape(db, 1, gw), state_t,
      *([cache_t] * SEL_TOPK), tms, tws, eye)
    return o[:, :, :NSA_GROUP, :].reshape(db, N_HEADS * HEAD_DIM), st_new


def kernel(x_prompt, x_sample, cache_moba_kv, cache_nsa_cmp_kv, cache_nsa_sel_kv, state_nsa_win_kv, page_table,
           rel_bias, attn_norm, ffn_norm, moba_w_qkv, moba_q_norm, moba_k_norm, moba_w_o, nsa_w_in,
           nsa_gate_bias, nsa_q_norm, nsa_k_norm, nsa_cmp_pos, nsa_cmp_w1, nsa_cmp_b1, nsa_cmp_w2, nsa_w_o,
           ffn_w_up, ffn_w_down):
    b, s, d = x_prompt.shape
    db = x_sample.shape[0]
    assert x_sample.shape[1] == 1 and d == N_HEADS * HEAD_DIM
    past = page_table.shape[1] * PAGE_SIZE
    depth = attn_norm.shape[0]
    tabs = _bias_tables(rel_bias, s, past)
    hp = x_prompt.reshape(b * s, d)
    hs = x_sample.reshape(db, d)
    outs = {k: [] for k in ("moba_p", "moba_s", "cmp_p", "cmp_s", "sel_p", "sel_s", "win_p", "win_s")}
    kvshape = lambda n, g: (n, -1, g, 2, HEAD_DIM)
    for i in range(depth):
        j = i // 2
        if i % 2 == 0:
            qp_t, kvp_t, kvpb, kmean, kvpb_t = _proj_moba(hp, attn_norm[i], moba_w_qkv[j], moba_q_norm[j],
                                                          moba_k_norm[j], qdt=BF16, seq=s)
            qs, kvs, _ = _proj_moba(hs, attn_norm[i], moba_w_qkv[j], moba_q_norm[j], moba_k_norm[j], qdt=F32)
            op = _moba_prompt(qp_t, kvpb, kvpb_t, kmean, tabs["moba"], b, s)
            cache_t = _slot_minor(cache_moba_kv[j], N_HEADS)
            ids = _moba_select(cache_t, page_table, qs, past)
            osm = _moba_sample(cache_t, page_table, ids, qs, kvs, tabs["ms"], past)
            outs["moba_p"].append(_token_major(kvp_t, b, s, N_HEADS))
            outs["moba_s"].append(kvs.reshape(kvshape(db, N_HEADS)))
            w_o = moba_w_o[j]
        else:
            qp_t, cp, sp_t, wp_t, spb, wpb, gp_t, cp_t, spb_t, wpb_t = _proj_nsa(
                hp, attn_norm[i], nsa_w_in[j], nsa_gate_bias[j], nsa_q_norm[j], nsa_k_norm[j], qdt=BF16, seq=s)
            qs, cs, ss, ws, _, _, gs = _proj_nsa(hs, attn_norm[i], nsa_w_in[j], nsa_gate_bias[j],
                                                 nsa_q_norm[j], nsa_k_norm[j], qdt=F32)
            cw = _compress_weights(nsa_cmp_pos[j], nsa_cmp_w1[j], nsa_cmp_b1[j], nsa_cmp_w2[j], nsa_k_norm[j, 0])
            kcvc_p, kcvc_pt = _compress_prompt(cp, cw, b, s)
            op = _nsa_prompt(qp_t, gp_t, kcvc_p, kcvc_pt, spb, spb_t, wpb, wpb_t, tabs, b, s)
            kcvc_s = _compress_sample(cache_nsa_cmp_kv[j], page_table, cs, cw, past)
            pad8 = lambda x: jnp.concatenate([x, jnp.zeros_like(x)], axis=2)
            q4 = pad8(qs.reshape(db, NSA_KV_HEADS, NSA_GROUP, HEAD_DIM))
            q4w = jnp.pad(q4, ((0, 0), (0, 0), (0, 0), (0, LANES - HEAD_DIM)))
            g4 = pad8(gs.reshape(db, NSA_KV_HEADS, LANES)[:, :, :NSA_GROUP * N_BRANCH]
                      .reshape(db, NSA_KV_HEADS, NSA_GROUP, N_BRANCH))
            g4 = jnp.pad(g4, ((0, 0), (0, 0), (0, 0), (0, LANES - N_BRANCH)))
            o_c, sel_ids = _nsa_select(q4w, kcvc_s, tabs["cs"], past)
            osm, st_t = _nsa_sample(_slot_minor(cache_nsa_sel_kv[j], NSA_KV_HEADS),
                                    _slot_minor(state_nsa_win_kv[j], NSA_KV_HEADS), page_table, sel_ids, q4, o_c,
                                    g4, ss, ws, tabs, past)
            ws_new = st_t.reshape(db, NSA_KV_HEADS, 2, HEAD_DIM, WINDOW).transpose(0, 4, 1, 2, 3)
            gsh = (NSA_KV_HEADS, 2, HEAD_DIM)
            outs["cmp_p"].append(_token_major(cp_t, b, s, NSA_KV_HEADS))
            outs["cmp_s"].append(cs.reshape((db, 1) + gsh))
            outs["sel_p"].append(_token_major(sp_t, b, s, NSA_KV_HEADS))
            outs["sel_s"].append(ss.reshape((db, 1) + gsh))
            outs["win_p"].append(_token_major(wp_t, b, s, NSA_KV_HEADS)[:, s - min(WINDOW, s):])
            outs["win_s"].append(ws_new.reshape((db, WINDOW) + gsh))
            w_o = nsa_w_o[j]
        hp = _attn_out_mlp(hp, op, w_o, ffn_norm[i], ffn_w_up[i], ffn_w_down[i])
        hs = _attn_out_mlp(hs, osm, w_o, ffn_norm[i], ffn_w_up[i], ffn_w_down[i])
    stack = lambda k: jnp.stack(outs[k])
    return (hp.reshape(b, s, d), hs.reshape(db, 1, d), stack("moba_p"), stack("moba_s"), stack("cmp_p"),
            stack("cmp_s"), stack("sel_p"), stack("sel_s"), stack("win_p"), stack("win_s"))
```

```python
import functools
import math

import numpy as np
import jax
import jax.numpy as jnp
from jax import lax
from jax.experimental import pallas as pl
from jax.experimental.pallas import tpu as pltpu

F32 = jnp.float32
BF16 = jnp.bfloat16
I32 = jnp.int32

N_HEADS = 16
HEAD_DIM = 64
NORM_EPS = 1e-6
ATTN_SCALE = HEAD_DIM ** -0.5
REL_BUCKETS = 32
REL_MAX_DIST = 128
PAGE_SIZE = 128
MOBA_BLOCK = 256
MOBA_TOPK = 3
NSA_KV_HEADS = 4
NSA_GROUP = N_HEADS // NSA_KV_HEADS
CMP_BLOCK = 32
CMP_STRIDE = 16
CMP_HIDDEN = 2 * HEAD_DIM
SEL_BLOCK = 64
SEL_TOPK = 16
N_LOCAL_SEL = 2
WINDOW = 512
N_BRANCH = 3

LANES = 128
LANE_SHIFT = 7
SEL_SHIFT = 6
MASK_NEG = -(2.0 ** 100)
VMEM_LIMIT = 56 * 1024 * 1024
MOBA_UNROLL = 4
MOBA_HEADS = 4
NSA_UNROLL = 4
NSA_TQ = 256
NSA_TK = 256
WIN_KEYS = WINDOW + NSA_TQ
CMP_NEAR = 32


def _cparams(*sem):
    return pltpu.CompilerParams(dimension_semantics=sem, vmem_limit_bytes=VMEM_LIMIT)


def _nt(a, b):
    return lax.dot_general(a, b, (((1,), (1,)), ((), ())), preferred_element_type=F32)


def _mm(a, b):
    return jnp.dot(a, b, preferred_element_type=F32)


def _split2(x):
    hi = x.astype(BF16)
    lo = (x - hi.astype(F32)).astype(BF16)
    return hi, lo


def _split3(x):
    hi = x.astype(BF16)
    r = x - hi.astype(F32)
    mid = r.astype(BF16)
    lo = (r - mid.astype(F32)).astype(BF16)
    return hi, mid, lo


def _rms_rows(x, g):
    return x * lax.rsqrt(jnp.mean(x * x, axis=-1, keepdims=True) + NORM_EPS) * g


def _group_norm(h, gmat_ref, gain, k_lanes_only):
    hi, lo = _split2(h * h)
    gm = gmat_ref[...]
    ss = _mm(hi, gm) + _mm(lo, gm)
    y = h * lax.rsqrt(ss * (1.0 / HEAD_DIM) + NORM_EPS) * gain
    if k_lanes_only:
        lane = lax.broadcasted_iota(I32, h.shape, 1)
        y = jnp.where((lane & (LANES - 1)) < HEAD_DIM, y, h)
    return y


def _softmax_rows(s):
    m = jnp.max(s, axis=-1, keepdims=True)
    m = jnp.where(m == -jnp.inf, 0.0, m)
    e = jnp.exp(s - m)
    return e * (1.0 / jnp.maximum(jnp.sum(e, axis=-1, keepdims=True), 1e-30))


def _bucket_of_dist():
    n = np.arange(REL_MAX_DIST + 1)
    max_exact = REL_BUCKETS // 2
    nf = np.maximum(n, 1).astype(np.float32)
    large = max_exact + (np.log(nf / np.float32(max_exact)) / np.float32(math.log(REL_MAX_DIST / max_exact))
                         * np.float32(REL_BUCKETS - max_exact)).astype(np.int32)
    large = np.minimum(large, REL_BUCKETS - 1)
    return np.where(n < max_exact, n, large).astype(np.int32)


def _dist_tables(seq, past):
    c = lambda d: np.clip(d, 0, REL_MAX_DIST)
    ncp = seq // CMP_STRIDE
    i256 = np.arange(MOBA_BLOCK)[:, None]
    j256 = np.arange(MOBA_BLOCK)[None, :]
    q128 = np.arange(NSA_TQ)[None, :]
    t = {}
    t["moba"] = np.stack([c(j256 - i256), c(MOBA_BLOCK + j256 - i256),
                          np.full((MOBA_BLOCK, MOBA_BLOCK), REL_MAX_DIST)])
    t["sel"] = np.stack([c(NSA_TQ * v + q128 - np.arange(NSA_TK)[:, None]) for v in range(4)])
    t["win"] = c(q128 + WINDOW - np.arange(WIN_KEYS)[:, None])
    mrow = np.arange(ncp)[:, None]
    near = q128 - CMP_STRIDE * (mrow - CMP_NEAR // 2) - (CMP_BLOCK - 1)
    t["cmp"] = np.where(mrow < CMP_NEAR, c(near), REL_MAX_DIST)
    r128 = np.arange(LANES)
    t["ms"] = np.stack([c(LANES - r128), np.full(LANES, REL_MAX_DIST), np.zeros(LANES, np.int64)])
    nrow = _cmp_rows(past)
    t["cs"] = c(past - CMP_STRIDE * np.arange(nrow) - (CMP_BLOCK - 1))
    t["ws"] = c(WINDOW - np.arange(WINDOW))
    return t


def _cmp_rows(past):
    return -(-(past // CMP_STRIDE + 4) // 16) * 16


def _tab_kernel(idx_ref, rb_ref, o_ref):
    idx = idx_ref[...]
    b = lax.broadcasted_iota(I32, (REL_BUCKETS, idx.shape[1]), 0)
    oh = jnp.where(b == idx, 1.0, 0.0).astype(BF16)
    o_ref[...] = _mm(rb_ref[0], oh) + _mm(rb_ref[1], oh) + _mm(rb_ref[2], oh)


def _bias_tables(rel_bias, seq, past):
    pats = _dist_tables(seq, past)
    bucket = _bucket_of_dist()
    chunk = 16384
    flat, spans, off = [], {}, 0
    for name, d in pats.items():
        n = d.size
        pad = -n % LANES
        flat.append(bucket[d.reshape(-1)])
        flat.append(np.zeros(pad, np.int32))
        spans[name] = (off, n, d.shape)
        off += n + pad
    total = -(-off // chunk) * chunk
    flat.append(np.zeros(total - off, np.int32))
    idx = jnp.asarray(np.concatenate(flat).astype(np.int32)).reshape(1, total)
    rb3 = jnp.stack(_split3(rel_bias.T.astype(F32)))
    tab = pl.pallas_call(
        _tab_kernel,
        grid=(total // chunk,),
        in_specs=[pl.BlockSpec((1, chunk), lambda i: (0, i)),
                  pl.BlockSpec((3, N_HEADS, REL_BUCKETS), lambda i: (0, 0, 0))],
        out_specs=pl.BlockSpec((N_HEADS, chunk), lambda i: (0, i)),
        out_shape=jax.ShapeDtypeStruct((N_HEADS, total), F32),
        compiler_params=_cparams("parallel"),
        name="bias_tables",
    )(idx, rb3)
    return {name: tab[:, o:o + n].reshape((N_HEADS,) + shp) for name, (o, n, shp) in spans.items()}


def _group_mats():
    i = np.arange(2 * LANES)
    g64 = (i[:, None] // HEAD_DIM == i[None, :] // HEAD_DIM)
    g128 = (i[:, None] // LANES == i[None, :] // LANES) & ((i[:, None] % LANES) < HEAD_DIM)
    return jnp.asarray(g64, BF16), jnp.asarray(g128, BF16)


def _interleave_gain(g, n):
    return jnp.tile(jnp.concatenate([g.astype(F32), jnp.ones((HEAD_DIM,), F32)]), n).reshape(1, n * LANES)


def _store_rows_or_columns(ref, sl, x):
    if len(ref.shape) == 3:
        ref[0, sl, :] = x.T.astype(ref.dtype)
    else:
        ref[:, sl] = x.astype(ref.dtype)


def _token_major(x_t, b, s, n_kv):
    return x_t.reshape(b, n_kv, 2, HEAD_DIM, s).transpose(0, 4, 1, 2, 3)


def _proj_moba_kernel(x_ref, an_ref, wq_ref, wkv_ref, qg_ref, kg_ref, g64_ref, g128_ref,
                      q_ref, kv_ref, kvb_ref, *rest, n_mean):
    xn = _rms_rows(x_ref[...], an_ref[...]).astype(BF16)
    cw = 2 * LANES
    for c in range(wq_ref.shape[1] // cw):
        sl = slice(c * cw, (c + 1) * cw)
        h = _mm(xn, wq_ref[:, sl])
        q = _group_norm(h, g64_ref, qg_ref[:, sl], False)
        _store_rows_or_columns(q_ref, sl, q)
    for c in range(wkv_ref.shape[1] // cw):
        sl = slice(c * cw, (c + 1) * cw)
        h = _mm(xn, wkv_ref[:, sl])
        kv = _group_norm(h, g128_ref, kg_ref[:, sl], True)
        _store_rows_or_columns(kv_ref, sl, kv)
        kvb_ref[:, sl] = kv.astype(BF16)
        if n_mean:
            km_ref, kvtb_ref = rest
            _store_rows_or_columns(kvtb_ref, sl, kv)
            for r in range(n_mean):
                km_ref[0, r:r + 1, sl] = jnp.mean(kv[r * MOBA_BLOCK:(r + 1) * MOBA_BLOCK], axis=0, keepdims=True)


def _proj_moba(x, an, w_qkv, q_gain, k_gain, *, qdt, seq=None):
    m, d = x.shape
    tm = min(m, 512)
    with_mean = seq is not None
    g64, g128 = _group_mats()
    w3 = w_qkv.reshape(d, 3, N_HEADS, HEAD_DIM)
    wkv = jnp.stack([w3[:, 1], w3[:, 2]], axis=2).reshape(d, 2 * d).astype(BF16)
    wq = w3[:, 0].reshape(d, d).astype(BF16)
    qg = jnp.tile(q_gain.astype(F32), N_HEADS).reshape(1, d)
    kg = _interleave_gain(k_gain, N_HEADS)
    qn = wq.shape[1]
    n_mean = tm // MOBA_BLOCK if with_mean else 0
    const = lambda i: (0, 0)
    if seq is None:
        q_shape, q_spec = (m, qn), pl.BlockSpec((tm, qn), lambda i: (i, 0))
        kv_shape, kv_spec = (m, 2 * d), pl.BlockSpec((tm, 2 * d), lambda i: (i, 0))
    else:
        nt = seq // tm
        q_shape, q_spec = (m // seq, qn, seq), pl.BlockSpec((1, qn, tm), lambda i: (i // nt, 0, i % nt))
        kv_shape, kv_spec = (m // seq, 2 * d, seq), pl.BlockSpec((1, 2 * d, tm), lambda i: (i // nt, 0, i % nt))
    out_shape = [jax.ShapeDtypeStruct(q_shape, qdt), jax.ShapeDtypeStruct(kv_shape, F32),
                 jax.ShapeDtypeStruct((m, 2 * d), BF16)]
    out_specs = [q_spec, kv_spec, pl.BlockSpec((tm, 2 * d), lambda i: (i, 0))]
    if n_mean:
        out_shape += [jax.ShapeDtypeStruct((m // tm, n_mean, 2 * d), F32), jax.ShapeDtypeStruct(kv_shape, BF16)]
        out_specs += [pl.BlockSpec((1, n_mean, 2 * d), lambda i: (i, 0, 0)), kv_spec]
    return pl.pallas_call(
        functools.partial(_proj_moba_kernel, n_mean=n_mean),
        grid=(m // tm,),
        in_specs=[pl.BlockSpec((tm, d), lambda i: (i, 0)), pl.BlockSpec((1, d), const),
                  pl.BlockSpec((d, qn), const), pl.BlockSpec((d, 2 * d), const),
                  pl.BlockSpec((1, qn), const), pl.BlockSpec((1, 2 * d), const),
                  pl.BlockSpec(g64.shape, const), pl.BlockSpec(g128.shape, const)],
        out_specs=out_specs, out_shape=out_shape,
        compiler_params=_cparams("parallel"),
        name="proj_moba",
    )(x, an.reshape(1, d).astype(F32), wq, wkv, qg, kg, g64, g128)


def _proj_nsa_kernel(x_ref, an_ref, wq_ref, wkv_ref, wg_ref, gb_ref, qg_ref, kg_ref, g64_ref, g128_ref,
                     q_ref, cmp_ref, sel_ref, win_ref, selb_ref, winb_ref, gate_ref, *rest):
    cmp_t_ref, selb_t_ref, winb_t_ref = rest if rest else (None, None, None)
    xn = _rms_rows(x_ref[...], an_ref[...]).astype(BF16)
    cw = 2 * LANES
    for c in range(wq_ref.shape[1] // cw):
        sl = slice(c * cw, (c + 1) * cw)
        h = _mm(xn, wq_ref[:, sl])
        _store_rows_or_columns(q_ref, sl, _group_norm(h, g64_ref, qg_ref[:, sl], False))
    per_branch = NSA_KV_HEADS * LANES // cw
    for c in range(wkv_ref.shape[1] // cw):
        sl = slice(c * cw, (c + 1) * cw)
        br, cc = divmod(c, per_branch)
        osl = slice(cc * cw, (cc + 1) * cw)
        h = _mm(xn, wkv_ref[:, sl])
        if br == 0:
            cmp_ref[:, osl] = h
            if cmp_t_ref is not None:
                _store_rows_or_columns(cmp_t_ref, osl, h)
        else:
            kv = _group_norm(h, g128_ref, kg_ref[:, sl], True)
            o32, o16, o16t = (sel_ref, selb_ref, selb_t_ref) if br == 1 else (win_ref, winb_ref, winb_t_ref)
            _store_rows_or_columns(o32, osl, kv)
            o16[:, osl] = kv.astype(BF16)
            if o16t is not None:
                _store_rows_or_columns(o16t, osl, kv)
    hg = _mm(xn, wg_ref[...]) + gb_ref[...]
    _store_rows_or_columns(gate_ref, slice(0, hg.shape[1]), 1.0 / (1.0 + jnp.exp(-hg)))


def _proj_nsa(x, an, w_in, gate_bias, q_gain, k_gain, *, qdt, seq=None):
    m, d = x.shape
    tm = min(m, 512)
    g64, g128 = _group_mats()
    kvd = N_BRANCH * NSA_KV_HEADS * LANES
    gw = NSA_KV_HEADS * LANES
    ng = NSA_GROUP * N_BRANCH
    qg = jnp.tile(q_gain.astype(F32), N_HEADS).reshape(1, d)
    wq = w_in[:, :d].astype(BF16)
    wkv = w_in[:, d:d + kvd].astype(BF16)
    wg = jnp.zeros((d, NSA_KV_HEADS, LANES), F32).at[:, :, :ng].set(
        w_in[:, d + kvd:].reshape(d, NSA_KV_HEADS, ng)).reshape(d, gw).astype(BF16)
    gb = jnp.zeros((NSA_KV_HEADS, LANES), F32).at[:, :ng].set(
        gate_bias.astype(F32).reshape(NSA_KV_HEADS, ng)).reshape(1, gw)
    kg = jnp.concatenate([jnp.ones((1, gw), F32), _interleave_gain(k_gain[1], NSA_KV_HEADS),
                          _interleave_gain(k_gain[2], NSA_KV_HEADS)], axis=1)
    qn = wq.shape[1]
    const = lambda i: (0, 0)
    row = lambda n: pl.BlockSpec((tm, n), lambda i: (i, 0))
    rows32 = jax.ShapeDtypeStruct((m, gw), F32)
    if seq is None:
        q_shape, q_spec = jax.ShapeDtypeStruct((m, qn), qdt), row(qn)
        kv_shape, kv_spec, gate_shape, gate_spec, extra_shape, extra_spec = rows32, row(gw), rows32, row(gw), [], []
    else:
        nt = seq // tm
        cols = lambda n: pl.BlockSpec((1, n, tm), lambda i: (i // nt, 0, i % nt))
        q_shape, q_spec = jax.ShapeDtypeStruct((m // seq, qn, seq), qdt), cols(qn)
        kv_shape, kv_spec = jax.ShapeDtypeStruct((m // seq, gw, seq), F32), cols(gw)
        gate_shape, gate_spec = kv_shape, kv_spec
        extra_shape = [kv_shape] + [jax.ShapeDtypeStruct((m // seq, gw, seq), BF16)] * 2
        extra_spec = [kv_spec] * 3
    return pl.pallas_call(
        _proj_nsa_kernel,
        grid=(m // tm,),
        in_specs=[row(d), pl.BlockSpec((1, d), const), pl.BlockSpec((d, qn), const),
                  pl.BlockSpec((d, kvd), const), pl.BlockSpec((d, gw), const), pl.BlockSpec((1, gw), const),
                  pl.BlockSpec((1, qn), const), pl.BlockSpec((1, kvd), const),
                  pl.BlockSpec(g64.shape, const), pl.BlockSpec(g128.shape, const)],
        out_specs=[q_spec, row(gw), kv_spec, kv_spec, row(gw), row(gw), gate_spec] + extra_spec,
        out_shape=[q_shape, rows32, kv_shape, kv_shape]
                  + [jax.ShapeDtypeStruct((m, gw), BF16)] * 2 + [gate_shape] + extra_shape,
        compiler_params=_cparams("parallel"),
        name="proj_nsa",
    )(x, an.reshape(1, d).astype(F32), wq, wkv, wg, gb, qg, kg, g64, g128)


def _mlp_kernel(h_ref, o_ref, wo_ref, g_ref, wup_ref, wdn_ref, out_ref, xn_s):
    @pl.when(pl.program_id(1) == 0)
    def _():
        h1 = h_ref[...] + _mm(o_ref[...], wo_ref[...])
        out_ref[...] = h1
        xn_s[...] = _rms_rows(h1, g_ref[...]).astype(BF16)

    u = _mm(xn_s[...], wup_ref[...])
    u = jnp.square(jnp.maximum(u, 0.0)).astype(BF16)
    out_ref[...] += _mm(u, wdn_ref[...])


def _attn_out_mlp(h, o, w_o, fn, w_up, w_down):
    m, d = h.shape
    dff = w_up.shape[1]
    tm = min(m, 1024)
    tf = 1024
    return pl.pallas_call(
        _mlp_kernel,
        grid=(m // tm, dff // tf),
        in_specs=[pl.BlockSpec((tm, d), lambda i, f: (i, 0)), pl.BlockSpec((tm, d), lambda i, f: (i, 0)),
                  pl.BlockSpec((d, d), lambda i, f: (0, 0)), pl.BlockSpec((1, d), lambda i, f: (0, 0)),
                  pl.BlockSpec((d, tf), lambda i, f: (0, f)), pl.BlockSpec((tf, d), lambda i, f: (f, 0))],
        out_specs=pl.BlockSpec((tm, d), lambda i, f: (i, 0)),
        out_shape=jax.ShapeDtypeStruct((m, d), F32),
        scratch_shapes=[pltpu.VMEM((tm, d), BF16)],
        compiler_params=_cparams("parallel", "arbitrary"),
        name="attn_out_mlp",
    )(h, o.astype(BF16), w_o.astype(BF16), fn.reshape(1, d).astype(F32), w_up.astype(BF16), w_down.astype(BF16))


def _topk_rows(x, k, on_pick):
    n = x.shape[0]
    row = lax.broadcasted_iota(I32, x.shape, 0)
    for _ in range(k):
        m = jnp.max(x, axis=0, keepdims=True)
        idx = jnp.min(jnp.where(x == m, row, n), axis=0, keepdims=True)
        hit = row == idx
        on_pick(hit, m)
        x = jnp.where(hit, -jnp.inf, x)


SUM_ROWS = 16


def _with_sum_rows(v_t):
    return jnp.concatenate([v_t, jnp.ones((SUM_ROWS, v_t.shape[1]), v_t.dtype)], axis=0)


def _online_softmax_step(m_ref, acc_ref, tiles):
    m_old = m_ref[0]
    mx = jnp.max(tiles[0][0], axis=0, keepdims=True)
    for s, _ in tiles[1:]:
        mx = jnp.maximum(mx, jnp.max(s, axis=0, keepdims=True))
    mn = jnp.maximum(m_old, mx)
    acc = jnp.exp(m_old - mn) * acc_ref[0]
    for s, v_aug in tiles:
        acc = acc + _mm(v_aug, jnp.exp(s - mn).astype(BF16))
    m_ref[0] = mn
    acc_ref[0] = acc


def _softmax_finish(acc):
    return acc[:HEAD_DIM] * (1.0 / jnp.maximum(acc[HEAD_DIM:HEAD_DIM + 1], 1e-30))


def _unrolled_range(n, unroll, visit):
    rem = lax.rem(n, unroll)
    done = 0
    size = 1
    while size < unroll:
        take = jnp.bitwise_and(rem, size)

        @pl.when(take != 0)
        def _(done=done, size=size):
            visit([done + k for k in range(size)])

        done = done + take
        size *= 2

    def body(i, carry):
        first = rem + unroll * i
        visit([first + k for k in range(unroll)])
        return carry

    lax.fori_loop(0, n // unroll, body, 0)


def _softmax_state_init(m_ref, acc_ref):
    m_ref[...] = jnp.full(m_ref.shape, -jnp.inf, F32)
    acc_ref[...] = jnp.zeros(acc_ref.shape, F32)


def _moba_prompt_kernel(qt_ref, kv_ref, kvt_ref, km_ref, tb_ref, o_ref, *state, nb):
    cur = pl.program_id(2)
    tq = MOBA_BLOCK
    key_lane = lax.broadcasted_iota(I32, (tq, LANES), 1)
    rown = lax.broadcasted_iota(I32, (nb, tq), 0)
    heads = [state[2 * hh:2 * hh + 2] for hh in range(MOBA_HEADS)]
    for st in heads:
        _softmax_state_init(*st)
    qzs, qaugs = [], []
    for hh in range(MOBA_HEADS):
        qt = qt_ref[0, hh * HEAD_DIM:(hh + 1) * HEAD_DIM, :] * jnp.asarray(ATTN_SCALE, BF16)
        qz = jnp.concatenate([qt, jnp.zeros_like(qt)], axis=0)
        km_hi, km_lo = _split2(km_ref[0, :, hh * LANES:(hh + 1) * LANES])
        gate = _mm(km_hi, qz) + _mm(km_lo, qz)
        state = [jnp.full((nb, tq), MASK_NEG, F32)]

        def pick(hit, m, state=state):
            state[0] = jnp.where(hit, jnp.where(m > -jnp.inf, 0.0, state[0]), state[0])

        _topk_rows(jnp.where(rown < cur, gate, -jnp.inf), MOBA_TOPK, pick)
        negt = jnp.concatenate([state[0], jnp.zeros((LANES - nb, tq), F32)], axis=0).astype(BF16)
        qzs.append(qz)
        qaugs.append(jnp.concatenate([qz, negt], axis=0))

    def k_rows(hh, n):
        return kv_ref[0, pl.ds(pl.multiple_of(n * tq, tq), tq), hh * LANES:(hh + 1) * LANES]

    def v_cols(hh, n):
        return _with_sum_rows(
            kvt_ref[0, hh * LANES + HEAD_DIM:(hh + 1) * LANES, pl.ds(pl.multiple_of(n * tq, tq), tq)])

    for hh in range(MOBA_HEADS):
        _online_softmax_step(*heads[hh], [(_mm(k_rows(hh, cur), qzs[hh]) + tb_ref[hh, 0], v_cols(hh, cur))])

    def past_blocks(blocks):
        for hh in range(MOBA_HEADS):
            tiles = []
            for n in blocks:
                onehot = jnp.where(key_lane == n, 1.0, 0.0).astype(BF16)
                s = _mm(jnp.concatenate([k_rows(hh, n), onehot], axis=1), qaugs[hh])
                tiles.append((s + tb_ref[hh, jnp.where(cur - n == 1, 1, 2)], v_cols(hh, n)))
            _online_softmax_step(*heads[hh], tiles)

    _unrolled_range(cur, MOBA_UNROLL, past_blocks)
    o_t = jnp.concatenate([_softmax_finish(acc[0]) for _, acc in heads], axis=0)
    o_ref[...] = o_t.T.astype(o_ref.dtype)


def _moba_prompt(qt, kvb, kvt, kmean, tb, b, s):
    d = qt.shape[1]
    nb = s // MOBA_BLOCK
    assert s % MOBA_BLOCK == 0 and MOBA_TOPK <= nb <= LANES
    nq = s // MOBA_BLOCK
    r = np.arange(MOBA_BLOCK)
    causal = np.zeros((3, MOBA_BLOCK, MOBA_BLOCK), np.float32)
    causal[0] = np.where(r[:, None] <= r[None, :], 0.0, -np.inf)
    tb = tb + jnp.asarray(causal)
    hps = MOBA_HEADS
    return pl.pallas_call(
        functools.partial(_moba_prompt_kernel, nb=nb),
        grid=(b, N_HEADS // hps, nq),
        in_specs=[pl.BlockSpec((1, hps * HEAD_DIM, MOBA_BLOCK), lambda i, h, t: (i, h, t)),
                  pl.BlockSpec((1, s, hps * LANES), lambda i, h, t: (i, 0, h)),
                  pl.BlockSpec((1, hps * LANES, s), lambda i, h, t: (i, h, 0)),
                  pl.BlockSpec((1, nb, hps * LANES), lambda i, h, t: (i, 0, h)),
                  pl.BlockSpec((hps, 3, MOBA_BLOCK, MOBA_BLOCK), lambda i, h, t: (h, 0, 0, 0))],
        out_specs=pl.BlockSpec((MOBA_BLOCK, hps * HEAD_DIM), lambda i, h, t: (i * nq + t, h)),
        out_shape=jax.ShapeDtypeStruct((b * s, d), BF16),
        scratch_shapes=[pltpu.VMEM((1, 1, MOBA_BLOCK), F32),
                        pltpu.VMEM((1, HEAD_DIM + SUM_ROWS, MOBA_BLOCK), F32)] * hps,
        compiler_params=_cparams("parallel", "parallel", "arbitrary"),
        name="moba_prompt",
    )(qt, kvb.reshape(b, s, 2 * d), kvt, kmean.reshape(b, nb, 2 * d), tb)


def _gelu_tanh(x):
    return 0.5 * x * (1.0 + jnp.tanh(math.sqrt(2.0 / math.pi) * (x + 0.044715 * (x * x * x))))


def _compress_weights(cmp_pos, w1, b1, w2, k_gain0):
    half = CMP_BLOCK // 2
    z = jnp.zeros((half, HEAD_DIM, CMP_HIDDEN), F32)

    def first_layer(lo):
        wk, wv = w1[0, lo:lo + half], w1[1, lo:lo + half]
        top = jnp.concatenate([wk, z], axis=2)
        bot = jnp.concatenate([z, wv], axis=2)
        return jnp.concatenate([top, bot], axis=1).reshape(half * LANES, 2 * CMP_HIDDEN).astype(BF16)

    zz = jnp.zeros((CMP_HIDDEN, HEAD_DIM), F32)
    w2bd = jnp.concatenate([jnp.concatenate([w2[0], zz], axis=1),
                            jnp.concatenate([zz, w2[1]], axis=1)], axis=0).astype(BF16)
    pos_a = cmp_pos[:half].reshape(half, LANES).astype(F32)
    pos_b = cmp_pos[half:].reshape(half, LANES).astype(F32)
    return (pos_a, pos_b, first_layer(0), first_layer(half), b1.reshape(1, 2 * CMP_HIDDEN).astype(F32), w2bd,
            _interleave_gain(k_gain0, 1))


def _compress_tail(xa_s, xb_s, wa_ref, wb_ref, b1_ref, w2_ref, kg_ref):
    ha = _mm(xa_s[...], wa_ref[...])
    hb = _mm(xb_s[...], wb_ref[...])
    rows = ha.shape[0]
    h = ha + pltpu.roll(hb, rows - 1, 0) + b1_ref[...]
    out = _mm(_gelu_tanh(h).astype(BF16), w2_ref[...])
    lane = lax.broadcasted_iota(I32, out.shape, 1)
    is_k = lane < HEAD_DIM
    ss = jnp.sum(jnp.where(is_k, out * out, 0.0), axis=1, keepdims=True)
    kn = out * lax.rsqrt(ss * (1.0 / HEAD_DIM) + NORM_EPS) * kg_ref[...]
    return jnp.where(is_k, kn, out)


def _compress_prompt_kernel(*refs, nseg):
    x_refs = refs[:NSA_KV_HEADS]
    pa_ref, pb_ref, wa_ref, wb_ref, b1_ref, w2_ref, kg_ref, o_ref, ot_ref, xa_s, xb_s = refs[NSA_KV_HEADS:]
    half = CMP_BLOCK // 2
    for g in range(NSA_KV_HEADS):
        for p in range(half):
            v = x_refs[g][0, pl.ds(p, nseg, stride=CMP_STRIDE), :]
            xa_s[g * nseg:(g + 1) * nseg, p * LANES:(p + 1) * LANES] = (v + pa_ref[p:p + 1, :]).astype(BF16)
            xb_s[g * nseg:(g + 1) * nseg, p * LANES:(p + 1) * LANES] = (v + pb_ref[p:p + 1, :]).astype(BF16)
    res = _compress_tail(xa_s, xb_s, wa_ref, wb_ref, b1_ref, w2_ref, kg_ref)
    for g in range(NSA_KV_HEADS):
        rows = res[g * nseg:(g + 1) * nseg]
        o_ref[0, g] = rows.astype(BF16)
        ot_ref[0, g] = rows.T.astype(BF16)


def _compress_prompt(kv_cmp, cw, b, s):
    nseg = s // CMP_STRIDE
    gw = NSA_KV_HEADS * LANES
    kdim = (CMP_BLOCK // 2) * LANES
    const = lambda i: (0, 0)
    return pl.pallas_call(
        functools.partial(_compress_prompt_kernel, nseg=nseg),
        grid=(b,),
        in_specs=[pl.BlockSpec((1, s, LANES), functools.partial(lambda i, g: (i, 0, g), g=g))
                  for g in range(NSA_KV_HEADS)] + [pl.BlockSpec(w.shape, const) for w in cw],
        out_specs=[pl.BlockSpec((1, NSA_KV_HEADS, nseg, LANES), lambda i: (i, 0, 0, 0)),
                   pl.BlockSpec((1, NSA_KV_HEADS, LANES, nseg), lambda i: (i, 0, 0, 0))],
        out_shape=[jax.ShapeDtypeStruct((b, NSA_KV_HEADS, nseg, LANES), BF16),
                   jax.ShapeDtypeStruct((b, NSA_KV_HEADS, LANES, nseg), BF16)],
        scratch_shapes=[pltpu.VMEM((NSA_KV_HEADS * nseg, kdim), BF16)] * 2,
        compiler_params=_cparams("parallel"),
        name="compress_prompt",
    )(*([kv_cmp.reshape(b, s, gw)] * NSA_KV_HEADS), *cw)


CMP_PAGES = 4


def _compress_sample_kernel(pt_ref, *refs, nseg, nrow):
    x_refs = refs[:CMP_PAGES]
    (new_ref, pa_ref, pb_ref, wa_ref, wb_ref, b1_ref, w2_ref, kg_ref, o_ref,
     xa_s, xb_s, xt_s) = refs[CMP_PAGES:]
    j = pl.program_id(1)
    for pg in range(CMP_PAGES):
        for g in range(NSA_KV_HEADS):
            xt_s[pg, g] = x_refs[pg][0, g].T
    half = CMP_BLOCK // 2
    per_page = PAGE_SIZE // CMP_STRIDE
    per_step = CMP_PAGES * per_page
    tail = nrow - nseg

    @pl.when(j == 0)
    def _():
        first = lax.broadcasted_iota(I32, (tail, LANES), 0) == 0
        for g in range(NSA_KV_HEADS):
            rows = slice(g * nrow + nseg, (g + 1) * nrow)
            for p in range(half):
                v = jnp.zeros((tail, LANES), F32)
                if p == 0:
                    v = jnp.where(first, new_ref[0, :, g * LANES:(g + 1) * LANES], 0.0)
                xa_s[rows, p * LANES:(p + 1) * LANES] = (v + pa_ref[p:p + 1, :]).astype(BF16)
                xb_s[rows, p * LANES:(p + 1) * LANES] = (v + pb_ref[p:p + 1, :]).astype(BF16)

    for g in range(NSA_KV_HEADS):
        r0 = pl.multiple_of(g * nrow + j * per_step, 16)
        for p in range(half):
            v = jnp.concatenate([xt_s[pg, g, pl.ds(p, per_page, stride=CMP_STRIDE), :]
                                 for pg in range(CMP_PAGES)], axis=0)
            xa_s[pl.ds(r0, per_step), p * LANES:(p + 1) * LANES] = (v + pa_ref[p:p + 1, :]).astype(BF16)
            xb_s[pl.ds(r0, per_step), p * LANES:(p + 1) * LANES] = (v + pb_ref[p:p + 1, :]).astype(BF16)

    @pl.when(j == pl.num_programs(1) - 1)
    def _():
        res = _compress_tail(xa_s, xb_s, wa_ref, wb_ref, b1_ref, w2_ref, kg_ref)
        for g in range(NSA_KV_HEADS):
            o_ref[0, g] = res[g * nrow:(g + 1) * nrow]


def _compress_sample(cache, page_table, new_rows, cw, past):
    db, n_pages = page_table.shape
    assert n_pages % CMP_PAGES == 0 and (CMP_PAGES * PAGE_SIZE // CMP_STRIDE) % 16 == 0
    gw = NSA_KV_HEADS * LANES
    nseg = past // CMP_STRIDE
    nrow = _cmp_rows(past)
    kdim = (CMP_BLOCK // 2) * LANES
    const = lambda i, j, pt: (0, 0)
    grid_spec = pltpu.PrefetchScalarGridSpec(
        num_scalar_prefetch=1,
        grid=(db, n_pages // CMP_PAGES),
        in_specs=[pl.BlockSpec((1, NSA_KV_HEADS, LANES, PAGE_SIZE),
                               functools.partial(lambda i, j, pt, pg: (pt[i * n_pages + CMP_PAGES * j + pg], 0, 0, 0),
                                                 pg=pg))
                  for pg in range(CMP_PAGES)]
                 + [pl.BlockSpec((1, 1, gw), lambda i, j, pt: (i, 0, 0))]
                 + [pl.BlockSpec(w.shape, const) for w in cw],
        out_specs=pl.BlockSpec((1, NSA_KV_HEADS, nrow, LANES), lambda i, j, pt: (i, 0, 0, 0)),
        scratch_shapes=[pltpu.VMEM((NSA_KV_HEADS * nrow, kdim), BF16)] * 2
                       + [pltpu.VMEM((CMP_PAGES, NSA_KV_HEADS, PAGE_SIZE, LANES), F32)],
    )
    cview = _slot_minor(cache, NSA_KV_HEADS)
    return pl.pallas_call(
        functools.partial(_compress_sample_kernel, nseg=nseg, nrow=nrow),
        grid_spec=grid_spec,
        out_shape=jax.ShapeDtypeStruct((db, NSA_KV_HEADS, nrow, LANES), F32),
        compiler_params=_cparams("parallel", "arbitrary"),
        name="compress_sample",
    )(page_table.reshape(-1), *([cview] * CMP_PAGES), new_rows.reshape(db, 1, gw), *cw)


def _imp_matrix(n_sel, n_rows, n_cmp):
    ratio = SEL_BLOCK // CMP_STRIDE
    lead = CMP_BLOCK // CMP_STRIDE - 1
    j = np.arange(n_sel)[:, None]
    n = np.arange(n_rows)[None, :]
    return ((n >= ratio * j - lead) & (n <= ratio * j + ratio - 1) & (n < n_cmp))


def _nsa_prompt_kernel(qt_ref, gt_ref, kc_ref, kct_ref, ks_ref, kst_ref, kw_ref, kwt_ref, tsel_ref, tw_ref,
                       tc_ref, at_ref, o_ref, m_s, acc_s, *, n_sel, ncp):
    t = pl.program_id(2)
    tq, tk, grp = NSA_TQ, NSA_TK, NSA_GROUP
    cols = grp * tq
    p0 = t * tq
    scale = jnp.asarray(ATTN_SCALE, BF16)
    qt = jnp.concatenate([qt_ref[0, p * HEAD_DIM:(p + 1) * HEAD_DIM, :] * scale for p in range(grp)], axis=1)
    qz = jnp.concatenate([qt, jnp.zeros_like(qt)], axis=0)
    qoff = jnp.bitwise_and(lax.broadcasted_iota(I32, (1, cols), 1), tq - 1)

    shift = lax.rem(t * (tq // CMP_STRIDE) - CMP_NEAR // 2 + ncp, ncp)
    c_end = lax.broadcasted_iota(I32, (ncp, cols), 0) * CMP_STRIDE + (CMP_BLOCK - 1)
    s_c = jnp.where(p0 + qoff >= c_end, _mm(kc_ref[0, 0], qz) + pltpu.roll(tc_ref[0], shift, 0), -jnp.inf)
    m_c = jnp.max(s_c, axis=0, keepdims=True)
    e_c = jnp.exp(s_c - jnp.where(m_c == -jnp.inf, 0.0, m_c))
    p_c = e_c * (1.0 / jnp.maximum(jnp.sum(e_c, axis=0, keepdims=True), 1e-30))
    o_c = _mm(kct_ref[0, 0, HEAD_DIM:, :], p_c.astype(BF16))
    p_sum = p_c[:, 0:tq]
    for p in range(1, grp):
        p_sum = p_sum + p_c[:, p * tq:(p + 1) * tq]
    at = at_ref[...]
    imp = sum(_mm(at, part) for part in _split3(p_sum))

    blk = lax.broadcasted_iota(I32, (n_sel, tq), 0)
    cur = jnp.right_shift(p0 + lax.broadcasted_iota(I32, (n_sel, tq), 1), SEL_SHIFT)
    valid = blk <= cur
    forced = (blk == 0) | (blk >= cur - (N_LOCAL_SEL - 1))
    score = jnp.where(valid, jnp.where(forced, jnp.inf, imp), -jnp.inf)
    state = [jnp.full((n_sel, tq), MASK_NEG, F32)]

    def pick(hit, m):
        state[0] = jnp.where(hit, jnp.where(m > -jnp.inf, 0.0, state[0]), state[0])

    _topk_rows(score, SEL_TOPK, pick)
    negt = state[0]
    if n_sel < LANES:
        negt = jnp.concatenate([negt, jnp.zeros((LANES - n_sel, tq), F32)], axis=0)
    negt = negt.astype(BF16)
    qaug = jnp.concatenate([qz, jnp.concatenate([negt] * grp, axis=1)], axis=0)

    kj = lax.broadcasted_iota(I32, (tk, LANES), 0)
    kl = lax.broadcasted_iota(I32, (tk, LANES), 1)

    def sel_tile(kt, variant):
        at_kt = pl.ds(pl.multiple_of(kt * tk, tk), tk)
        onehot = jnp.where(kl == kt * (tk // SEL_BLOCK) + jnp.right_shift(kj, SEL_SHIFT), 1.0, 0.0).astype(BF16)
        s = _mm(jnp.concatenate([ks_ref[0, at_kt, :], onehot], axis=1), qaug) + tsel_ref[0, variant]
        return s, _with_sum_rows(kst_ref[0, HEAD_DIM:, at_kt])

    _softmax_state_init(m_s, acc_s)
    kt_d = t // (tk // tq)
    _online_softmax_step(m_s, acc_s, [sel_tile(kt_d, lax.rem(t, tk // tq))])

    def past_tiles(kts):
        _online_softmax_step(m_s, acc_s, [sel_tile(kt, jnp.minimum((p0 - kt * tk) // tq, 3)) for kt in kts])

    _unrolled_range(kt_d, NSA_UNROLL, past_tiles)
    o_s = _softmax_finish(acc_s[0])

    at_w = pl.ds(pl.multiple_of(p0, tq), WIN_KEYS)
    wj = lax.broadcasted_iota(I32, (WIN_KEYS, cols), 0)
    s_w = jnp.where(p0 + wj >= WINDOW, _mm(kw_ref[0, at_w, :], qz) + tw_ref[0], -jnp.inf)
    m_w = jnp.max(s_w, axis=0, keepdims=True)
    e_w = jnp.exp(s_w - jnp.where(m_w == -jnp.inf, 0.0, m_w))
    o_w = _softmax_finish(_mm(_with_sum_rows(kwt_ref[0, HEAD_DIM:, at_w]), e_w.astype(BF16)))

    gt = gt_ref[0]
    per_head = []
    for p in range(grp):
        c = slice(p * tq, (p + 1) * tq)
        g = [gt[N_BRANCH * p + k:N_BRANCH * p + k + 1, :] for k in range(N_BRANCH)]
        per_head.append(g[0] * o_c[:, c] + g[1] * o_s[:, c] + g[2] * o_w[:, c])
    o_ref[...] = jnp.concatenate(per_head, axis=0).T.astype(o_ref.dtype)


def _heads_on_lanes(tab, lead):
    n = len(lead)
    x = tab.reshape((NSA_KV_HEADS, NSA_GROUP) + tab.shape[1:])
    perm = (0,) + tuple(range(2, 3 + n)) + (1, 3 + n)
    x = x.transpose(perm)
    return x.reshape(x.shape[:2 + n] + (NSA_GROUP * tab.shape[-1],))


def _nsa_prompt(qt, gt, kcvc, kcvc_t, selb, selb_t, winb, winb_t, tabs, b, s):
    d = qt.shape[1]
    gw = NSA_KV_HEADS * LANES
    ncp = s // CMP_STRIDE
    n_sel = s // SEL_BLOCK
    assert s % NSA_TK == 0 and SEL_TOPK <= n_sel <= LANES and ncp >= CMP_NEAR
    nq = s // NSA_TQ
    at = jnp.asarray(_imp_matrix(n_sel, ncp, ncp - 1), BF16)
    winp = jnp.pad(winb.reshape(b, s, gw), ((0, 0), (WINDOW, 0), (0, 0)))
    winp_t = jnp.pad(winb_t, ((0, 0), (0, 0), (WINDOW, 0)))
    gq = NSA_GROUP
    cols = gq * NSA_TQ
    kr = np.arange(NSA_TK)[:, None]
    wr = np.arange(WIN_KEYS)[:, None]
    qc = np.arange(cols)[None, :] % NSA_TQ
    sel_mask = np.zeros((4, NSA_TK, cols), np.float32)
    for v in range(NSA_TK // NSA_TQ):
        sel_mask[v] = np.where(kr <= qc + v * NSA_TQ, 0.0, -np.inf)
    win_mask = np.where((wr > qc) & (wr <= qc + WINDOW), 0.0, -np.inf).astype(np.float32)
    tsel = _heads_on_lanes(tabs["sel"], (4,)) + jnp.asarray(sel_mask)
    tw = _heads_on_lanes(tabs["win"], ()) + jnp.asarray(win_mask)
    tc = _heads_on_lanes(tabs["cmp"], ())
    return pl.pallas_call(
        functools.partial(_nsa_prompt_kernel, n_sel=n_sel, ncp=ncp),
        grid=(b, NSA_KV_HEADS, nq),
        in_specs=[pl.BlockSpec((1, gq * HEAD_DIM, NSA_TQ), lambda i, g, t: (i, g, t)),
                  pl.BlockSpec((1, LANES, NSA_TQ), lambda i, g, t: (i, g, t)),
                  pl.BlockSpec((1, 1, ncp, LANES), lambda i, g, t: (i, g, 0, 0)),
                  pl.BlockSpec((1, 1, LANES, ncp), lambda i, g, t: (i, g, 0, 0)),
                  pl.BlockSpec((1, s, LANES), lambda i, g, t: (i, 0, g)),
                  pl.BlockSpec((1, LANES, s), lambda i, g, t: (i, g, 0)),
                  pl.BlockSpec((1, s + WINDOW, LANES), lambda i, g, t: (i, 0, g)),
                  pl.BlockSpec((1, LANES, s + WINDOW), lambda i, g, t: (i, g, 0)),
                  pl.BlockSpec((1, 4, NSA_TK, cols), lambda i, g, t: (g, 0, 0, 0)),
                  pl.BlockSpec((1, WIN_KEYS, cols), lambda i, g, t: (g, 0, 0)),
                  pl.BlockSpec((1, ncp, cols), lambda i, g, t: (g, 0, 0)),
                  pl.BlockSpec(at.shape, lambda i, g, t: (0, 0))],
        out_specs=pl.BlockSpec((NSA_TQ, gq * HEAD_DIM), lambda i, g, t: (i * nq + t, g)),
        out_shape=jax.ShapeDtypeStruct((b * s, d), BF16),
        scratch_shapes=[pltpu.VMEM((1, 1, cols), F32), pltpu.VMEM((1, HEAD_DIM + SUM_ROWS, cols), F32)],
        compiler_params=_cparams("parallel", "parallel", "arbitrary"),
        name="nsa_prompt",
    )(qt, gt, kcvc, kcvc_t, selb.reshape(b, s, gw), selb_t, winp, winp_t, tsel, tw, tc, at)


def _topk_lanes(x, k):
    ax = x.ndim - 1
    lane = lax.broadcasted_iota(I32, x.shape, ax)
    oshape = x.shape[:-1] + (LANES,)
    out_lane = lax.broadcasted_iota(I32, oshape, ax)
    ids = jnp.zeros(oshape, I32)
    for r in range(k):
        m = jnp.max(x, axis=ax, keepdims=True)
        idx = jnp.min(jnp.where(x == m, lane, x.shape[ax]), axis=ax, keepdims=True)
        ids = jnp.where(out_lane == r, idx, ids)
        x = jnp.where(lane == idx, -jnp.inf, x)
    return ids


def _slot_minor(cache, n_kv):
    pages, slots = cache.shape[:2]
    return jnp.transpose(cache, (0, 2, 3, 4, 1)).reshape(pages, n_kv, LANES, slots)


MOBA_SELECT_PAGES = 8


def _moba_select_kernel(pt_ref, *refs, nb):
    x_refs = refs[:MOBA_SELECT_PAGES]
    q_ref, ids_ref, ksum_s = refs[MOBA_SELECT_PAGES:]
    j = pl.program_id(1)
    ppb = MOBA_BLOCK // PAGE_SIZE
    per_step = MOBA_SELECT_PAGES // ppb

    @pl.when(j == 0)
    def _():
        ksum_s[...] = jnp.zeros_like(ksum_s)

    lane = lax.broadcasted_iota(I32, ksum_s.shape, 2)
    acc = ksum_s[...]
    for bi in range(per_step):
        x = x_refs[bi * ppb][0]
        for pg in range(1, ppb):
            x = x + x_refs[bi * ppb + pg][0]
        col = jnp.sum(x, axis=2, keepdims=True)
        acc = acc + jnp.where(lane == j * per_step + bi, col, 0.0)
    ksum_s[...] = acc

    @pl.when(j == pl.num_programs(1) - 1)
    def _():
        km = ksum_s[...] * (1.0 / MOBA_BLOCK)
        gate = sum(jnp.einsum("hqd,hdn->hqn", a, b_, preferred_element_type=F32)
                   for a in _split3(q_ref[0]) for b_ in _split3(km))
        blk = lax.broadcasted_iota(I32, gate.shape, 2)
        ids_ref[0] = _topk_lanes(jnp.where(blk < nb, gate, -jnp.inf), MOBA_TOPK)


def _moba_select(cache_t, page_table, q, past):
    db, n_pages = page_table.shape
    nb = past // MOBA_BLOCK
    assert past % MOBA_BLOCK == 0 and MOBA_TOPK <= nb <= LANES and n_pages % MOBA_SELECT_PAGES == 0
    q8 = jnp.pad(q.reshape(db, N_HEADS, 1, HEAD_DIM), ((0, 0), (0, 0), (0, 7), (0, 0)))
    grid_spec = pltpu.PrefetchScalarGridSpec(
        num_scalar_prefetch=1,
        grid=(db, n_pages // MOBA_SELECT_PAGES),
        in_specs=[pl.BlockSpec((1, N_HEADS, HEAD_DIM, PAGE_SIZE),
                               functools.partial(lambda i, j, pt, pg: (pt[i * n_pages + MOBA_SELECT_PAGES * j + pg],
                                                                       0, 0, 0), pg=pg))
                  for pg in range(MOBA_SELECT_PAGES)]
                 + [pl.BlockSpec((1, N_HEADS, 8, HEAD_DIM), lambda i, j, pt: (i, 0, 0, 0))],
        out_specs=pl.BlockSpec((1, N_HEADS, 8, LANES), lambda i, j, pt: (i, 0, 0, 0)),
        scratch_shapes=[pltpu.VMEM((N_HEADS, HEAD_DIM, LANES), F32)],
    )
    ids = pl.pallas_call(
        functools.partial(_moba_select_kernel, nb=nb),
        grid_spec=grid_spec,
        out_shape=jax.ShapeDtypeStruct((db, N_HEADS, 8, LANES), I32),
        compiler_params=_cparams("parallel", "arbitrary"),
        name="moba_select",
    )(page_table.reshape(-1), *([cache_t] * MOBA_SELECT_PAGES), q8)
    return ids[:, :, 0, :MOBA_TOPK]


def _pad_rows(x, rows):
    return jnp.concatenate([x, jnp.zeros((rows - x.shape[0],) + x.shape[1:], x.dtype)], axis=0)


MOBA_SAMPLE_HEADS = 4


def _moba_sample_kernel(pt_ref, ids_ref, q_ref, new_ref, *rest, n_tiles, last_blk, pages_per_block):
    tiles = rest[:MOBA_SAMPLE_HEADS * n_tiles]
    tab_ref, o_ref = rest[MOBA_SAMPLE_HEADS * n_tiles:]
    i, hg = pl.program_id(0), pl.program_id(1)
    for hh in range(MOBA_SAMPLE_HEADS):
        h = hg * MOBA_SAMPLE_HEADS + hh
        q8 = q_ref[0, hh] * ATTN_SCALE
        q8b = q8.astype(BF16)
        new = new_ref[0, hh]
        k_new, v_new = new[:, :HEAD_DIM], new[:, HEAD_DIM:]
        scores, vts = [], []
        for ti in range(n_tiles):
            k, pp = divmod(ti, pages_per_block)
            blk = ids_ref[(i * N_HEADS + h) * MOBA_TOPK + k]
            near = jnp.logical_and(blk == last_blk, pp == pages_per_block - 1)
            tile = tiles[hh * n_tiles + ti][0, 0]
            vts.append(tile[HEAD_DIM:].astype(BF16))
            scores.append(_mm(q8b, tile[:HEAD_DIM].astype(BF16))
                          + jnp.where(near, tab_ref[hh, 0:1, :], tab_ref[hh, 1:2, :]))
        s = jnp.concatenate(scores, axis=1)
        s_self = jnp.sum(q8 * k_new, axis=1, keepdims=True) + tab_ref[hh, 2:3, 0:1]
        m = jnp.maximum(jnp.max(s, axis=1, keepdims=True), s_self)
        p = jnp.exp(s - m)
        p_self = jnp.exp(s_self - m)
        l = jnp.sum(p, axis=1, keepdims=True) + p_self
        o = p_self * v_new
        for ti in range(n_tiles):
            o = o + _nt(p[:, ti * PAGE_SIZE:(ti + 1) * PAGE_SIZE].astype(BF16), vts[ti])
        o_ref[0, hh] = o / jnp.maximum(l, 1e-30)


def _moba_sample(cache_t, page_table, ids, q, kv_new, tab_ms, past):
    db, n_pages = page_table.shape
    ppb = MOBA_BLOCK // PAGE_SIZE
    n_tiles = MOBA_TOPK * ppb

    hps = MOBA_SAMPLE_HEADS

    def tile_spec(hh, ti):
        k, pp = divmod(ti, ppb)

        def imap(i, hg, pt, sel):
            h = hg * hps + hh
            return (pt[i * n_pages + sel[(i * N_HEADS + h) * MOBA_TOPK + k] * ppb + pp], h, 0, 0)
        return pl.BlockSpec((1, 1, LANES, PAGE_SIZE), imap)

    q8 = jnp.pad(q.reshape(db, N_HEADS, 1, HEAD_DIM), ((0, 0), (0, 0), (0, 7), (0, 0)))
    grid_spec = pltpu.PrefetchScalarGridSpec(
        num_scalar_prefetch=2,
        grid=(db, N_HEADS // hps),
        in_specs=[pl.BlockSpec((1, hps, 8, HEAD_DIM), lambda i, h, pt, sel: (i, h, 0, 0)),
                  pl.BlockSpec((1, hps, 1, LANES), lambda i, h, pt, sel: (i, h, 0, 0))]
                 + [tile_spec(hh, ti) for hh in range(hps) for ti in range(n_tiles)]
                 + [pl.BlockSpec((hps, 3, LANES), lambda i, h, pt, sel: (h, 0, 0))],
        out_specs=pl.BlockSpec((1, hps, 8, HEAD_DIM), lambda i, h, pt, sel: (i, h, 0, 0)),
    )
    out = pl.pallas_call(
        functools.partial(_moba_sample_kernel, n_tiles=n_tiles, last_blk=past // MOBA_BLOCK - 1,
                          pages_per_block=ppb),
        grid_spec=grid_spec,
        out_shape=jax.ShapeDtypeStruct((db, N_HEADS, 8, HEAD_DIM), F32),
        compiler_params=_cparams("parallel", "arbitrary"),
        name="moba_sample",
    )(page_table.reshape(-1), ids.reshape(-1), q8, kv_new.reshape(db, N_HEADS, 1, LANES),
      *([cache_t] * (hps * n_tiles)), tab_ms)
    return out[:, :, 0, :].reshape(db, N_HEADS * HEAD_DIM)


def _nsa_select_kernel(q_ref, kc_ref, tc_ref, a_ref, oc_ref, ids_ref, *, n_valid, n_sel, cur):
    nrow = kc_ref.shape[2]
    lane = lax.broadcasted_iota(I32, (8, nrow), 1)
    psums = []
    for g in range(NSA_KV_HEADS):
        q8 = (q_ref[0, g] * ATTN_SCALE).astype(BF16)
        kc = kc_ref[0, g].astype(BF16)
        s = jnp.where(lane < n_valid, _nt(q8, kc) + tc_ref[g], -jnp.inf)
        p_c = _softmax_rows(s)
        oc_ref[0, g] = _mm(p_c.astype(BF16), kc)[:, HEAD_DIM:]
        psums.append(jnp.sum(p_c[0:NSA_GROUP], axis=0, keepdims=True))
    p_sum = _pad_rows(jnp.concatenate(psums, axis=0), 8)
    a = a_ref[...]
    imp = sum(_mm(part, a) for part in _split3(p_sum))
    blk = lax.broadcasted_iota(I32, imp.shape, 1)
    valid = blk <= min(cur, n_sel - 1)
    forced = (blk == 0) | (blk >= cur - (N_LOCAL_SEL - 1))
    score = jnp.where(valid, jnp.where(forced, jnp.inf, imp), -jnp.inf)
    ids_ref[0] = _topk_lanes(score, SEL_TOPK)


def _nsa_select(q4, kcvc, tab_cs, past):
    db = q4.shape[0]
    nrow = kcvc.shape[2]
    n_cmp_valid = (past - (CMP_BLOCK - 1)) // CMP_STRIDE + 1
    n_sel = max(-(-(past + 1) // SEL_BLOCK), SEL_TOPK)
    n_cmp = n_sel * SEL_BLOCK // CMP_STRIDE - CMP_BLOCK // CMP_STRIDE + 1
    cur = past // SEL_BLOCK
    assert n_sel <= 2 * LANES and cur >= SEL_TOPK - 1 and n_cmp <= nrow
    a = jnp.asarray(_imp_matrix(2 * LANES, nrow, n_cmp).T, BF16)
    tc = jnp.concatenate([tab_cs.reshape(NSA_KV_HEADS, NSA_GROUP, nrow)] * 2, axis=1)
    return pl.pallas_call(
        functools.partial(_nsa_select_kernel, n_valid=n_cmp_valid, n_sel=n_sel, cur=cur),
        grid=(db,),
        in_specs=[pl.BlockSpec((1, NSA_KV_HEADS, 8, LANES), lambda i: (i, 0, 0, 0)),
                  pl.BlockSpec((1, NSA_KV_HEADS, nrow, LANES), lambda i: (i, 0, 0, 0)),
                  pl.BlockSpec(tc.shape, lambda i: (0, 0, 0)),
                  pl.BlockSpec(a.shape, lambda i: (0, 0))],
        out_specs=[pl.BlockSpec((1, NSA_KV_HEADS, 8, HEAD_DIM), lambda i: (i, 0, 0, 0)),
                   pl.BlockSpec((1, 8, LANES), lambda i: (i, 0, 0))],
        out_shape=[jax.ShapeDtypeStruct((db, NSA_KV_HEADS, 8, HEAD_DIM), F32),
                   jax.ShapeDtypeStruct((db, 8, LANES), I32)],
        compiler_params=_cparams("parallel"),
        name="nsa_select",
    )(q4, kcvc, tc, a)


def _nsa_sample_kernel(pt_ref, ids_ref, q_ref, oc_ref, gate_ref, snew_ref, wnew_ref, st_ref, *rest,
                       n_cache_blk):
    tiles = rest[:SEL_TOPK]
    tms_ref, tws_ref, eye_ref, o_ref, st_out = rest[SEL_TOPK:]
    i, g = pl.program_id(0), pl.program_id(1)
    q8 = q_ref[0, 0] * ATTN_SCALE
    q8b = q8.astype(BF16)
    halves = PAGE_SIZE // SEL_BLOCK
    last_page = n_cache_blk // halves - 1
    half_of_lane = jnp.right_shift(lax.broadcasted_iota(I32, (8, PAGE_SIZE), 1), SEL_SHIFT)
    t0 = tms_ref[0, 2][:, 0:1]

    scores, vts = [], []
    for k in range(SEL_TOPK):
        blk = ids_ref[(i * NSA_KV_HEADS + g) * SEL_TOPK + k]
        in_cache = blk < n_cache_blk
        half = jnp.where(in_cache, lax.rem(blk, halves), -1)
        near = jnp.logical_and(in_cache, blk // halves == last_page)
        tile = tiles[k][0, 0]
        vts.append(tile[HEAD_DIM:].astype(BF16))
        s = _mm(q8b, tile[:HEAD_DIM].astype(BF16)) + jnp.where(near, tms_ref[0, 0], tms_ref[0, 1])
        scores.append(jnp.where(half_of_lane == half, s, -jnp.inf))
    s = jnp.concatenate(scores, axis=1)
    snew = snew_ref[0]
    s_self = jnp.sum(q8 * snew[:, :HEAD_DIM], axis=1, keepdims=True) + t0
    m = jnp.maximum(jnp.max(s, axis=1, keepdims=True), s_self)
    p = jnp.exp(s - m)
    p_self = jnp.exp(s_self - m)
    l = jnp.sum(p, axis=1, keepdims=True) + p_self
    o_s = p_self * snew[:, HEAD_DIM:]
    for k in range(SEL_TOPK):
        o_s = o_s + _nt(p[:, k * PAGE_SIZE:(k + 1) * PAGE_SIZE].astype(BF16), vts[k])
    o_s = o_s / jnp.maximum(l, 1e-30)

    ws = st_ref[0, 0]
    wl = lax.broadcasted_iota(I32, (8, WINDOW), 1)
    s_w = jnp.where(wl >= 1, _mm(q8b, ws[:HEAD_DIM].astype(BF16)) + tws_ref[0], -jnp.inf)
    wnew = wnew_ref[0]
    w_self = jnp.sum(q8 * wnew[:, :HEAD_DIM], axis=1, keepdims=True) + t0
    m = jnp.maximum(jnp.max(s_w, axis=1, keepdims=True), w_self)
    p = jnp.exp(s_w - m)
    p_self = jnp.exp(w_self - m)
    l = jnp.sum(p, axis=1, keepdims=True) + p_self
    o_w = (_nt(p.astype(BF16), ws[HEAD_DIM:].astype(BF16)) + p_self * wnew[:, HEAD_DIM:]) / jnp.maximum(l, 1e-30)

    gate = gate_ref[0, 0]
    o_ref[0, 0] = gate[:, 0:1] * oc_ref[0, 0] + gate[:, 1:2] * o_s + gate[:, 2:3] * o_w
    new_col = sum(_nt(eye_ref[...], _pad_rows(part, 8)) for part in _split3(wnew))[:, 0:1]
    wcol = lax.broadcasted_iota(I32, ws.shape, 1)
    st_out[0, 0] = jnp.where(wcol == WINDOW - 1, new_col, pltpu.roll(ws, WINDOW - 1, 1))


def _nsa_sample(cache_t, state_t, page_table, ids, q4, o_c, gates, sel_new, win_new, tabs, past):
    db, n_pages = page_table.shape
    gw = NSA_KV_HEADS * LANES
    halves = PAGE_SIZE // SEL_BLOCK
    n_cache_blk = past // SEL_BLOCK
    assert state_t.shape[-1] == WINDOW and past >= WINDOW and past % PAGE_SIZE == 0

    def tile_spec(k):
        def imap(i, g, pt, sel):
            blk = jnp.minimum(sel[(i * NSA_KV_HEADS + g) * SEL_TOPK + k], n_cache_blk - 1)
            return (pt[i * n_pages + blk // halves], g, 0, 0)
        return pl.BlockSpec((1, 1, LANES, PAGE_SIZE), imap)

    grp = lambda n: pl.BlockSpec((1, 1, 8, n), lambda i, g, pt, sel: (i, g, 0, 0))
    new = pl.BlockSpec((1, 1, LANES), lambda i, g, pt, sel: (i, 0, g))
    st = pl.BlockSpec((1, 1, LANES, WINDOW), lambda i, g, pt, sel: (i, g, 0, 0))
    pad8 = lambda x: jnp.concatenate([x, jnp.zeros_like(x)], axis=-2)
    tms = pad8(tabs["ms"].reshape(NSA_KV_HEADS, NSA_GROUP, 3, LANES).transpose(0, 2, 1, 3))
    tws = pad8(tabs["ws"].reshape(NSA_KV_HEADS, NSA_GROUP, WINDOW))
    eye = jnp.asarray(np.eye(LANES), BF16)
    grid_spec = pltpu.PrefetchScalarGridSpec(
        num_scalar_prefetch=2,
        grid=(db, NSA_KV_HEADS),
        in_specs=[grp(HEAD_DIM), grp(HEAD_DIM), grp(LANES), new, new, st] + [tile_spec(k) for k in range(SEL_TOPK)]
                 + [pl.BlockSpec((1, 3, 8, LANES), lambda i, g, pt, sel: (g, 0, 0, 0)),
                    pl.BlockSpec((1, 8, WINDOW), lambda i, g, pt, sel: (g, 0, 0)),
                    pl.BlockSpec(eye.shape, lambda i, g, pt, sel: (0, 0))],
        out_specs=[grp(HEAD_DIM), st],
    )
    o, st_new = pl.pallas_call(
        functools.partial(_nsa_sample_kernel, n_cache_blk=n_cache_blk),
        grid_spec=grid_spec,
        out_shape=[jax.ShapeDtypeStruct((db, NSA_KV_HEADS, 8, HEAD_DIM), F32),
                   jax.ShapeDtypeStruct(state_t.shape, F32)],
        compiler_params=_cparams("parallel", "arbitrary"),
        name="nsa_sample",
    )(page_table.reshape(-1), ids[:, :NSA_KV_HEADS, :SEL_TOPK].reshape(-1), q4, o_c, gates,
      sel_new.reshape(db, 1, gw), win_new.reshape(db, 1, gw), state_t,
      *([cache_t] * SEL_TOPK), tms, tws, eye)
    return o[:, :, :NSA_GROUP, :].reshape(db, N_HEADS * HEAD_DIM), st_new


def kernel(x_prompt, x_sample, cache_moba_kv, cache_nsa_cmp_kv, cache_nsa_sel_kv, state_nsa_win_kv, page_table,
           rel_bias, attn_norm, ffn_norm, moba_w_qkv, moba_q_norm, moba_k_norm, moba_w_o, nsa_w_in,
           nsa_gate_bias, nsa_q_norm, nsa_k_norm, nsa_cmp_pos, nsa_cmp_w1, nsa_cmp_b1, nsa_cmp_w2, nsa_w_o,
           ffn_w_up, ffn_w_down):
    b, s, d = x_prompt.shape
    db = x_sample.shape[0]
    assert x_sample.shape[1] == 1 and d == N_HEADS * HEAD_DIM
    past = page_table.shape[1] * PAGE_SIZE
    depth = attn_norm.shape[0]
    tabs = _bias_tables(rel_bias, s, past)
    hp = x_prompt.reshape(b * s, d)
    hs = x_sample.reshape(db, d)
    outs = {k: [] for k in ("moba_p", "moba_s", "cmp_p", "cmp_s", "sel_p", "sel_s", "win_p", "win_s")}
    kvshape = lambda n, g: (n, -1, g, 2, HEAD_DIM)
    for i in range(depth):
        j = i // 2
        if i % 2 == 0:
            qp_t, kvp_t, kvpb, kmean, kvpb_t = _proj_moba(hp, attn_norm[i], moba_w_qkv[j], moba_q_norm[j],
                                                          moba_k_norm[j], qdt=BF16, seq=s)
            qs, kvs, _ = _proj_moba(hs, attn_norm[i], moba_w_qkv[j], moba_q_norm[j], moba_k_norm[j], qdt=F32)
            op = _moba_prompt(qp_t, kvpb, kvpb_t, kmean, tabs["moba"], b, s)
            cache_t = _slot_minor(cache_moba_kv[j], N_HEADS)
            ids = _moba_select(cache_t, page_table, qs, past)
            osm = _moba_sample(cache_t, page_table, ids, qs, kvs, tabs["ms"], past)
            outs["moba_p"].append(_token_major(kvp_t, b, s, N_HEADS))
            outs["moba_s"].append(kvs.reshape(kvshape(db, N_HEADS)))
            w_o = moba_w_o[j]
        else:
            qp_t, cp, sp_t, wp_t, spb, wpb, gp_t, cp_t, spb_t, wpb_t = _proj_nsa(
                hp, attn_norm[i], nsa_w_in[j], nsa_gate_bias[j], nsa_q_norm[j], nsa_k_norm[j], qdt=BF16, seq=s)
            qs, cs, ss, ws, _, _, gs = _proj_nsa(hs, attn_norm[i], nsa_w_in[j], nsa_gate_bias[j],
                                                 nsa_q_norm[j], nsa_k_norm[j], qdt=F32)
            cw = _compress_weights(nsa_cmp_pos[j], nsa_cmp_w1[j], nsa_cmp_b1[j], nsa_cmp_w2[j], nsa_k_norm[j, 0])
            kcvc_p, kcvc_pt = _compress_prompt(cp, cw, b, s)
            op = _nsa_prompt(qp_t, gp_t, kcvc_p, kcvc_pt, spb, spb_t, wpb, wpb_t, tabs, b, s)
            kcvc_s = _compress_sample(cache_nsa_cmp_kv[j], page_table, cs, cw, past)
            pad8 = lambda x: jnp.concatenate([x, jnp.zeros_like(x)], axis=2)
            q4 = pad8(qs.reshape(db, NSA_KV_HEADS, NSA_GROUP, HEAD_DIM))
            q4w = jnp.pad(q4, ((0, 0), (0, 0), (0, 0), (0, LANES - HEAD_DIM)))
            g4 = pad8(gs.reshape(db, NSA_KV_HEADS, LANES)[:, :, :NSA_GROUP * N_BRANCH]
                      .reshape(db, NSA_KV_HEADS, NSA_GROUP, N_BRANCH))
            g4 = jnp.pad(g4, ((0, 0), (0, 0), (0, 0), (0, LANES - N_BRANCH)))
            o_c, sel_ids = _nsa_select(q4w, kcvc_s, tabs["cs"], past)
            osm, st_t = _nsa_sample(_slot_minor(cache_nsa_sel_kv[j], NSA_KV_HEADS),
                                    _slot_minor(state_nsa_win_kv[j], NSA_KV_HEADS), page_table, sel_ids, q4, o_c,
                                    g4, ss, ws, tabs, past)
            ws_new = st_t.reshape(db, NSA_KV_HEADS, 2, HEAD_DIM, WINDOW).transpose(0, 4, 1, 2, 3)
            gsh = (NSA_KV_HEADS, 2, HEAD_DIM)
            outs["cmp_p"].append(_token_major(cp_t, b, s, NSA_KV_HEADS))
            outs["cmp_s"].append(cs.reshape((db, 1) + gsh))
            outs["sel_p"].append(_token_major(sp_t, b, s, NSA_KV_HEADS))
            outs["sel_s"].append(ss.reshape((db, 1) + gsh))
            outs["win_p"].append(_token_major(wp_t, b, s, NSA_KV_HEADS)[:, s - min(WINDOW, s):])
            outs["win_s"].append(ws_new.reshape((db, WINDOW) + gsh))
            w_o = nsa_w_o[j]
        hp = _attn_out_mlp(hp, op, w_o, ffn_norm[i], ffn_w_up[i], ffn_w_down[i])
        hs = _attn_out_mlp(hs, osm, w_o, ffn_norm[i], ffn_w_up[i], ffn_w_down[i])
    stack = lambda k: jnp.stack(outs[k])
    return (hp.reshape(b, s, d), hs.reshape(db, 1, d), stack("moba_p"), stack("moba_s"), stack("cmp_p"),
            stack("cmp_s"), stack("sel_p"), stack("sel_s"), stack("win_p"), stack("win_s"))
```

```python
import functools
import math

import numpy as np
import jax
import jax.numpy as jnp
from jax import lax
from jax.experimental import pallas as pl
from jax.experimental.pallas import tpu as pltpu

F32 = jnp.float32
BF16 = jnp.bfloat16
I32 = jnp.int32

N_HEADS = 16
HEAD_DIM = 64
NORM_EPS = 1e-6
ATTN_SCALE = HEAD_DIM ** -0.5
REL_BUCKETS = 32
REL_MAX_DIST = 128
PAGE_SIZE = 128
MOBA_BLOCK = 256
MOBA_TOPK = 3
NSA_KV_HEADS = 4
NSA_GROUP = N_HEADS // NSA_KV_HEADS
CMP_BLOCK = 32
CMP_STRIDE = 16
CMP_HIDDEN = 2 * HEAD_DIM
SEL_BLOCK = 64
SEL_TOPK = 16
N_LOCAL_SEL = 2
WINDOW = 512
N_BRANCH = 3

LANES = 128
LANE_SHIFT = 7
SEL_SHIFT = 6
MASK_NEG = -(2.0 ** 100)
VMEM_LIMIT = 56 * 1024 * 1024
MOBA_UNROLL = 4
MOBA_HEADS = 4
NSA_UNROLL = 4
NSA_TQ = 256
NSA_TK = 256
WIN_KEYS = WINDOW + NSA_TQ
CMP_NEAR = 32


def _cparams(*sem):
    return pltpu.CompilerParams(dimension_semantics=sem, vmem_limit_bytes=VMEM_LIMIT)


def _nt(a, b):
    return lax.dot_general(a, b, (((1,), (1,)), ((), ())), preferred_element_type=F32)


def _mm(a, b):
    return jnp.dot(a, b, preferred_element_type=F32)


def _split2(x):
    hi = x.astype(BF16)
    lo = (x - hi.astype(F32)).astype(BF16)
    return hi, lo


def _split3(x):
    hi = x.astype(BF16)
    r = x - hi.astype(F32)
    mid = r.astype(BF16)
    lo = (r - mid.astype(F32)).astype(BF16)
    return hi, mid, lo


def _rms_rows(x, g):
    return x * lax.rsqrt(jnp.mean(x * x, axis=-1, keepdims=True) + NORM_EPS) * g


def _group_norm(h, gmat_ref, gain, k_lanes_only):
    hi, lo = _split2(h * h)
    gm = gmat_ref[...]
    ss = _mm(hi, gm) + _mm(lo, gm)
    y = h * lax.rsqrt(ss * (1.0 / HEAD_DIM) + NORM_EPS) * gain
    if k_lanes_only:
        lane = lax.broadcasted_iota(I32, h.shape, 1)
        y = jnp.where((lane & (LANES - 1)) < HEAD_DIM, y, h)
    return y


def _softmax_rows(s):
    m = jnp.max(s, axis=-1, keepdims=True)
    m = jnp.where(m == -jnp.inf, 0.0, m)
    e = jnp.exp(s - m)
    return e * (1.0 / jnp.maximum(jnp.sum(e, axis=-1, keepdims=True), 1e-30))


def _bucket_of_dist():
    n = np.arange(REL_MAX_DIST + 1)
    max_exact = REL_BUCKETS // 2
    nf = np.maximum(n, 1).astype(np.float32)
    large = max_exact + (np.log(nf / np.float32(max_exact)) / np.float32(math.log(REL_MAX_DIST / max_exact))
                         * np.float32(REL_BUCKETS - max_exact)).astype(np.int32)
    large = np.minimum(large, REL_BUCKETS - 1)
    return np.where(n < max_exact, n, large).astype(np.int32)


def _dist_tables(seq, past):
    c = lambda d: np.clip(d, 0, REL_MAX_DIST)
    ncp = seq // CMP_STRIDE
    i256 = np.arange(MOBA_BLOCK)[:, None]
    j256 = np.arange(MOBA_BLOCK)[None, :]
    q128 = np.arange(NSA_TQ)[None, :]
    t = {}
    t["moba"] = np.stack([c(j256 - i256), c(MOBA_BLOCK + j256 - i256),
                          np.full((MOBA_BLOCK, MOBA_BLOCK), REL_MAX_DIST)])
    t["sel"] = np.stack([c(NSA_TQ * v + q128 - np.arange(NSA_TK)[:, None]) for v in range(4)])
    t["win"] = c(q128 + WINDOW - np.arange(WIN_KEYS)[:, None])
    mrow = np.arange(ncp)[:, None]
    near = q128 - CMP_STRIDE * (mrow - CMP_NEAR // 2) - (CMP_BLOCK - 1)
    t["cmp"] = np.where(mrow < CMP_NEAR, c(near), REL_MAX_DIST)
    r128 = np.arange(LANES)
    t["ms"] = np.stack([c(LANES - r128), np.full(LANES, REL_MAX_DIST), np.zeros(LANES, np.int64)])
    nrow = _cmp_rows(past)
    t["cs"] = c(past - CMP_STRIDE * np.arange(nrow) - (CMP_BLOCK - 1))
    t["ws"] = c(WINDOW - np.arange(WINDOW))
    return t


def _cmp_rows(past):
    return -(-(past // CMP_STRIDE + 4) // 16) * 16


def _tab_kernel(idx_ref, rb_ref, o_ref):
    idx = idx_ref[...]
    b = lax.broadcasted_iota(I32, (REL_BUCKETS, idx.shape[1]), 0)
    oh = jnp.where(b == idx, 1.0, 0.0).astype(BF16)
    o_ref[...] = _mm(rb_ref[0], oh) + _mm(rb_ref[1], oh) + _mm(rb_ref[2], oh)


def _bias_tables(rel_bias, seq, past):
    pats = _dist_tables(seq, past)
    bucket = _bucket_of_dist()
    chunk = 16384
    flat, spans, off = [], {}, 0
    for name, d in pats.items():
        n = d.size
        pad = -n % LANES
        flat.append(bucket[d.reshape(-1)])
        flat.append(np.zeros(pad, np.int32))
        spans[name] = (off, n, d.shape)
        off += n + pad
    total = -(-off // chunk) * chunk
    flat.append(np.zeros(total - off, np.int32))
    idx = jnp.asarray(np.concatenate(flat).astype(np.int32)).reshape(1, total)
    rb3 = jnp.stack(_split3(rel_bias.T.astype(F32)))
    tab = pl.pallas_call(
        _tab_kernel,
        grid=(total // chunk,),
        in_specs=[pl.BlockSpec((1, chunk), lambda i: (0, i)),
                  pl.BlockSpec((3, N_HEADS, REL_BUCKETS), lambda i: (0, 0, 0))],
        out_specs=pl.BlockSpec((N_HEADS, chunk), lambda i: (0, i)),
        out_shape=jax.ShapeDtypeStruct((N_HEADS, total), F32),
        compiler_params=_cparams("parallel"),
        name="bias_tables",
    )(idx, rb3)
    return {name: tab[:, o:o + n].reshape((N_HEADS,) + shp) for name, (o, n, shp) in spans.items()}


def _group_mats():
    i = np.arange(2 * LANES)
    g64 = (i[:, None] // HEAD_DIM == i[None, :] // HEAD_DIM)
    g128 = (i[:, None] // LANES == i[None, :] // LANES) & ((i[:, None] % LANES) < HEAD_DIM)
    return jnp.asarray(g64, BF16), jnp.asarray(g128, BF16)


def _interleave_gain(g, n):
    return jnp.tile(jnp.concatenate([g.astype(F32), jnp.ones((HEAD_DIM,), F32)]), n).reshape(1, n * LANES)


def _store_rows_or_columns(ref, sl, x):
    if len(ref.shape) == 3:
        ref[0, sl, :] = x.T.astype(ref.dtype)
    else:
        ref[:, sl] = x.astype(ref.dtype)


def _token_major(x_t, b, s, n_kv):
    return x_t.reshape(b, n_kv, 2, HEAD_DIM, s).transpose(0, 4, 1, 2, 3)


def _proj_moba_kernel(x_ref, an_ref, wq_ref, wkv_ref, qg_ref, kg_ref, g64_ref, g128_ref,
                      q_ref, kv_ref, kvb_ref, *rest, n_mean):
    xn = _rms_rows(x_ref[...], an_ref[...]).astype(BF16)
    cw = 2 * LANES
    for c in range(wq_ref.shape[1] // cw):
        sl = slice(c * cw, (c + 1) * cw)
        h = _mm(xn, wq_ref[:, sl])
        q = _group_norm(h, g64_ref, qg_ref[:, sl], False)
        _store_rows_or_columns(q_ref, sl, q)
    for c in range(wkv_ref.shape[1] // cw):
        sl = slice(c * cw, (c + 1) * cw)
        h = _mm(xn, wkv_ref[:, sl])
        kv = _group_norm(h, g128_ref, kg_ref[:, sl], True)
        _store_rows_or_columns(kv_ref, sl, kv)
        kvb_ref[:, sl] = kv.astype(BF16)
        if n_mean:
            km_ref, kvtb_ref = rest
            _store_rows_or_columns(kvtb_ref, sl, kv)
            for r in range(n_mean):
                km_ref[0, r:r + 1, sl] = jnp.mean(kv[r * MOBA_BLOCK:(r + 1) * MOBA_BLOCK], axis=0, keepdims=True)


def _proj_moba(x, an, w_qkv, q_gain, k_gain, *, qdt, seq=None):
    m, d = x.shape
    tm = min(m, 512)
    with_mean = seq is not None
    g64, g128 = _group_mats()
    w3 = w_qkv.reshape(d, 3, N_HEADS, HEAD_DIM)
    wkv = jnp.stack([w3[:, 1], w3[:, 2]], axis=2).reshape(d, 2 * d).astype(BF16)
    wq = w3[:, 0].reshape(d, d).astype(BF16)
    qg = jnp.tile(q_gain.astype(F32), N_HEADS).reshape(1, d)
    kg = _interleave_gain(k_gain, N_HEADS)
    qn = wq.shape[1]
    n_mean = tm // MOBA_BLOCK if with_mean else 0
    const = lambda i: (0, 0)
    if seq is None:
        q_shape, q_spec = (m, qn), pl.BlockSpec((tm, qn), lambda i: (i, 0))
        kv_shape, kv_spec = (m, 2 * d), pl.BlockSpec((tm, 2 * d), lambda i: (i, 0))
    else:
        nt = seq // tm
        q_shape, q_spec = (m // seq, qn, seq), pl.BlockSpec((1, qn, tm), lambda i: (i // nt, 0, i % nt))
        kv_shape, kv_spec = (m // seq, 2 * d, seq), pl.BlockSpec((1, 2 * d, tm), lambda i: (i // nt, 0, i % nt))
    out_shape = [jax.ShapeDtypeStruct(q_shape, qdt), jax.ShapeDtypeStruct(kv_shape, F32),
                 jax.ShapeDtypeStruct((m, 2 * d), BF16)]
    out_specs = [q_spec, kv_spec, pl.BlockSpec((tm, 2 * d), lambda i: (i, 0))]
    if n_mean:
        out_shape += [jax.ShapeDtypeStruct((m // tm, n_mean, 2 * d), F32), jax.ShapeDtypeStruct(kv_shape, BF16)]
        out_specs += [pl.BlockSpec((1, n_mean, 2 * d), lambda i: (i, 0, 0)), kv_spec]
    return pl.pallas_call(
        functools.partial(_proj_moba_kernel, n_mean=n_mean),
        grid=(m // tm,),
        in_specs=[pl.BlockSpec((tm, d), lambda i: (i, 0)), pl.BlockSpec((1, d), const),
                  pl.BlockSpec((d, qn), const), pl.BlockSpec((d, 2 * d), const),
                  pl.BlockSpec((1, qn), const), pl.BlockSpec((1, 2 * d), const),
                  pl.BlockSpec(g64.shape, const), pl.BlockSpec(g128.shape, const)],
        out_specs=out_specs, out_shape=out_shape,
        compiler_params=_cparams("parallel"),
        name="proj_moba",
    )(x, an.reshape(1, d).astype(F32), wq, wkv, qg, kg, g64, g128)


def _proj_nsa_kernel(x_ref, an_ref, wq_ref, wkv_ref, wg_ref, gb_ref, qg_ref, kg_ref, g64_ref, g128_ref,
                     q_ref, cmp_ref, sel_ref, win_ref, selb_ref, winb_ref, gate_ref, *rest):
    cmp_t_ref, selb_t_ref, winb_t_ref = rest if rest else (None, None, None)
    xn = _rms_rows(x_ref[...], an_ref[...]).astype(BF16)
    cw = 2 * LANES
    for c in range(wq_ref.shape[1] // cw):
        sl = slice(c * cw, (c + 1) * cw)
        h = _mm(xn, wq_ref[:, sl])
        _store_rows_or_columns(q_ref, sl, _group_norm(h, g64_ref, qg_ref[:, sl], False))
    per_branch = NSA_KV_HEADS * LANES // cw
    for c in range(wkv_ref.shape[1] // cw):
        sl = slice(c * cw, (c + 1) * cw)
        br, cc = divmod(c, per_branch)
        osl = slice(cc * cw, (cc + 1) * cw)
        h = _mm(xn, wkv_ref[:, sl])
        if br == 0:
            cmp_ref[:, osl] = h
            if cmp_t_ref is not None:
                _store_rows_or_columns(cmp_t_ref, osl, h)
        else:
            kv = _group_norm(h, g128_ref, kg_ref[:, sl], True)
            o32, o16, o16t = (sel_ref, selb_ref, selb_t_ref) if br == 1 else (win_ref, winb_ref, winb_t_ref)
            _store_rows_or_columns(o32, osl, kv)
            o16[:, osl] = kv.astype(BF16)
            if o16t is not None:
                _store_rows_or_columns(o16t, osl, kv)
    hg = _mm(xn, wg_ref[...]) + gb_ref[...]
    _store_rows_or_columns(gate_ref, slice(0, hg.shape[1]), 1.0 / (1.0 + jnp.exp(-hg)))


def _proj_nsa(x, an, w_in, gate_bias, q_gain, k_gain, *, qdt, seq=None):
    m, d = x.shape
    tm = min(m, 512)
    g64, g128 = _group_mats()
    kvd = N_BRANCH * NSA_KV_HEADS * LANES
    gw = NSA_KV_HEADS * LANES
    ng = NSA_GROUP * N_BRANCH
    qg = jnp.tile(q_gain.astype(F32), N_HEADS).reshape(1, d)
    wq = w_in[:, :d].astype(BF16)
    wkv = w_in[:, d:d + kvd].astype(BF16)
    wg = jnp.zeros((d, NSA_KV_HEADS, LANES), F32).at[:, :, :ng].set(
        w_in[:, d + kvd:].reshape(d, NSA_KV_HEADS, ng)).reshape(d, gw).astype(BF16)
    gb = jnp.zeros((NSA_KV_HEADS, LANES), F32).at[:, :ng].set(
        gate_bias.astype(F32).reshape(NSA_KV_HEADS, ng)).reshape(1, gw)
    kg = jnp.concatenate([jnp.ones((1, gw), F32), _interleave_gain(k_gain[1], NSA_KV_HEADS),
                          _interleave_gain(k_gain[2], NSA_KV_HEADS)], axis=1)
    qn = wq.shape[1]
    const = lambda i: (0, 0)
    row = lambda n: pl.BlockSpec((tm, n), lambda i: (i, 0))
    rows32 = jax.ShapeDtypeStruct((m, gw), F32)
    if seq is None:
        q_shape, q_spec = jax.ShapeDtypeStruct((m, qn), qdt), row(qn)
        kv_shape, kv_spec, gate_shape, gate_spec, extra_shape, extra_spec = rows32, row(gw), rows32, row(gw), [], []
    else:
        nt = seq // tm
        cols = lambda n: pl.BlockSpec((1, n, tm), lambda i: (i // nt, 0, i % nt))
        q_shape, q_spec = jax.ShapeDtypeStruct((m // seq, qn, seq), qdt), cols(qn)
        kv_shape, kv_spec = jax.ShapeDtypeStruct((m // seq, gw, seq), F32), cols(gw)
        gate_shape, gate_spec = kv_shape, kv_spec
        extra_shape = [kv_shape] + [jax.ShapeDtypeStruct((m // seq, gw, seq), BF16)] * 2
        extra_spec = [kv_spec] * 3
    return pl.pallas_call(
        _proj_nsa_kernel,
        grid=(m // tm,),
        in_specs=[row(d), pl.BlockSpec((1, d), const), pl.BlockSpec((d, qn), const),
                  pl.BlockSpec((d, kvd), const), pl.BlockSpec((d, gw), const), pl.BlockSpec((1, gw), const),
                  pl.BlockSpec((1, qn), const), pl.BlockSpec((1, kvd), const),
                  pl.BlockSpec(g64.shape, const), pl.BlockSpec(g128.shape, const)],
        out_specs=[q_spec, row(gw), kv_spec, kv_spec, row(gw), row(gw), gate_spec] + extra_spec,
        out_shape=[q_shape, rows32, kv_shape, kv_shape]
                  + [jax.ShapeDtypeStruct((m, gw), BF16)] * 2 + [gate_shape] + extra_shape,
        compiler_params=_cparams("parallel"),
        name="proj_nsa",
    )(x, an.reshape(1, d).astype(F32), wq, wkv, wg, gb, qg, kg, g64, g128)


def _mlp_kernel(h_ref, o_ref, wo_ref, g_ref, wup_ref, wdn_ref, out_ref, xn_s):
    @pl.when(pl.program_id(1) == 0)
    def _():
        h1 = h_ref[...] + _mm(o_ref[...], wo_ref[...])
        out_ref[...] = h1
        xn_s[...] = _rms_rows(h1, g_ref[...]).astype(BF16)

    u = _mm(xn_s[...], wup_ref[...])
    u = jnp.square(jnp.maximum(u, 0.0)).astype(BF16)
    out_ref[...] += _mm(u, wdn_ref[...])


def _attn_out_mlp(h, o, w_o, fn, w_up, w_down):
    m, d = h.shape
    dff = w_up.shape[1]
    tm = min(m, 1024)
    tf = 1024
    return pl.pallas_call(
        _mlp_kernel,
        grid=(m // tm, dff // tf),
        in_specs=[pl.BlockSpec((tm, d), lambda i, f: (i, 0)), pl.BlockSpec((tm, d), lambda i, f: (i, 0)),
                  pl.BlockSpec((d, d), lambda i, f: (0, 0)), pl.BlockSpec((1, d), lambda i, f: (0, 0)),
                  pl.BlockSpec((d, tf), lambda i, f: (0, f)), pl.BlockSpec((tf, d), lambda i, f: (f, 0))],
        out_specs=pl.BlockSpec((tm, d), lambda i, f: (i, 0)),
        out_shape=jax.ShapeDtypeStruct((m, d), F32),
        scratch_shapes=[pltpu.VMEM((tm, d), BF16)],
        compiler_params=_cparams("parallel", "arbitrary"),
        name="attn_out_mlp",
    )(h, o.astype(BF16), w_o.astype(BF16), fn.reshape(1, d).astype(F32), w_up.astype(BF16), w_down.astype(BF16))


def _topk_rows(x, k, on_pick):
    n = x.shape[0]
    row = lax.broadcasted_iota(I32, x.shape, 0)
    for _ in range(k):
        m = jnp.max(x, axis=0, keepdims=True)
        idx = jnp.min(jnp.where(x == m, row, n), axis=0, keepdims=True)
        hit = row == idx
        on_pick(hit, m)
        x = jnp.where(hit, -jnp.inf, x)


SUM_ROWS = 16


def _with_sum_rows(v_t):
    return jnp.concatenate([v_t, jnp.ones((SUM_ROWS, v_t.shape[1]), v_t.dtype)], axis=0)


def _online_softmax_step(m_ref, acc_ref, tiles):
    m_old = m_ref[0]
    mx = jnp.max(tiles[0][0], axis=0, keepdims=True)
    for s, _ in tiles[1:]:
        mx = jnp.maximum(mx, jnp.max(s, axis=0, keepdims=True))
    mn = jnp.maximum(m_old, mx)
    acc = jnp.exp(m_old - mn) * acc_ref[0]
    for s, v_aug in tiles:
        acc = acc + _mm(v_aug, jnp.exp(s - mn).astype(BF16))
    m_ref[0] = mn
    acc_ref[0] = acc


def _softmax_finish(acc):
    return acc[:HEAD_DIM] * (1.0 / jnp.maximum(acc[HEAD_DIM:HEAD_DIM + 1], 1e-30))


def _unrolled_range(n, unroll, visit):
    rem = lax.rem(n, unroll)
    done = 0
    size = 1
    while size < unroll:
        take = jnp.bitwise_and(rem, size)

        @pl.when(take != 0)
        def _(done=done, size=size):
            visit([done + k for k in range(size)])

        done = done + take
        size *= 2

    def body(i, carry):
        first = rem + unroll * i
        visit([first + k for k in range(unroll)])
        return carry

    lax.fori_loop(0, n // unroll, body, 0)


def _softmax_state_init(m_ref, acc_ref):
    m_ref[...] = jnp.full(m_ref.shape, -jnp.inf, F32)
    acc_ref[...] = jnp.zeros(acc_ref.shape, F32)


def _moba_prompt_kernel(qt_ref, kv_ref, kvt_ref, km_ref, tb_ref, o_ref, *state, nb):
    cur = pl.program_id(2)
    tq = MOBA_BLOCK
    key_lane = lax.broadcasted_iota(I32, (tq, LANES), 1)
    rown = lax.broadcasted_iota(I32, (nb, tq), 0)
    heads = [state[2 * hh:2 * hh + 2] for hh in range(MOBA_HEADS)]
    for st in heads:
        _softmax_state_init(*st)
    qzs, qaugs = [], []
    for hh in range(MOBA_HEADS):
        qt = qt_ref[0, hh * HEAD_DIM:(hh + 1) * HEAD_DIM, :] * jnp.asarray(ATTN_SCALE, BF16)
        qz = jnp.concatenate([qt, jnp.zeros_like(qt)], axis=0)
        km_hi, km_lo = _split2(km_ref[0, :, hh * LANES:(hh + 1) * LANES])
        gate = _mm(km_hi, qz) + _mm(km_lo, qz)
        state = [jnp.full((nb, tq), MASK_NEG, F32)]

        def pick(hit, m, state=state):
            state[0] = jnp.where(hit, jnp.where(m > -jnp.inf, 0.0, state[0]), state[0])

        _topk_rows(jnp.where(rown < cur, gate, -jnp.inf), MOBA_TOPK, pick)
        negt = jnp.concatenate([state[0], jnp.zeros((LANES - nb, tq), F32)], axis=0).astype(BF16)
        qzs.append(qz)
        qaugs.append(jnp.concatenate([qz, negt], axis=0))

    def k_rows(hh, n):
        return kv_ref[0, pl.ds(pl.multiple_of(n * tq, tq), tq), hh * LANES:(hh + 1) * LANES]

    def v_cols(hh, n):
        return _with_sum_rows(
            kvt_ref[0, hh * LANES + HEAD_DIM:(hh + 1) * LANES, pl.ds(pl.multiple_of(n * tq, tq), tq)])

    for hh in range(MOBA_HEADS):
        _online_softmax_step(*heads[hh], [(_mm(k_rows(hh, cur), qzs[hh]) + tb_ref[hh, 0], v_cols(hh, cur))])

    def past_blocks(blocks):
        for hh in range(MOBA_HEADS):
            tiles = []
            for n in blocks:
                onehot = jnp.where(key_lane == n, 1.0, 0.0).astype(BF16)
                s = _mm(jnp.concatenate([k_rows(hh, n), onehot], axis=1), qaugs[hh])
                tiles.append((s + tb_ref[hh, jnp.where(cur - n == 1, 1, 2)], v_cols(hh, n)))
            _online_softmax_step(*heads[hh], tiles)

    _unrolled_range(cur, MOBA_UNROLL, past_blocks)
    o_t = jnp.concatenate([_softmax_finish(acc[0]) for _, acc in heads], axis=0)
    o_ref[...] = o_t.T.astype(o_ref.dtype)


def _moba_prompt(qt, kvb, kvt, kmean, tb, b, s):
    d = qt.shape[1]
    nb = s // MOBA_BLOCK
    assert s % MOBA_BLOCK == 0 and MOBA_TOPK <= nb <= LANES
    nq = s // MOBA_BLOCK
    r = np.arange(MOBA_BLOCK)
    causal = np.zeros((3, MOBA_BLOCK, MOBA_BLOCK), np.float32)
    causal[0] = np.where(r[:, None] <= r[None, :], 0.0, -np.inf)
    tb = tb + jnp.asarray(causal)
    hps = MOBA_HEADS
    return pl.pallas_call(
        functools.partial(_moba_prompt_kernel, nb=nb),
        grid=(b, N_HEADS // hps, nq),
        in_specs=[pl.BlockSpec((1, hps * HEAD_DIM, MOBA_BLOCK), lambda i, h, t: (i, h, t)),
                  pl.BlockSpec((1, s, hps * LANES), lambda i, h, t: (i, 0, h)),
                  pl.BlockSpec((1, hps * LANES, s), lambda i, h, t: (i, h, 0)),
                  pl.BlockSpec((1, nb, hps * LANES), lambda i, h, t: (i, 0, h)),
                  pl.BlockSpec((hps, 3, MOBA_BLOCK, MOBA_BLOCK), lambda i, h, t: (h, 0, 0, 0))],
        out_specs=pl.BlockSpec((MOBA_BLOCK, hps * HEAD_DIM), lambda i, h, t: (i * nq + t, h)),
        out_shape=jax.ShapeDtypeStruct((b * s, d), BF16),
        scratch_shapes=[pltpu.VMEM((1, 1, MOBA_BLOCK), F32),
                        pltpu.VMEM((1, HEAD_DIM + SUM_ROWS, MOBA_BLOCK), F32)] * hps,
        compiler_params=_cparams("parallel", "parallel", "arbitrary"),
        name="moba_prompt",
    )(qt, kvb.reshape(b, s, 2 * d), kvt, kmean.reshape(b, nb, 2 * d), tb)


def _gelu_tanh(x):
    return 0.5 * x * (1.0 + jnp.tanh(math.sqrt(2.0 / math.pi) * (x + 0.044715 * (x * x * x))))


def _compress_weights(cmp_pos, w1, b1, w2, k_gain0):
    half = CMP_BLOCK // 2
    z = jnp.zeros((half, HEAD_DIM, CMP_HIDDEN), F32)

    def first_layer(lo):
        wk, wv = w1[0, lo:lo + half], w1[1, lo:lo + half]
        top = jnp.concatenate([wk, z], axis=2)
        bot = jnp.concatenate([z, wv], axis=2)
        return jnp.concatenate([top, bot], axis=1).reshape(half * LANES, 2 * CMP_HIDDEN).astype(BF16)

    zz = jnp.zeros((CMP_HIDDEN, HEAD_DIM), F32)
    w2bd = jnp.concatenate([jnp.concatenate([w2[0], zz], axis=1),
                            jnp.concatenate([zz, w2[1]], axis=1)], axis=0).astype(BF16)
    pos_a = cmp_pos[:half].reshape(half, LANES).astype(F32)
    pos_b = cmp_pos[half:].reshape(half, LANES).astype(F32)
    return (pos_a, pos_b, first_layer(0), first_layer(half), b1.reshape(1, 2 * CMP_HIDDEN).astype(F32), w2bd,
            _interleave_gain(k_gain0, 1))


def _compress_tail(xa_s, xb_s, wa_ref, wb_ref, b1_ref, w2_ref, kg_ref):
    ha = _mm(xa_s[...], wa_ref[...])
    hb = _mm(xb_s[...], wb_ref[...])
    rows = ha.shape[0]
    h = ha + pltpu.roll(hb, rows - 1, 0) + b1_ref[...]
    out = _mm(_gelu_tanh(h).astype(BF16), w2_ref[...])
    lane = lax.broadcasted_iota(I32, out.shape, 1)
    is_k = lane < HEAD_DIM
    ss = jnp.sum(jnp.where(is_k, out * out, 0.0), axis=1, keepdims=True)
    kn = out * lax.rsqrt(ss * (1.0 / HEAD_DIM) + NORM_EPS) * kg_ref[...]
    return jnp.where(is_k, kn, out)


def _compress_prompt_kernel(*refs, nseg):
    x_refs = refs[:NSA_KV_HEADS]
    pa_ref, pb_ref, wa_ref, wb_ref, b1_ref, w2_ref, kg_ref, o_ref, ot_ref, xa_s, xb_s = refs[NSA_KV_HEADS:]
    half = CMP_BLOCK // 2
    for g in range(NSA_KV_HEADS):
        for p in range(half):
            v = x_refs[g][0, pl.ds(p, nseg, stride=CMP_STRIDE), :]
            xa_s[g * nseg:(g + 1) * nseg, p * LANES:(p + 1) * LANES] = (v + pa_ref[p:p + 1, :]).astype(BF16)
            xb_s[g * nseg:(g + 1) * nseg, p * LANES:(p + 1) * LANES] = (v + pb_ref[p:p + 1, :]).astype(BF16)
    res = _compress_tail(xa_s, xb_s, wa_ref, wb_ref, b1_ref, w2_ref, kg_ref)
    for g in range(NSA_KV_HEADS):
        rows = res[g * nseg:(g + 1) * nseg]
        o_ref[0, g] = rows.astype(BF16)
        ot_ref[0, g] = rows.T.astype(BF16)


def _compress_prompt(kv_cmp, cw, b, s):
    nseg = s // CMP_STRIDE
    gw = NSA_KV_HEADS * LANES
    kdim = (CMP_BLOCK // 2) * LANES
    const = lambda i: (0, 0)
    return pl.pallas_call(
        functools.partial(_compress_prompt_kernel, nseg=nseg),
        grid=(b,),
        in_specs=[pl.BlockSpec((1, s, LANES), functools.partial(lambda i, g: (i, 0, g), g=g))
                  for g in range(NSA_KV_HEADS)] + [pl.BlockSpec(w.shape, const) for w in cw],
        out_specs=[pl.BlockSpec((1, NSA_KV_HEADS, nseg, LANES), lambda i: (i, 0, 0, 0)),
                   pl.BlockSpec((1, NSA_KV_HEADS, LANES, nseg), lambda i: (i, 0, 0, 0))],
        out_shape=[jax.ShapeDtypeStruct((b, NSA_KV_HEADS, nseg, LANES), BF16),
                   jax.ShapeDtypeStruct((b, NSA_KV_HEADS, LANES, nseg), BF16)],
        scratch_shapes=[pltpu.VMEM((NSA_KV_HEADS * nseg, kdim), BF16)] * 2,
        compiler_params=_cparams("parallel"),
        name="compress_prompt",
    )(*([kv_cmp.reshape(b, s, gw)] * NSA_KV_HEADS), *cw)


CMP_PAGES = 4


def _compress_sample_kernel(pt_ref, *refs, nseg, nrow):
    x_refs = refs[:CMP_PAGES]
    (new_ref, pa_ref, pb_ref, wa_ref, wb_ref, b1_ref, w2_ref, kg_ref, o_ref,
     xa_s, xb_s, xt_s) = refs[CMP_PAGES:]
    j = pl.program_id(1)
    for pg in range(CMP_PAGES):
        for g in range(NSA_KV_HEADS):
            xt_s[pg, g] = x_refs[pg][0, g].T
    half = CMP_BLOCK // 2
    per_page = PAGE_SIZE // CMP_STRIDE
    per_step = CMP_PAGES * per_page
    tail = nrow - nseg

    @pl.when(j == 0)
    def _():
        first = lax.broadcasted_iota(I32, (tail, LANES), 0) == 0
        for g in range(NSA_KV_HEADS):
            rows = slice(g * nrow + nseg, (g + 1) * nrow)
            for p in range(half):
                v = jnp.zeros((tail, LANES), F32)
                if p == 0:
                    v = jnp.where(first, new_ref[0, :, g * LANES:(g + 1) * LANES], 0.0)
                xa_s[rows, p * LANES:(p + 1) * LANES] = (v + pa_ref[p:p + 1, :]).astype(BF16)
                xb_s[rows, p * LANES:(p + 1) * LANES] = (v + pb_ref[p:p + 1, :]).astype(BF16)

    for g in range(NSA_KV_HEADS):
        r0 = pl.multiple_of(g * nrow + j * per_step, 16)
        for p in range(half):
            v = jnp.concatenate([xt_s[pg, g, pl.ds(p, per_page, stride=CMP_STRIDE), :]
                                 for pg in range(CMP_PAGES)], axis=0)
            xa_s[pl.ds(r0, per_step), p * LANES:(p + 1) * LANES] = (v + pa_ref[p:p + 1, :]).astype(BF16)
            xb_s[pl.ds(r0, per_step), p * LANES:(p + 1) * LANES] = (v + pb_ref[p:p + 1, :]).astype(BF16)

    @pl.when(j == pl.num_programs(1) - 1)
    def _():
        res = _compress_tail(xa_s, xb_s, wa_ref, wb_ref, b1_ref, w2_ref, kg_ref)
        for g in range(NSA_KV_HEADS):
            o_ref[0, g] = res[g * nrow:(g + 1) * nrow]


def _compress_sample(cache, page_table, new_rows, cw, past):
    db, n_pages = page_table.shape
    assert n_pages % CMP_PAGES == 0 and (CMP_PAGES * PAGE_SIZE // CMP_STRIDE) % 16 == 0
    gw = NSA_KV_HEADS * LANES
    nseg = past // CMP_STRIDE
    nrow = _cmp_rows(past)
    kdim = (CMP_BLOCK // 2) * LANES
    const = lambda i, j, pt: (0, 0)
    grid_spec = pltpu.PrefetchScalarGridSpec(
        num_scalar_prefetch=1,
        grid=(db, n_pages // CMP_PAGES),
        in_specs=[pl.BlockSpec((1, NSA_KV_HEADS, LANES, PAGE_SIZE),
                               functools.partial(lambda i, j, pt, pg: (pt[i * n_pages + CMP_PAGES * j + pg], 0, 0, 0),
                                                 pg=pg))
                  for pg in range(CMP_PAGES)]
                 + [pl.BlockSpec((1, 1, gw), lambda i, j, pt: (i, 0, 0))]
                 + [pl.BlockSpec(w.shape, const) for w in cw],
        out_specs=pl.BlockSpec((1, NSA_KV_HEADS, nrow, LANES), lambda i, j, pt: (i, 0, 0, 0)),
        scratch_shapes=[pltpu.VMEM((NSA_KV_HEADS * nrow, kdim), BF16)] * 2
                       + [pltpu.VMEM((CMP_PAGES, NSA_KV_HEADS, PAGE_SIZE, LANES), F32)],
    )
    cview = _slot_minor(cache, NSA_KV_HEADS)
    return pl.pallas_call(
        functools.partial(_compress_sample_kernel, nseg=nseg, nrow=nrow),
        grid_spec=grid_spec,
        out_shape=jax.ShapeDtypeStruct((db, NSA_KV_HEADS, nrow, LANES), F32),
        compiler_params=_cparams("parallel", "arbitrary"),
        name="compress_sample",
    )(page_table.reshape(-1), *([cview] * CMP_PAGES), new_rows.reshape(db, 1, gw), *cw)


def _imp_matrix(n_sel, n_rows, n_cmp):
    ratio = SEL_BLOCK // CMP_STRIDE
    lead = CMP_BLOCK // CMP_STRIDE - 1
    j = np.arange(n_sel)[:, None]
    n = np.arange(n_rows)[None, :]
    return ((n >= ratio * j - lead) & (n <= ratio * j + ratio - 1) & (n < n_cmp))


def _nsa_prompt_kernel(qt_ref, gt_ref, kc_ref, kct_ref, ks_ref, kst_ref, kw_ref, kwt_ref, tsel_ref, tw_ref,
                       tc_ref, at_ref, o_ref, m_s, acc_s, *, n_sel, ncp):
    t = pl.program_id(2)
    tq, tk, grp = NSA_TQ, NSA_TK, NSA_GROUP
    cols = grp * tq
    p0 = t * tq
    scale = jnp.asarray(ATTN_SCALE, BF16)
    qt = jnp.concatenate([qt_ref[0, p * HEAD_DIM:(p + 1) * HEAD_DIM, :] * scale for p in range(grp)], axis=1)
    qz = jnp.concatenate([qt, jnp.zeros_like(qt)], axis=0)
    qoff = jnp.bitwise_and(lax.broadcasted_iota(I32, (1, cols), 1), tq - 1)

    shift = lax.rem(t * (tq // CMP_STRIDE) - CMP_NEAR // 2 + ncp, ncp)
    bias_c = tc_ref[0, pl.ds(pl.multiple_of(ncp - shift, 8), ncp), :]
    c_end = lax.broadcasted_iota(I32, (ncp, cols), 0) * CMP_STRIDE + (CMP_BLOCK - 1)
    s_c = jnp.where(p0 + qoff >= c_end, _mm(kc_ref[0, 0], qz) + bias_c, -jnp.inf)
    m_c = jnp.max(s_c, axis=0, keepdims=True)
    e_c = jnp.exp(s_c - jnp.where(m_c == -jnp.inf, 0.0, m_c))
    p_c = e_c * (1.0 / jnp.maximum(jnp.sum(e_c, axis=0, keepdims=True), 1e-30))
    o_c = _mm(kct_ref[0, 0, HEAD_DIM:, :], p_c.astype(BF16))
    p_sum = p_c[:, 0:tq]
    for p in range(1, grp):
        p_sum = p_sum + p_c[:, p * tq:(p + 1) * tq]
    at = at_ref[...]
    imp = sum(_mm(at, part) for part in _split3(p_sum))

    blk = lax.broadcasted_iota(I32, (n_sel, tq), 0)
    cur = jnp.right_shift(p0 + lax.broadcasted_iota(I32, (n_sel, tq), 1), SEL_SHIFT)
    valid = blk <= cur
    forced = (blk == 0) | (blk >= cur - (N_LOCAL_SEL - 1))
    score = jnp.where(valid, jnp.where(forced, jnp.inf, imp), -jnp.inf)
    state = [jnp.full((n_sel, tq), MASK_NEG, F32)]

    def pick(hit, m):
        state[0] = jnp.where(hit, jnp.where(m > -jnp.inf, 0.0, state[0]), state[0])

    _topk_rows(score, SEL_TOPK, pick)
    negt = state[0]
    if n_sel < LANES:
        negt = jnp.concatenate([negt, jnp.zeros((LANES - n_sel, tq), F32)], axis=0)
    negt = negt.astype(BF16)
    qaug = jnp.concatenate([qz, jnp.concatenate([negt] * grp, axis=1)], axis=0)

    kj = lax.broadcasted_iota(I32, (tk, LANES), 0)
    kl = lax.broadcasted_iota(I32, (tk, LANES), 1)

    def sel_tile(kt, variant):
        at_kt = pl.ds(pl.multiple_of(kt * tk, tk), tk)
        onehot = jnp.where(kl == kt * (tk // SEL_BLOCK) + jnp.right_shift(kj, SEL_SHIFT), 1.0, 0.0).astype(BF16)
        s = _mm(jnp.concatenate([ks_ref[0, at_kt, :], onehot], axis=1), qaug) + tsel_ref[0, variant]
        return s, _with_sum_rows(kst_ref[0, HEAD_DIM:, at_kt])

    _softmax_state_init(m_s, acc_s)
    kt_d = t // (tk // tq)
    _online_softmax_step(m_s, acc_s, [sel_tile(kt_d, lax.rem(t, tk // tq))])

    def past_tiles(kts):
        _online_softmax_step(m_s, acc_s, [sel_tile(kt, jnp.minimum((p0 - kt * tk) // tq, 3)) for kt in kts])

    _unrolled_range(kt_d, NSA_UNROLL, past_tiles)
    o_s = _softmax_finish(acc_s[0])

    at_w = pl.ds(pl.multiple_of(p0, tq), WIN_KEYS)
    wj = lax.broadcasted_iota(I32, (WIN_KEYS, cols), 0)
    s_w = jnp.where(p0 + wj >= WINDOW, _mm(kw_ref[0, at_w, :], qz) + tw_ref[0], -jnp.inf)
    m_w = jnp.max(s_w, axis=0, keepdims=True)
    e_w = jnp.exp(s_w - jnp.where(m_w == -jnp.inf, 0.0, m_w))
    o_w = _softmax_finish(_mm(_with_sum_rows(kwt_ref[0, HEAD_DIM:, at_w]), e_w.astype(BF16)))

    gt = gt_ref[0]
    per_head = []
    for p in range(grp):
        c = slice(p * tq, (p + 1) * tq)
        g = [gt[N_BRANCH * p + k:N_BRANCH * p + k + 1, :] for k in range(N_BRANCH)]
        per_head.append(g[0] * o_c[:, c] + g[1] * o_s[:, c] + g[2] * o_w[:, c])
    o_ref[...] = jnp.concatenate(per_head, axis=0).T.astype(o_ref.dtype)


def _heads_on_lanes(tab, lead):
    n = len(lead)
    x = tab.reshape((NSA_KV_HEADS, NSA_GROUP) + tab.shape[1:])
    perm = (0,) + tuple(range(2, 3 + n)) + (1, 3 + n)
    x = x.transpose(perm)
    return x.reshape(x.shape[:2 + n] + (NSA_GROUP * tab.shape[-1],))


def _nsa_prompt(qt, gt, kcvc, kcvc_t, selb, selb_t, winb, winb_t, tabs, b, s):
    d = qt.shape[1]
    gw = NSA_KV_HEADS * LANES
    ncp = s // CMP_STRIDE
    n_sel = s // SEL_BLOCK
    assert s % NSA_TK == 0 and SEL_TOPK <= n_sel <= LANES and ncp >= CMP_NEAR
    nq = s // NSA_TQ
    at = jnp.asarray(_imp_matrix(n_sel, ncp, ncp - 1), BF16)
    winp = jnp.pad(winb.reshape(b, s, gw), ((0, 0), (WINDOW, 0), (0, 0)))
    winp_t = jnp.pad(winb_t, ((0, 0), (0, 0), (WINDOW, 0)))
    gq = NSA_GROUP
    cols = gq * NSA_TQ
    kr = np.arange(NSA_TK)[:, None]
    wr = np.arange(WIN_KEYS)[:, None]
    qc = np.arange(cols)[None, :] % NSA_TQ
    sel_mask = np.zeros((4, NSA_TK, cols), np.float32)
    for v in range(NSA_TK // NSA_TQ):
        sel_mask[v] = np.where(kr <= qc + v * NSA_TQ, 0.0, -np.inf)
    win_mask = np.where((wr > qc) & (wr <= qc + WINDOW), 0.0, -np.inf).astype(np.float32)
    tsel = _heads_on_lanes(tabs["sel"], (4,)) + jnp.asarray(sel_mask)
    tw = _heads_on_lanes(tabs["win"], ()) + jnp.asarray(win_mask)
    tc = jnp.tile(_heads_on_lanes(tabs["cmp"], ()), (1, 2, 1))
    return pl.pallas_call(
        functools.partial(_nsa_prompt_kernel, n_sel=n_sel, ncp=ncp),
        grid=(b, NSA_KV_HEADS, nq),
        in_specs=[pl.BlockSpec((1, gq * HEAD_DIM, NSA_TQ), lambda i, g, t: (i, g, t)),
                  pl.BlockSpec((1, LANES, NSA_TQ), lambda i, g, t: (i, g, t)),
                  pl.BlockSpec((1, 1, ncp, LANES), lambda i, g, t: (i, g, 0, 0)),
                  pl.BlockSpec((1, 1, LANES, ncp), lambda i, g, t: (i, g, 0, 0)),
                  pl.BlockSpec((1, s, LANES), lambda i, g, t: (i, 0, g)),
                  pl.BlockSpec((1, LANES, s), lambda i, g, t: (i, g, 0)),
                  pl.BlockSpec((1, s + WINDOW, LANES), lambda i, g, t: (i, 0, g)),
                  pl.BlockSpec((1, LANES, s + WINDOW), lambda i, g, t: (i, g, 0)),
                  pl.BlockSpec((1, 4, NSA_TK, cols), lambda i, g, t: (g, 0, 0, 0)),
                  pl.BlockSpec((1, WIN_KEYS, cols), lambda i, g, t: (g, 0, 0)),
                  pl.BlockSpec((1, 2 * ncp, cols), lambda i, g, t: (g, 0, 0)),
                  pl.BlockSpec(at.shape, lambda i, g, t: (0, 0))],
        out_specs=pl.BlockSpec((NSA_TQ, gq * HEAD_DIM), lambda i, g, t: (i * nq + t, g)),
        out_shape=jax.ShapeDtypeStruct((b * s, d), BF16),
        scratch_shapes=[pltpu.VMEM((1, 1, cols), F32), pltpu.VMEM((1, HEAD_DIM + SUM_ROWS, cols), F32)],
        compiler_params=_cparams("parallel", "parallel", "arbitrary"),
        name="nsa_prompt",
    )(qt, gt, kcvc, kcvc_t, selb.reshape(b, s, gw), selb_t, winp, winp_t, tsel, tw, tc, at)


def _topk_lanes(x, k):
    ax = x.ndim - 1
    lane = lax.broadcasted_iota(I32, x.shape, ax)
    oshape = x.shape[:-1] + (LANES,)
    out_lane = lax.broadcasted_iota(I32, oshape, ax)
    ids = jnp.zeros(oshape, I32)
    for r in range(k):
        m = jnp.max(x, axis=ax, keepdims=True)
        idx = jnp.min(jnp.where(x == m, lane, x.shape[ax]), axis=ax, keepdims=True)
        ids = jnp.where(out_lane == r, idx, ids)
        x = jnp.where(lane == idx, -jnp.inf, x)
    return ids


def _slot_minor(cache, n_kv):
    pages, slots = cache.shape[:2]
    return jnp.transpose(cache, (0, 2, 3, 4, 1)).reshape(pages, n_kv, LANES, slots)


MOBA_SELECT_PAGES = 8


def _moba_select_kernel(pt_ref, *refs, nb):
    x_refs = refs[:MOBA_SELECT_PAGES]
    q_ref, ids_ref, ksum_s = refs[MOBA_SELECT_PAGES:]
    j = pl.program_id(1)
    ppb = MOBA_BLOCK // PAGE_SIZE
    per_step = MOBA_SELECT_PAGES // ppb

    @pl.when(j == 0)
    def _():
        ksum_s[...] = jnp.zeros_like(ksum_s)

    lane = lax.broadcasted_iota(I32, ksum_s.shape, 2)
    acc = ksum_s[...]
    for bi in range(per_step):
        x = x_refs[bi * ppb][0]
        for pg in range(1, ppb):
            x = x + x_refs[bi * ppb + pg][0]
        col = jnp.sum(x, axis=2, keepdims=True)
        acc = acc + jnp.where(lane == j * per_step + bi, col, 0.0)
    ksum_s[...] = acc

    @pl.when(j == pl.num_programs(1) - 1)
    def _():
        km = ksum_s[...] * (1.0 / MOBA_BLOCK)
        gate = sum(jnp.einsum("hqd,hdn->hqn", a, b_, preferred_element_type=F32)
                   for a in _split3(q_ref[0]) for b_ in _split3(km))
        blk = lax.broadcasted_iota(I32, gate.shape, 2)
        ids_ref[0] = _topk_lanes(jnp.where(blk < nb, gate, -jnp.inf), MOBA_TOPK)


def _moba_select(cache_t, page_table, q, past):
    db, n_pages = page_table.shape
    nb = past // MOBA_BLOCK
    assert past % MOBA_BLOCK == 0 and MOBA_TOPK <= nb <= LANES and n_pages % MOBA_SELECT_PAGES == 0
    q8 = jnp.pad(q.reshape(db, N_HEADS, 1, HEAD_DIM), ((0, 0), (0, 0), (0, 7), (0, 0)))
    grid_spec = pltpu.PrefetchScalarGridSpec(
        num_scalar_prefetch=1,
        grid=(db, n_pages // MOBA_SELECT_PAGES),
        in_specs=[pl.BlockSpec((1, N_HEADS, HEAD_DIM, PAGE_SIZE),
                               functools.partial(lambda i, j, pt, pg: (pt[i * n_pages + MOBA_SELECT_PAGES * j + pg],
                                                                       0, 0, 0), pg=pg))
                  for pg in range(MOBA_SELECT_PAGES)]
                 + [pl.BlockSpec((1, N_HEADS, 8, HEAD_DIM), lambda i, j, pt: (i, 0, 0, 0))],
        out_specs=pl.BlockSpec((1, N_HEADS, 8, LANES), lambda i, j, pt: (i, 0, 0, 0)),
        scratch_shapes=[pltpu.VMEM((N_HEADS, HEAD_DIM, LANES), F32)],
    )
    ids = pl.pallas_call(
        functools.partial(_moba_select_kernel, nb=nb),
        grid_spec=grid_spec,
        out_shape=jax.ShapeDtypeStruct((db, N_HEADS, 8, LANES), I32),
        compiler_params=_cparams("parallel", "arbitrary"),
        name="moba_select",
    )(page_table.reshape(-1), *([cache_t] * MOBA_SELECT_PAGES), q8)
    return ids[:, :, 0, :MOBA_TOPK]


def _pad_rows(x, rows):
    return jnp.concatenate([x, jnp.zeros((rows - x.shape[0],) + x.shape[1:], x.dtype)], axis=0)


MOBA_SAMPLE_HEADS = 4


def _moba_sample_kernel(pt_ref, ids_ref, q_ref, new_ref, *rest, n_tiles, last_blk, pages_per_block):
    tiles = rest[:MOBA_SAMPLE_HEADS * n_tiles]
    tab_ref, o_ref = rest[MOBA_SAMPLE_HEADS * n_tiles:]
    i, hg = pl.program_id(0), pl.program_id(1)
    for hh in range(MOBA_SAMPLE_HEADS):
        h = hg * MOBA_SAMPLE_HEADS + hh
        q8 = q_ref[0, hh] * ATTN_SCALE
        q8b = q8.astype(BF16)
        new = new_ref[0, hh]
        k_new, v_new = new[:, :HEAD_DIM], new[:, HEAD_DIM:]
        scores, vts = [], []
        for ti in range(n_tiles):
            k, pp = divmod(ti, pages_per_block)
            blk = ids_ref[(i * N_HEADS + h) * MOBA_TOPK + k]
            near = jnp.logical_and(blk == last_blk, pp == pages_per_block - 1)
            tile = tiles[hh * n_tiles + ti][0, 0]
            vts.append(tile[HEAD_DIM:].astype(BF16))
            scores.append(_mm(q8b, tile[:HEAD_DIM].astype(BF16))
                          + jnp.where(near, tab_ref[hh, 0:1, :], tab_ref[hh, 1:2, :]))
        s = jnp.concatenate(scores, axis=1)
        s_self = jnp.sum(q8 * k_new, axis=1, keepdims=True) + tab_ref[hh, 2:3, 0:1]
        m = jnp.maximum(jnp.max(s, axis=1, keepdims=True), s_self)
        p = jnp.exp(s - m)
        p_self = jnp.exp(s_self - m)
        l = jnp.sum(p, axis=1, keepdims=True) + p_self
        o = p_self * v_new
        for ti in range(n_tiles):
            o = o + _nt(p[:, ti * PAGE_SIZE:(ti + 1) * PAGE_SIZE].astype(BF16), vts[ti])
        o_ref[0, hh] = o / jnp.maximum(l, 1e-30)


def _moba_sample(cache_t, page_table, ids, q, kv_new, tab_ms, past):
    db, n_pages = page_table.shape
    ppb = MOBA_BLOCK // PAGE_SIZE
    n_tiles = MOBA_TOPK * ppb

    hps = MOBA_SAMPLE_HEADS

    def tile_spec(hh, ti):
        k, pp = divmod(ti, ppb)

        def imap(i, hg, pt, sel):
            h = hg * hps + hh
            return (pt[i * n_pages + sel[(i * N_HEADS + h) * MOBA_TOPK + k] * ppb + pp], h, 0, 0)
        return pl.BlockSpec((1, 1, LANES, PAGE_SIZE), imap)

    q8 = jnp.pad(q.reshape(db, N_HEADS, 1, HEAD_DIM), ((0, 0), (0, 0), (0, 7), (0, 0)))
    grid_spec = pltpu.PrefetchScalarGridSpec(
        num_scalar_prefetch=2,
        grid=(db, N_HEADS // hps),
        in_specs=[pl.BlockSpec((1, hps, 8, HEAD_DIM), lambda i, h, pt, sel: (i, h, 0, 0)),
                  pl.BlockSpec((1, hps, 1, LANES), lambda i, h, pt, sel: (i, h, 0, 0))]
                 + [tile_spec(hh, ti) for hh in range(hps) for ti in range(n_tiles)]
                 + [pl.BlockSpec((hps, 3, LANES), lambda i, h, pt, sel: (h, 0, 0))],
        out_specs=pl.BlockSpec((1, hps, 8, HEAD_DIM), lambda i, h, pt, sel: (i, h, 0, 0)),
    )
    out = pl.pallas_call(
        functools.partial(_moba_sample_kernel, n_tiles=n_tiles, last_blk=past // MOBA_BLOCK - 1,
                          pages_per_block=ppb),
        grid_spec=grid_spec,
        out_shape=jax.ShapeDtypeStruct((db, N_HEADS, 8, HEAD_DIM), F32),
        compiler_params=_cparams("parallel", "arbitrary"),
        name="moba_sample",
    )(page_table.reshape(-1), ids.reshape(-1), q8, kv_new.reshape(db, N_HEADS, 1, LANES),
      *([cache_t] * (hps * n_tiles)), tab_ms)
    return out[:, :, 0, :].reshape(db, N_HEADS * HEAD_DIM)


def _nsa_select_kernel(q_ref, kc_ref, tc_ref, a_ref, oc_ref, ids_ref, *, n_valid, n_sel, cur):
    nrow = kc_ref.shape[2]
    lane = lax.broadcasted_iota(I32, (8, nrow), 1)
    psums = []
    for g in range(NSA_KV_HEADS):
        q8 = (q_ref[0, g] * ATTN_SCALE).astype(BF16)
        kc = kc_ref[0, g].astype(BF16)
        s = jnp.where(lane < n_valid, _nt(q8, kc) + tc_ref[g], -jnp.inf)
        p_c = _softmax_rows(s)
        oc_ref[0, g] = _mm(p_c.astype(BF16), kc)[:, HEAD_DIM:]
        psums.append(jnp.sum(p_c[0:NSA_GROUP], axis=0, keepdims=True))
    p_sum = _pad_rows(jnp.concatenate(psums, axis=0), 8)
    a = a_ref[...]
    imp = sum(_mm(part, a) for part in _split3(p_sum))
    blk = lax.broadcasted_iota(I32, imp.shape, 1)
    valid = blk <= min(cur, n_sel - 1)
    forced = (blk == 0) | (blk >= cur - (N_LOCAL_SEL - 1))
    score = jnp.where(valid, jnp.where(forced, jnp.inf, imp), -jnp.inf)
    ids_ref[0] = _topk_lanes(score, SEL_TOPK)


def _nsa_select(q4, kcvc, tab_cs, past):
    db = q4.shape[0]
    nrow = kcvc.shape[2]
    n_cmp_valid = (past - (CMP_BLOCK - 1)) // CMP_STRIDE + 1
    n_sel = max(-(-(past + 1) // SEL_BLOCK), SEL_TOPK)
    n_cmp = n_sel * SEL_BLOCK // CMP_STRIDE - CMP_BLOCK // CMP_STRIDE + 1
    cur = past // SEL_BLOCK
    assert n_sel <= 2 * LANES and cur >= SEL_TOPK - 1 and n_cmp <= nrow
    a = jnp.asarray(_imp_matrix(2 * LANES, nrow, n_cmp).T, BF16)
    tc = jnp.concatenate([tab_cs.reshape(NSA_KV_HEADS, NSA_GROUP, nrow)] * 2, axis=1)
    return pl.pallas_call(
        functools.partial(_nsa_select_kernel, n_valid=n_cmp_valid, n_sel=n_sel, cur=cur),
        grid=(db,),
        in_specs=[pl.BlockSpec((1, NSA_KV_HEADS, 8, LANES), lambda i: (i, 0, 0, 0)),
                  pl.BlockSpec((1, NSA_KV_HEADS, nrow, LANES), lambda i: (i, 0, 0, 0)),
                  pl.BlockSpec(tc.shape, lambda i: (0, 0, 0)),
                  pl.BlockSpec(a.shape, lambda i: (0, 0))],
        out_specs=[pl.BlockSpec((1, NSA_KV_HEADS, 8, HEAD_DIM), lambda i: (i, 0, 0, 0)),
                   pl.BlockSpec((1, 8, LANES), lambda i: (i, 0, 0))],
        out_shape=[jax.ShapeDtypeStruct((db, NSA_KV_HEADS, 8, HEAD_DIM), F32),
                   jax.ShapeDtypeStruct((db, 8, LANES), I32)],
        compiler_params=_cparams("parallel"),
        name="nsa_select",
    )(q4, kcvc, tc, a)


def _nsa_sample_kernel(pt_ref, ids_ref, q_ref, oc_ref, gate_ref, snew_ref, wnew_ref, st_ref, *rest,
                       n_cache_blk):
    tiles = rest[:SEL_TOPK]
    tms_ref, tws_ref, eye_ref, o_ref, st_out = rest[SEL_TOPK:]
    i, g = pl.program_id(0), pl.program_id(1)
    q8 = q_ref[0, 0] * ATTN_SCALE
    q8b = q8.astype(BF16)
    halves = PAGE_SIZE // SEL_BLOCK
    last_page = n_cache_blk // halves - 1
    half_of_lane = jnp.right_shift(lax.broadcasted_iota(I32, (8, PAGE_SIZE), 1), SEL_SHIFT)
    t0 = tms_ref[0, 2][:, 0:1]

    scores, vts = [], []
    for k in range(SEL_TOPK):
        blk = ids_ref[(i * NSA_KV_HEADS + g) * SEL_TOPK + k]
        in_cache = blk < n_cache_blk
        half = jnp.where(in_cache, lax.rem(blk, halves), -1)
        near = jnp.logical_and(in_cache, blk // halves == last_page)
        tile = tiles[k][0, 0]
        vts.append(tile[HEAD_DIM:].astype(BF16))
        s = _mm(q8b, tile[:HEAD_DIM].astype(BF16)) + jnp.where(near, tms_ref[0, 0], tms_ref[0, 1])
        scores.append(jnp.where(half_of_lane == half, s, -jnp.inf))
    s = jnp.concatenate(scores, axis=1)
    snew = snew_ref[0]
    s_self = jnp.sum(q8 * snew[:, :HEAD_DIM], axis=1, keepdims=True) + t0
    m = jnp.maximum(jnp.max(s, axis=1, keepdims=True), s_self)
    p = jnp.exp(s - m)
    p_self = jnp.exp(s_self - m)
    l = jnp.sum(p, axis=1, keepdims=True) + p_self
    o_s = p_self * snew[:, HEAD_DIM:]
    for k in range(SEL_TOPK):
        o_s = o_s + _nt(p[:, k * PAGE_SIZE:(k + 1) * PAGE_SIZE].astype(BF16), vts[k])
    o_s = o_s / jnp.maximum(l, 1e-30)

    ws = st_ref[0, 0]
    wl = lax.broadcasted_iota(I32, (8, WINDOW), 1)
    s_w = jnp.where(wl >= 1, _mm(q8b, ws[:HEAD_DIM].astype(BF16)) + tws_ref[0], -jnp.inf)
    wnew = wnew_ref[0]
    w_self = jnp.sum(q8 * wnew[:, :HEAD_DIM], axis=1, keepdims=True) + t0
    m = jnp.maximum(jnp.max(s_w, axis=1, keepdims=True), w_self)
    p = jnp.exp(s_w - m)
    p_self = jnp.exp(w_self - m)
    l = jnp.sum(p, axis=1, keepdims=True) + p_self
    o_w = (_nt(p.astype(BF16), ws[HEAD_DIM:].astype(BF16)) + p_self * wnew[:, HEAD_DIM:]) / jnp.maximum(l, 1e-30)

    gate = gate_ref[0, 0]
    o_ref[0, 0] = gate[:, 0:1] * oc_ref[0, 0] + gate[:, 1:2] * o_s + gate[:, 2:3] * o_w
    new_col = sum(_nt(eye_ref[...], _pad_rows(part, 8)) for part in _split3(wnew))[:, 0:1]
    wcol = lax.broadcasted_iota(I32, ws.shape, 1)
    st_out[0, 0] = jnp.where(wcol == WINDOW - 1, new_col, pltpu.roll(ws, WINDOW - 1, 1))


def _nsa_sample(cache_t, state_t, page_table, ids, q4, o_c, gates, sel_new, win_new, tabs, past):
    db, n_pages = page_table.shape
    gw = NSA_KV_HEADS * LANES
    halves = PAGE_SIZE // SEL_BLOCK
    n_cache_blk = past // SEL_BLOCK
    assert state_t.shape[-1] == WINDOW and past >= WINDOW and past % PAGE_SIZE == 0

    def tile_spec(k):
        def imap(i, g, pt, sel):
            blk = jnp.minimum(sel[(i * NSA_KV_HEADS + g) * SEL_TOPK + k], n_cache_blk - 1)
            return (pt[i * n_pages + blk // halves], g, 0, 0)
        return pl.BlockSpec((1, 1, LANES, PAGE_SIZE), imap)

    grp = lambda n: pl.BlockSpec((1, 1, 8, n), lambda i, g, pt, sel: (i, g, 0, 0))
    new = pl.BlockSpec((1, 1, LANES), lambda i, g, pt, sel: (i, 0, g))
    st = pl.BlockSpec((1, 1, LANES, WINDOW), lambda i, g, pt, sel: (i, g, 0, 0))
    pad8 = lambda x: jnp.concatenate([x, jnp.zeros_like(x)], axis=-2)
    tms = pad8(tabs["ms"].reshape(NSA_KV_HEADS, NSA_GROUP, 3, LANES).transpose(0, 2, 1, 3))
    tws = pad8(tabs["ws"].reshape(NSA_KV_HEADS, NSA_GROUP, WINDOW))
    eye = jnp.asarray(np.eye(LANES), BF16)
    grid_spec = pltpu.PrefetchScalarGridSpec(
        num_scalar_prefetch=2,
        grid=(db, NSA_KV_HEADS),
        in_specs=[grp(HEAD_DIM), grp(HEAD_DIM), grp(LANES), new, new, st] + [tile_spec(k) for k in range(SEL_TOPK)]
                 + [pl.BlockSpec((1, 3, 8, LANES), lambda i, g, pt, sel: (g, 0, 0, 0)),
                    pl.BlockSpec((1, 8, WINDOW), lambda i, g, pt, sel: (g, 0, 0)),
                    pl.BlockSpec(eye.shape, lambda i, g, pt, sel: (0, 0))],
        out_specs=[grp(HEAD_DIM), st],
    )
    o, st_new = pl.pallas_call(
        functools.partial(_nsa_sample_kernel, n_cache_blk=n_cache_blk),
        grid_spec=grid_spec,
        out_shape=[jax.ShapeDtypeStruct((db, NSA_KV_HEADS, 8, HEAD_DIM), F32),
                   jax.ShapeDtypeStruct(state_t.shape, F32)],
        compiler_params=_cparams("parallel", "arbitrary"),
        name="nsa_sample",
    )(page_table.reshape(-1), ids[:, :NSA_KV_HEADS, :SEL_TOPK].reshape(-1), q4, o_c, gates,
      sel_new.reshape(db, 1, gw), win_new.reshape(db, 1, gw), state_t,
      *([cache_t] * SEL_TOPK), tms, tws, eye)
    return o[:, :, :NSA_GROUP, :].reshape(db, N_HEADS * HEAD_DIM), st_new


def kernel(x_prompt, x_sample, cache_moba_kv, cache_nsa_cmp_kv, cache_nsa_sel_kv, state_nsa_win_kv, page_table,
           rel_bias, attn_norm, ffn_norm, moba_w_qkv, moba_q_norm, moba_k_norm, moba_w_o, nsa_w_in,
           nsa_gate_bias, nsa_q_norm, nsa_k_norm, nsa_cmp_pos, nsa_cmp_w1, nsa_cmp_b1, nsa_cmp_w2, nsa_w_o,
           ffn_w_up, ffn_w_down):
    b, s, d = x_prompt.shape
    db = x_sample.shape[0]
    assert x_sample.shape[1] == 1 and d == N_HEADS * HEAD_DIM
    past = page_table.shape[1] * PAGE_SIZE
    depth = attn_norm.shape[0]
    tabs = _bias_tables(rel_bias, s, past)
    hp = x_prompt.reshape(b * s, d)
    hs = x_sample.reshape(db, d)
    outs = {k: [] for k in ("moba_p", "moba_s", "cmp_p", "cmp_s", "sel_p", "sel_s", "win_p", "win_s")}
    kvshape = lambda n, g: (n, -1, g, 2, HEAD_DIM)
    for i in range(depth):
        j = i // 2
        if i % 2 == 0:
            qp_t, kvp_t, kvpb, kmean, kvpb_t = _proj_moba(hp, attn_norm[i], moba_w_qkv[j], moba_q_norm[j],
                                                          moba_k_norm[j], qdt=BF16, seq=s)
            qs, kvs, _ = _proj_moba(hs, attn_norm[i], moba_w_qkv[j], moba_q_norm[j], moba_k_norm[j], qdt=F32)
            op = _moba_prompt(qp_t, kvpb, kvpb_t, kmean, tabs["moba"], b, s)
            cache_t = _slot_minor(cache_moba_kv[j], N_HEADS)
            ids = _moba_select(cache_t, page_table, qs, past)
            osm = _moba_sample(cache_t, page_table, ids, qs, kvs, tabs["ms"], past)
            outs["moba_p"].append(_token_major(kvp_t, b, s, N_HEADS))
            outs["moba_s"].append(kvs.reshape(kvshape(db, N_HEADS)))
            w_o = moba_w_o[j]
        else:
            qp_t, cp, sp_t, wp_t, spb, wpb, gp_t, cp_t, spb_t, wpb_t = _proj_nsa(
                hp, attn_norm[i], nsa_w_in[j], nsa_gate_bias[j], nsa_q_norm[j], nsa_k_norm[j], qdt=BF16, seq=s)
            qs, cs, ss, ws, _, _, gs = _proj_nsa(hs, attn_norm[i], nsa_w_in[j], nsa_gate_bias[j],
                                                 nsa_q_norm[j], nsa_k_norm[j], qdt=F32)
            cw = _compress_weights(nsa_cmp_pos[j], nsa_cmp_w1[j], nsa_cmp_b1[j], nsa_cmp_w2[j], nsa_k_norm[j, 0])
            kcvc_p, kcvc_pt = _compress_prompt(cp, cw, b, s)
            op = _nsa_prompt(qp_t, gp_t, kcvc_p, kcvc_pt, spb, spb_t, wpb, wpb_t, tabs, b, s)
            kcvc_s = _compress_sample(cache_nsa_cmp_kv[j], page_table, cs, cw, past)
            pad8 = lambda x: jnp.concatenate([x, jnp.zeros_like(x)], axis=2)
            q4 = pad8(qs.reshape(db, NSA_KV_HEADS, NSA_GROUP, HEAD_DIM))
            q4w = jnp.pad(q4, ((0, 0), (0, 0), (0, 0), (0, LANES - HEAD_DIM)))
            g4 = pad8(gs.reshape(db, NSA_KV_HEADS, LANES)[:, :, :NSA_GROUP * N_BRANCH]
                      .reshape(db, NSA_KV_HEADS, NSA_GROUP, N_BRANCH))
            g4 = jnp.pad(g4, ((0, 0), (0, 0), (0, 0), (0, LANES - N_BRANCH)))
            o_c, sel_ids = _nsa_select(q4w, kcvc_s, tabs["cs"], past)
            osm, st_t = _nsa_sample(_slot_minor(cache_nsa_sel_kv[j], NSA_KV_HEADS),
                                    _slot_minor(state_nsa_win_kv[j], NSA_KV_HEADS), page_table, sel_ids, q4, o_c,
                                    g4, ss, ws, tabs, past)
            ws_new = st_t.reshape(db, NSA_KV_HEADS, 2, HEAD_DIM, WINDOW).transpose(0, 4, 1, 2, 3)
            gsh = (NSA_KV_HEADS, 2, HEAD_DIM)
            outs["cmp_p"].append(_token_major(cp_t, b, s, NSA_KV_HEADS))
            outs["cmp_s"].append(cs.reshape((db, 1) + gsh))
            outs["sel_p"].append(_token_major(sp_t, b, s, NSA_KV_HEADS))
            outs["sel_s"].append(ss.reshape((db, 1) + gsh))
            outs["win_p"].append(_token_major(wp_t, b, s, NSA_KV_HEADS)[:, s - min(WINDOW, s):])
            outs["win_s"].append(ws_new.reshape((db, WINDOW) + gsh))
            w_o = nsa_w_o[j]
        hp = _attn_out_mlp(hp, op, w_o, ffn_norm[i], ffn_w_up[i], ffn_w_down[i])
        hs = _attn_out_mlp(hs, osm, w_o, ffn_norm[i], ffn_w_up[i], ffn_w_down[i])
    stack = lambda k: jnp.stack(outs[k])
    return (hp.reshape(b, s, d), hs.reshape(db, 1, d), stack("moba_p"), stack("moba_s"), stack("cmp_p"),
            stack("cmp_s"), stack("sel_p"), stack("sel_s"), stack("win_p"), stack("win_s"))
```

```python
import functools
import math

import numpy as np
import jax
import jax.numpy as jnp
from jax import lax
from jax.experimental import pallas as pl
from jax.experimental.pallas import tpu as pltpu

F32 = jnp.float32
BF16 = jnp.bfloat16
I32 = jnp.int32

N_HEADS = 16
HEAD_DIM = 64
NORM_EPS = 1e-6
ATTN_SCALE = HEAD_DIM ** -0.5
REL_BUCKETS = 32
REL_MAX_DIST = 128
PAGE_SIZE = 128
MOBA_BLOCK = 256
MOBA_TOPK = 3
NSA_KV_HEADS = 4
NSA_GROUP = N_HEADS // NSA_KV_HEADS
CMP_BLOCK = 32
CMP_STRIDE = 16
CMP_HIDDEN = 2 * HEAD_DIM
SEL_BLOCK = 64
SEL_TOPK = 16
N_LOCAL_SEL = 2
WINDOW = 512
N_BRANCH = 3

LANES = 128
LANE_SHIFT = 7
SEL_SHIFT = 6
MASK_NEG = -(2.0 ** 100)
VMEM_LIMIT = 56 * 1024 * 1024
MOBA_UNROLL = 4
MOBA_HEADS = 4
NSA_UNROLL = 4
NSA_TQ = 256
NSA_TK = 256
WIN_KEYS = WINDOW + NSA_TQ
CMP_NEAR = 32


def _cparams(*sem):
    return pltpu.CompilerParams(dimension_semantics=sem, vmem_limit_bytes=VMEM_LIMIT)


def _nt(a, b):
    return lax.dot_general(a, b, (((1,), (1,)), ((), ())), preferred_element_type=F32)


def _mm(a, b):
    return jnp.dot(a, b, preferred_element_type=F32)


def _split2(x):
    hi = x.astype(BF16)
    lo = (x - hi.astype(F32)).astype(BF16)
    return hi, lo


def _split3(x):
    hi = x.astype(BF16)
    r = x - hi.astype(F32)
    mid = r.astype(BF16)
    lo = (r - mid.astype(F32)).astype(BF16)
    return hi, mid, lo


def _rms_rows(x, g):
    return x * lax.rsqrt(jnp.mean(x * x, axis=-1, keepdims=True) + NORM_EPS) * g


def _group_norm(h, gmat_ref, gain, k_lanes_only):
    hi, lo = _split2(h * h)
    gm = gmat_ref[...]
    ss = _mm(hi, gm) + _mm(lo, gm)
    y = h * lax.rsqrt(ss * (1.0 / HEAD_DIM) + NORM_EPS) * gain
    if k_lanes_only:
        lane = lax.broadcasted_iota(I32, h.shape, 1)
        y = jnp.where((lane & (LANES - 1)) < HEAD_DIM, y, h)
    return y


def _softmax_rows(s):
    m = jnp.max(s, axis=-1, keepdims=True)
    m = jnp.where(m == -jnp.inf, 0.0, m)
    e = jnp.exp(s - m)
    return e * (1.0 / jnp.maximum(jnp.sum(e, axis=-1, keepdims=True), 1e-30))


def _bucket_of_dist():
    n = np.arange(REL_MAX_DIST + 1)
    max_exact = REL_BUCKETS // 2
    nf = np.maximum(n, 1).astype(np.float32)
    large = max_exact + (np.log(nf / np.float32(max_exact)) / np.float32(math.log(REL_MAX_DIST / max_exact))
                         * np.float32(REL_BUCKETS - max_exact)).astype(np.int32)
    large = np.minimum(large, REL_BUCKETS - 1)
    return np.where(n < max_exact, n, large).astype(np.int32)


def _dist_tables(seq, past):
    c = lambda d: np.clip(d, 0, REL_MAX_DIST)
    ncp = seq // CMP_STRIDE
    i256 = np.arange(MOBA_BLOCK)[:, None]
    j256 = np.arange(MOBA_BLOCK)[None, :]
    q128 = np.arange(NSA_TQ)[None, :]
    t = {}
    t["moba"] = np.stack([c(j256 - i256), c(MOBA_BLOCK + j256 - i256),
                          np.full((MOBA_BLOCK, MOBA_BLOCK), REL_MAX_DIST)])
    t["sel"] = np.stack([c(NSA_TQ * v + q128 - np.arange(NSA_TK)[:, None]) for v in range(4)])
    t["win"] = c(q128 + WINDOW - np.arange(WIN_KEYS)[:, None])
    mrow = np.arange(ncp)[:, None]
    near = q128 - CMP_STRIDE * (mrow - CMP_NEAR // 2) - (CMP_BLOCK - 1)
    t["cmp"] = np.where(mrow < CMP_NEAR, c(near), REL_MAX_DIST)
    r128 = np.arange(LANES)
    t["ms"] = np.stack([c(LANES - r128), np.full(LANES, REL_MAX_DIST), np.zeros(LANES, np.int64)])
    nrow = _cmp_rows(past)
    t["cs"] = c(past - CMP_STRIDE * np.arange(nrow) - (CMP_BLOCK - 1))
    t["ws"] = c(WINDOW - np.arange(WINDOW))
    return t


def _cmp_rows(past):
    return -(-(past // CMP_STRIDE + 4) // 16) * 16


def _tab_kernel(idx_ref, rb_ref, o_ref):
    idx = idx_ref[...]
    b = lax.broadcasted_iota(I32, (REL_BUCKETS, idx.shape[1]), 0)
    oh = jnp.where(b == idx, 1.0, 0.0).astype(BF16)
    o_ref[...] = _mm(rb_ref[0], oh) + _mm(rb_ref[1], oh) + _mm(rb_ref[2], oh)


def _bias_tables(rel_bias, seq, past):
    pats = _dist_tables(seq, past)
    bucket = _bucket_of_dist()
    chunk = 16384
    flat, spans, off = [], {}, 0
    for name, d in pats.items():
        n = d.size
        pad = -n % LANES
        flat.append(bucket[d.reshape(-1)])
        flat.append(np.zeros(pad, np.int32))
        spans[name] = (off, n, d.shape)
        off += n + pad
    total = -(-off // chunk) * chunk
    flat.append(np.zeros(total - off, np.int32))
    idx = jnp.asarray(np.concatenate(flat).astype(np.int32)).reshape(1, total)
    rb3 = jnp.stack(_split3(rel_bias.T.astype(F32)))
    tab = pl.pallas_call(
        _tab_kernel,
        grid=(total // chunk,),
        in_specs=[pl.BlockSpec((1, chunk), lambda i: (0, i)),
                  pl.BlockSpec((3, N_HEADS, REL_BUCKETS), lambda i: (0, 0, 0))],
        out_specs=pl.BlockSpec((N_HEADS, chunk), lambda i: (0, i)),
        out_shape=jax.ShapeDtypeStruct((N_HEADS, total), F32),
        compiler_params=_cparams("parallel"),
        name="bias_tables",
    )(idx, rb3)
    return {name: tab[:, o:o + n].reshape((N_HEADS,) + shp) for name, (o, n, shp) in spans.items()}


def _group_mats():
    i = np.arange(2 * LANES)
    g64 = (i[:, None] // HEAD_DIM == i[None, :] // HEAD_DIM)
    g128 = (i[:, None] // LANES == i[None, :] // LANES) & ((i[:, None] % LANES) < HEAD_DIM)
    return jnp.asarray(g64, BF16), jnp.asarray(g128, BF16)


def _interleave_gain(g, n):
    return jnp.tile(jnp.concatenate([g.astype(F32), jnp.ones((HEAD_DIM,), F32)]), n).reshape(1, n * LANES)


def _store_rows_or_columns(ref, sl, x):
    if len(ref.shape) == 3:
        ref[0, sl, :] = x.T.astype(ref.dtype)
    else:
        ref[:, sl] = x.astype(ref.dtype)


def _token_major(x_t, b, s, n_kv):
    return x_t.reshape(b, n_kv, 2, HEAD_DIM, s).transpose(0, 4, 1, 2, 3)


def _proj_moba_kernel(x_ref, an_ref, wq_ref, wkv_ref, qg_ref, kg_ref, g64_ref, g128_ref,
                      q_ref, kv_ref, kvb_ref, *rest, n_mean):
    xn = _rms_rows(x_ref[...], an_ref[...]).astype(BF16)
    cw = 2 * LANES
    for c in range(wq_ref.shape[1] // cw):
        sl = slice(c * cw, (c + 1) * cw)
        h = _mm(xn, wq_ref[:, sl])
        q = _group_norm(h, g64_ref, qg_ref[:, sl], False)
        _store_rows_or_columns(q_ref, sl, q)
    for c in range(wkv_ref.shape[1] // cw):
        sl = slice(c * cw, (c + 1) * cw)
        h = _mm(xn, wkv_ref[:, sl])
        kv = _group_norm(h, g128_ref, kg_ref[:, sl], True)
        _store_rows_or_columns(kv_ref, sl, kv)
        kvb_ref[:, sl] = kv.astype(BF16)
        if n_mean:
            km_ref, kvtb_ref = rest
            _store_rows_or_columns(kvtb_ref, sl, kv)
            for r in range(n_mean):
                km_ref[0, r:r + 1, sl] = jnp.mean(kv[r * MOBA_BLOCK:(r + 1) * MOBA_BLOCK], axis=0, keepdims=True)


def _proj_moba(x, an, w_qkv, q_gain, k_gain, *, qdt, seq=None):
    m, d = x.shape
    tm = min(m, 512)
    with_mean = seq is not None
    g64, g128 = _group_mats()
    w3 = w_qkv.reshape(d, 3, N_HEADS, HEAD_DIM)
    wkv = jnp.stack([w3[:, 1], w3[:, 2]], axis=2).reshape(d, 2 * d).astype(BF16)
    wq = w3[:, 0].reshape(d, d).astype(BF16)
    qg = jnp.tile(q_gain.astype(F32), N_HEADS).reshape(1, d)
    kg = _interleave_gain(k_gain, N_HEADS)
    qn = wq.shape[1]
    n_mean = tm // MOBA_BLOCK if with_mean else 0
    const = lambda i: (0, 0)
    if seq is None:
        q_shape, q_spec = (m, qn), pl.BlockSpec((tm, qn), lambda i: (i, 0))
        kv_shape, kv_spec = (m, 2 * d), pl.BlockSpec((tm, 2 * d), lambda i: (i, 0))
    else:
        nt = seq // tm
        q_shape, q_spec = (m // seq, qn, seq), pl.BlockSpec((1, qn, tm), lambda i: (i // nt, 0, i % nt))
        kv_shape, kv_spec = (m // seq, 2 * d, seq), pl.BlockSpec((1, 2 * d, tm), lambda i: (i // nt, 0, i % nt))
    out_shape = [jax.ShapeDtypeStruct(q_shape, qdt), jax.ShapeDtypeStruct(kv_shape, F32),
                 jax.ShapeDtypeStruct((m, 2 * d), BF16)]
    out_specs = [q_spec, kv_spec, pl.BlockSpec((tm, 2 * d), lambda i: (i, 0))]
    if n_mean:
        out_shape += [jax.ShapeDtypeStruct((m // tm, n_mean, 2 * d), F32), jax.ShapeDtypeStruct(kv_shape, BF16)]
        out_specs += [pl.BlockSpec((1, n_mean, 2 * d), lambda i: (i, 0, 0)), kv_spec]
    return pl.pallas_call(
        functools.partial(_proj_moba_kernel, n_mean=n_mean),
        grid=(m // tm,),
        in_specs=[pl.BlockSpec((tm, d), lambda i: (i, 0)), pl.BlockSpec((1, d), const),
                  pl.BlockSpec((d, qn), const), pl.BlockSpec((d, 2 * d), const),
                  pl.BlockSpec((1, qn), const), pl.BlockSpec((1, 2 * d), const),
                  pl.BlockSpec(g64.shape, const), pl.BlockSpec(g128.shape, const)],
        out_specs=out_specs, out_shape=out_shape,
        compiler_params=_cparams("parallel"),
        name="proj_moba",
    )(x, an.reshape(1, d).astype(F32), wq, wkv, qg, kg, g64, g128)


def _proj_nsa_kernel(x_ref, an_ref, wq_ref, wkv_ref, wg_ref, gb_ref, qg_ref, kg_ref, g64_ref, g128_ref,
                     q_ref, cmp_ref, sel_ref, win_ref, selb_ref, winb_ref, gate_ref, *rest):
    cmp_t_ref, selb_t_ref, winb_t_ref = rest if rest else (None, None, None)
    xn = _rms_rows(x_ref[...], an_ref[...]).astype(BF16)
    cw = 2 * LANES
    for c in range(wq_ref.shape[1] // cw):
        sl = slice(c * cw, (c + 1) * cw)
        h = _mm(xn, wq_ref[:, sl])
        _store_rows_or_columns(q_ref, sl, _group_norm(h, g64_ref, qg_ref[:, sl], False))
    per_branch = NSA_KV_HEADS * LANES // cw
    for c in range(wkv_ref.shape[1] // cw):
        sl = slice(c * cw, (c + 1) * cw)
        br, cc = divmod(c, per_branch)
        osl = slice(cc * cw, (cc + 1) * cw)
        h = _mm(xn, wkv_ref[:, sl])
        if br == 0:
            cmp_ref[:, osl] = h
            if cmp_t_ref is not None:
                _store_rows_or_columns(cmp_t_ref, osl, h)
        else:
            kv = _group_norm(h, g128_ref, kg_ref[:, sl], True)
            o32, o16, o16t = (sel_ref, selb_ref, selb_t_ref) if br == 1 else (win_ref, winb_ref, winb_t_ref)
            _store_rows_or_columns(o32, osl, kv)
            o16[:, osl] = kv.astype(BF16)
            if o16t is not None:
                _store_rows_or_columns(o16t, osl, kv)
    hg = _mm(xn, wg_ref[...]) + gb_ref[...]
    _store_rows_or_columns(gate_ref, slice(0, hg.shape[1]), 1.0 / (1.0 + jnp.exp(-hg)))


def _proj_nsa(x, an, w_in, gate_bias, q_gain, k_gain, *, qdt, seq=None):
    m, d = x.shape
    tm = min(m, 512)
    g64, g128 = _group_mats()
    kvd = N_BRANCH * NSA_KV_HEADS * LANES
    gw = NSA_KV_HEADS * LANES
    ng = NSA_GROUP * N_BRANCH
    qg = jnp.tile(q_gain.astype(F32), N_HEADS).reshape(1, d)
    wq = w_in[:, :d].astype(BF16)
    wkv = w_in[:, d:d + kvd].astype(BF16)
    wg = jnp.zeros((d, NSA_KV_HEADS, LANES), F32).at[:, :, :ng].set(
        w_in[:, d + kvd:].reshape(d, NSA_KV_HEADS, ng)).reshape(d, gw).astype(BF16)
    gb = jnp.zeros((NSA_KV_HEADS, LANES), F32).at[:, :ng].set(
        gate_bias.astype(F32).reshape(NSA_KV_HEADS, ng)).reshape(1, gw)
    kg = jnp.concatenate([jnp.ones((1, gw), F32), _interleave_gain(k_gain[1], NSA_KV_HEADS),
                          _interleave_gain(k_gain[2], NSA_KV_HEADS)], axis=1)
    qn = wq.shape[1]
    const = lambda i: (0, 0)
    row = lambda n: pl.BlockSpec((tm, n), lambda i: (i, 0))
    rows32 = jax.ShapeDtypeStruct((m, gw), F32)
    if seq is None:
        q_shape, q_spec = jax.ShapeDtypeStruct((m, qn), qdt), row(qn)
        kv_shape, kv_spec, gate_shape, gate_spec, extra_shape, extra_spec = rows32, row(gw), rows32, row(gw), [], []
    else:
        nt = seq // tm
        cols = lambda n: pl.BlockSpec((1, n, tm), lambda i: (i // nt, 0, i % nt))
        q_shape, q_spec = jax.ShapeDtypeStruct((m // seq, qn, seq), qdt), cols(qn)
        kv_shape, kv_spec = jax.ShapeDtypeStruct((m // seq, gw, seq), F32), cols(gw)
        gate_shape, gate_spec = kv_shape, kv_spec
        extra_shape = [kv_shape] + [jax.ShapeDtypeStruct((m // seq, gw, seq), BF16)] * 2
        extra_spec = [kv_spec] * 3
    return pl.pallas_call(
        _proj_nsa_kernel,
        grid=(m // tm,),
        in_specs=[row(d), pl.BlockSpec((1, d), const), pl.BlockSpec((d, qn), const),
                  pl.BlockSpec((d, kvd), const), pl.BlockSpec((d, gw), const), pl.BlockSpec((1, gw), const),
                  pl.BlockSpec((1, qn), const), pl.BlockSpec((1, kvd), const),
                  pl.BlockSpec(g64.shape, const), pl.BlockSpec(g128.shape, const)],
        out_specs=[q_spec, row(gw), kv_spec, kv_spec, row(gw), row(gw), gate_spec] + extra_spec,
        out_shape=[q_shape, rows32, kv_shape, kv_shape]
                  + [jax.ShapeDtypeStruct((m, gw), BF16)] * 2 + [gate_shape] + extra_shape,
        compiler_params=_cparams("parallel"),
        name="proj_nsa",
    )(x, an.reshape(1, d).astype(F32), wq, wkv, wg, gb, qg, kg, g64, g128)


def _mlp_kernel(h_ref, o_ref, wo_ref, g_ref, wup_ref, wdn_ref, out_ref, xn_s):
    @pl.when(pl.program_id(1) == 0)
    def _():
        h1 = h_ref[...] + _mm(o_ref[...], wo_ref[...])
        out_ref[...] = h1
        xn_s[...] = _rms_rows(h1, g_ref[...]).astype(BF16)

    u = _mm(xn_s[...], wup_ref[...])
    u = jnp.square(jnp.maximum(u, 0.0)).astype(BF16)
    out_ref[...] += _mm(u, wdn_ref[...])


def _attn_out_mlp(h, o, w_o, fn, w_up, w_down):
    m, d = h.shape
    dff = w_up.shape[1]
    tm = min(m, 1024)
    tf = 1024
    return pl.pallas_call(
        _mlp_kernel,
        grid=(m // tm, dff // tf),
        in_specs=[pl.BlockSpec((tm, d), lambda i, f: (i, 0)), pl.BlockSpec((tm, d), lambda i, f: (i, 0)),
                  pl.BlockSpec((d, d), lambda i, f: (0, 0)), pl.BlockSpec((1, d), lambda i, f: (0, 0)),
                  pl.BlockSpec((d, tf), lambda i, f: (0, f)), pl.BlockSpec((tf, d), lambda i, f: (f, 0))],
        out_specs=pl.BlockSpec((tm, d), lambda i, f: (i, 0)),
        out_shape=jax.ShapeDtypeStruct((m, d), F32),
        scratch_shapes=[pltpu.VMEM((tm, d), BF16)],
        compiler_params=_cparams("parallel", "arbitrary"),
        name="attn_out_mlp",
    )(h, o.astype(BF16), w_o.astype(BF16), fn.reshape(1, d).astype(F32), w_up.astype(BF16), w_down.astype(BF16))


def _topk_rows(x, k, on_pick):
    n = x.shape[0]
    row = lax.broadcasted_iota(I32, x.shape, 0)
    for _ in range(k):
        m = jnp.max(x, axis=0, keepdims=True)
        idx = jnp.min(jnp.where(x == m, row, n), axis=0, keepdims=True)
        hit = row == idx
        on_pick(hit, m)
        x = jnp.where(hit, -jnp.inf, x)


SUM_ROWS = 16


def _with_sum_rows(v_t):
    return jnp.concatenate([v_t, jnp.ones((SUM_ROWS, v_t.shape[1]), v_t.dtype)], axis=0)


def _online_softmax_step(m_ref, acc_ref, tiles):
    m_old = m_ref[0]
    mx = jnp.max(tiles[0][0], axis=0, keepdims=True)
    for s, _ in tiles[1:]:
        mx = jnp.maximum(mx, jnp.max(s, axis=0, keepdims=True))
    mn = jnp.maximum(m_old, mx)
    acc = jnp.exp(m_old - mn) * acc_ref[0]
    for s, v_aug in tiles:
        acc = acc + _mm(v_aug, jnp.exp(s - mn).astype(BF16))
    m_ref[0] = mn
    acc_ref[0] = acc


def _softmax_finish(acc):
    return acc[:HEAD_DIM] * (1.0 / jnp.maximum(acc[HEAD_DIM:HEAD_DIM + 1], 1e-30))


def _unrolled_range(n, unroll, visit):
    rem = lax.rem(n, unroll)
    done = 0
    size = 1
    while size < unroll:
        take = jnp.bitwise_and(rem, size)

        @pl.when(take != 0)
        def _(done=done, size=size):
            visit([done + k for k in range(size)])

        done = done + take
        size *= 2

    def body(i, carry):
        first = rem + unroll * i
        visit([first + k for k in range(unroll)])
        return carry

    lax.fori_loop(0, n // unroll, body, 0)


def _softmax_state_init(m_ref, acc_ref):
    m_ref[...] = jnp.full(m_ref.shape, -jnp.inf, F32)
    acc_ref[...] = jnp.zeros(acc_ref.shape, F32)


def _moba_prompt_kernel(qt_ref, kv_ref, kvt_ref, km_ref, tb_ref, o_ref, *state, nb):
    cur = pl.program_id(2)
    tq = MOBA_BLOCK
    key_lane = lax.broadcasted_iota(I32, (tq, LANES), 1)
    rown = lax.broadcasted_iota(I32, (nb, tq), 0)
    heads = [state[2 * hh:2 * hh + 2] for hh in range(MOBA_HEADS)]
    for st in heads:
        _softmax_state_init(*st)
    qzs, qaugs = [], []
    for hh in range(MOBA_HEADS):
        qt = qt_ref[0, hh * HEAD_DIM:(hh + 1) * HEAD_DIM, :] * jnp.asarray(ATTN_SCALE, BF16)
        qz = jnp.concatenate([qt, jnp.zeros_like(qt)], axis=0)
        km_hi, km_lo = _split2(km_ref[0, :, hh * LANES:(hh + 1) * LANES])
        gate = _mm(km_hi, qz) + _mm(km_lo, qz)
        state = [jnp.full((nb, tq), MASK_NEG, F32)]

        def pick(hit, m, state=state):
            state[0] = jnp.where(hit, jnp.where(m > -jnp.inf, 0.0, state[0]), state[0])

        _topk_rows(jnp.where(rown < cur, gate, -jnp.inf), MOBA_TOPK, pick)
        negt = jnp.concatenate([state[0], jnp.zeros((LANES - nb, tq), F32)], axis=0).astype(BF16)
        qzs.append(qz)
        qaugs.append(jnp.concatenate([qz, negt], axis=0))

    def k_rows(hh, n):
        return kv_ref[0, pl.ds(pl.multiple_of(n * tq, tq), tq), hh * LANES:(hh + 1) * LANES]

    def v_cols(hh, n):
        return _with_sum_rows(
            kvt_ref[0, hh * LANES + HEAD_DIM:(hh + 1) * LANES, pl.ds(pl.multiple_of(n * tq, tq), tq)])

    for hh in range(MOBA_HEADS):
        _online_softmax_step(*heads[hh], [(_mm(k_rows(hh, cur), qzs[hh]) + tb_ref[hh, 0], v_cols(hh, cur))])

    def past_blocks(blocks):
        for hh in range(MOBA_HEADS):
            tiles = []
            for n in blocks:
                onehot = jnp.where(key_lane == n, 1.0, 0.0).astype(BF16)
                s = _mm(jnp.concatenate([k_rows(hh, n), onehot], axis=1), qaugs[hh])
                tiles.append((s + tb_ref[hh, jnp.where(cur - n == 1, 1, 2)], v_cols(hh, n)))
            _online_softmax_step(*heads[hh], tiles)

    _unrolled_range(cur, MOBA_UNROLL, past_blocks)
    o_t = jnp.concatenate([_softmax_finish(acc[0]) for _, acc in heads], axis=0)
    o_ref[...] = o_t.T.astype(o_ref.dtype)


def _moba_prompt(qt, kvb, kvt, kmean, tb, b, s):
    d = qt.shape[1]
    nb = s // MOBA_BLOCK
    assert s % MOBA_BLOCK == 0 and MOBA_TOPK <= nb <= LANES
    nq = s // MOBA_BLOCK
    r = np.arange(MOBA_BLOCK)
    causal = np.zeros((3, MOBA_BLOCK, MOBA_BLOCK), np.float32)
    causal[0] = np.where(r[:, None] <= r[None, :], 0.0, -np.inf)
    tb = tb + jnp.asarray(causal)
    hps = MOBA_HEADS
    return pl.pallas_call(
        functools.partial(_moba_prompt_kernel, nb=nb),
        grid=(b, N_HEADS // hps, nq),
        in_specs=[pl.BlockSpec((1, hps * HEAD_DIM, MOBA_BLOCK), lambda i, h, t: (i, h, t)),
                  pl.BlockSpec((1, s, hps * LANES), lambda i, h, t: (i, 0, h)),
                  pl.BlockSpec((1, hps * LANES, s), lambda i, h, t: (i, h, 0)),
                  pl.BlockSpec((1, nb, hps * LANES), lambda i, h, t: (i, 0, h)),
                  pl.BlockSpec((hps, 3, MOBA_BLOCK, MOBA_BLOCK), lambda i, h, t: (h, 0, 0, 0))],
        out_specs=pl.BlockSpec((MOBA_BLOCK, hps * HEAD_DIM), lambda i, h, t: (i * nq + t, h)),
        out_shape=jax.ShapeDtypeStruct((b * s, d), BF16),
        scratch_shapes=[pltpu.VMEM((1, 1, MOBA_BLOCK), F32),
                        pltpu.VMEM((1, HEAD_DIM + SUM_ROWS, MOBA_BLOCK), F32)] * hps,
        compiler_params=_cparams("parallel", "parallel", "arbitrary"),
        name="moba_prompt",
    )(qt, kvb.reshape(b, s, 2 * d), kvt, kmean.reshape(b, nb, 2 * d), tb)


def _gelu_tanh(x):
    return 0.5 * x * (1.0 + jnp.tanh(math.sqrt(2.0 / math.pi) * (x + 0.044715 * (x * x * x))))


def _compress_weights(cmp_pos, w1, b1, w2, k_gain0):
    half = CMP_BLOCK // 2
    z = jnp.zeros((half, HEAD_DIM, CMP_HIDDEN), F32)

    def first_layer(lo):
        wk, wv = w1[0, lo:lo + half], w1[1, lo:lo + half]
        top = jnp.concatenate([wk, z], axis=2)
        bot = jnp.concatenate([z, wv], axis=2)
        return jnp.concatenate([top, bot], axis=1).reshape(half * LANES, 2 * CMP_HIDDEN).astype(BF16)

    zz = jnp.zeros((CMP_HIDDEN, HEAD_DIM), F32)
    w2bd = jnp.concatenate([jnp.concatenate([w2[0], zz], axis=1),
                            jnp.concatenate([zz, w2[1]], axis=1)], axis=0).astype(BF16)
    pos_a = cmp_pos[:half].reshape(half, LANES).astype(F32)
    pos_b = cmp_pos[half:].reshape(half, LANES).astype(F32)
    return (pos_a, pos_b, first_layer(0), first_layer(half), b1.reshape(1, 2 * CMP_HIDDEN).astype(F32), w2bd,
            _interleave_gain(k_gain0, 1))


def _compress_tail(xa_s, xb_s, wa_ref, wb_ref, b1_ref, w2_ref, kg_ref):
    ha = _mm(xa_s[...], wa_ref[...])
    hb = _mm(xb_s[...], wb_ref[...])
    rows = ha.shape[0]
    h = ha + pltpu.roll(hb, rows - 1, 0) + b1_ref[...]
    out = _mm(_gelu_tanh(h).astype(BF16), w2_ref[...])
    lane = lax.broadcasted_iota(I32, out.shape, 1)
    is_k = lane < HEAD_DIM
    ss = jnp.sum(jnp.where(is_k, out * out, 0.0), axis=1, keepdims=True)
    kn = out * lax.rsqrt(ss * (1.0 / HEAD_DIM) + NORM_EPS) * kg_ref[...]
    return jnp.where(is_k, kn, out)


def _compress_prompt_kernel(*refs, nseg):
    x_refs = refs[:NSA_KV_HEADS]
    pa_ref, pb_ref, wa_ref, wb_ref, b1_ref, w2_ref, kg_ref, o_ref, ot_ref, xa_s, xb_s = refs[NSA_KV_HEADS:]
    half = CMP_BLOCK // 2
    for g in range(NSA_KV_HEADS):
        for p in range(half):
            v = x_refs[g][0, pl.ds(p, nseg, stride=CMP_STRIDE), :]
            xa_s[g * nseg:(g + 1) * nseg, p * LANES:(p + 1) * LANES] = (v + pa_ref[p:p + 1, :]).astype(BF16)
            xb_s[g * nseg:(g + 1) * nseg, p * LANES:(p + 1) * LANES] = (v + pb_ref[p:p + 1, :]).astype(BF16)
    res = _compress_tail(xa_s, xb_s, wa_ref, wb_ref, b1_ref, w2_ref, kg_ref)
    for g in range(NSA_KV_HEADS):
        rows = res[g * nseg:(g + 1) * nseg]
        o_ref[0, g] = rows.astype(BF16)
        ot_ref[0, g] = rows.T.astype(BF16)


def _compress_prompt(kv_cmp, cw, b, s):
    nseg = s // CMP_STRIDE
    gw = NSA_KV_HEADS * LANES
    kdim = (CMP_BLOCK // 2) * LANES
    const = lambda i: (0, 0)
    return pl.pallas_call(
        functools.partial(_compress_prompt_kernel, nseg=nseg),
        grid=(b,),
        in_specs=[pl.BlockSpec((1, s, LANES), functools.partial(lambda i, g: (i, 0, g), g=g))
                  for g in range(NSA_KV_HEADS)] + [pl.BlockSpec(w.shape, const) for w in cw],
        out_specs=[pl.BlockSpec((1, NSA_KV_HEADS, nseg, LANES), lambda i: (i, 0, 0, 0)),
                   pl.BlockSpec((1, NSA_KV_HEADS, LANES, nseg), lambda i: (i, 0, 0, 0))],
        out_shape=[jax.ShapeDtypeStruct((b, NSA_KV_HEADS, nseg, LANES), BF16),
                   jax.ShapeDtypeStruct((b, NSA_KV_HEADS, LANES, nseg), BF16)],
        scratch_shapes=[pltpu.VMEM((NSA_KV_HEADS * nseg, kdim), BF16)] * 2,
        compiler_params=_cparams("parallel"),
        name="compress_prompt",
    )(*([kv_cmp.reshape(b, s, gw)] * NSA_KV_HEADS), *cw)


CMP_PAGES = 8


def _compress_sample_kernel(pt_ref, *refs, nseg, nrow):
    x_refs = refs[:CMP_PAGES]
    (new_ref, pa_ref, pb_ref, wa_ref, wb_ref, b1_ref, w2_ref, kg_ref, o_ref,
     xa_s, xb_s, xt_s) = refs[CMP_PAGES:]
    j = pl.program_id(1)
    for pg in range(CMP_PAGES):
        for g in range(NSA_KV_HEADS):
            xt_s[pg, g] = x_refs[pg][0, g].T
    half = CMP_BLOCK // 2
    per_page = PAGE_SIZE // CMP_STRIDE
    per_step = CMP_PAGES * per_page
    tail = nrow - nseg

    @pl.when(j == 0)
    def _():
        first = lax.broadcasted_iota(I32, (tail, LANES), 0) == 0
        for g in range(NSA_KV_HEADS):
            rows = slice(g * nrow + nseg, (g + 1) * nrow)
            for p in range(half):
                v = jnp.zeros((tail, LANES), F32)
                if p == 0:
                    v = jnp.where(first, new_ref[0, :, g * LANES:(g + 1) * LANES], 0.0)
                xa_s[rows, p * LANES:(p + 1) * LANES] = (v + pa_ref[p:p + 1, :]).astype(BF16)
                xb_s[rows, p * LANES:(p + 1) * LANES] = (v + pb_ref[p:p + 1, :]).astype(BF16)

    for g in range(NSA_KV_HEADS):
        r0 = pl.multiple_of(g * nrow + j * per_step, 16)
        for p in range(half):
            v = jnp.concatenate([xt_s[pg, g, pl.ds(p, per_page, stride=CMP_STRIDE), :]
                                 for pg in range(CMP_PAGES)], axis=0)
            xa_s[pl.ds(r0, per_step), p * LANES:(p + 1) * LANES] = (v + pa_ref[p:p + 1, :]).astype(BF16)
            xb_s[pl.ds(r0, per_step), p * LANES:(p + 1) * LANES] = (v + pb_ref[p:p + 1, :]).astype(BF16)

    @pl.when(j == pl.num_programs(1) - 1)
    def _():
        res = _compress_tail(xa_s, xb_s, wa_ref, wb_ref, b1_ref, w2_ref, kg_ref)
        for g in range(NSA_KV_HEADS):
            o_ref[0, g] = res[g * nrow:(g + 1) * nrow]


def _compress_sample(cache, page_table, new_rows, cw, past):
    db, n_pages = page_table.shape
    assert n_pages % CMP_PAGES == 0 and (CMP_PAGES * PAGE_SIZE // CMP_STRIDE) % 16 == 0
    gw = NSA_KV_HEADS * LANES
    nseg = past // CMP_STRIDE
    nrow = _cmp_rows(past)
    kdim = (CMP_BLOCK // 2) * LANES
    const = lambda i, j, pt: (0, 0)
    grid_spec = pltpu.PrefetchScalarGridSpec(
        num_scalar_prefetch=1,
        grid=(db, n_pages // CMP_PAGES),
        in_specs=[pl.BlockSpec((1, NSA_KV_HEADS, LANES, PAGE_SIZE),
                               functools.partial(lambda i, j, pt, pg: (pt[i * n_pages + CMP_PAGES * j + pg], 0, 0, 0),
                                                 pg=pg))
                  for pg in range(CMP_PAGES)]
                 + [pl.BlockSpec((1, 1, gw), lambda i, j, pt: (i, 0, 0))]
                 + [pl.BlockSpec(w.shape, const) for w in cw],
        out_specs=pl.BlockSpec((1, NSA_KV_HEADS, nrow, LANES), lambda i, j, pt: (i, 0, 0, 0)),
        scratch_shapes=[pltpu.VMEM((NSA_KV_HEADS * nrow, kdim), BF16)] * 2
                       + [pltpu.VMEM((CMP_PAGES, NSA_KV_HEADS, PAGE_SIZE, LANES), F32)],
    )
    cview = _slot_minor(cache, NSA_KV_HEADS)
    return pl.pallas_call(
        functools.partial(_compress_sample_kernel, nseg=nseg, nrow=nrow),
        grid_spec=grid_spec,
        out_shape=jax.ShapeDtypeStruct((db, NSA_KV_HEADS, nrow, LANES), F32),
        compiler_params=_cparams("parallel", "arbitrary"),
        name="compress_sample",
    )(page_table.reshape(-1), *([cview] * CMP_PAGES), new_rows.reshape(db, 1, gw), *cw)


def _imp_matrix(n_sel, n_rows, n_cmp):
    ratio = SEL_BLOCK // CMP_STRIDE
    lead = CMP_BLOCK // CMP_STRIDE - 1
    j = np.arange(n_sel)[:, None]
    n = np.arange(n_rows)[None, :]
    return ((n >= ratio * j - lead) & (n <= ratio * j + ratio - 1) & (n < n_cmp))


def _nsa_prompt_kernel(qt_ref, gt_ref, kc_ref, kct_ref, ks_ref, kst_ref, kw_ref, kwt_ref, tsel_ref, tw_ref,
                       tc_ref, at_ref, o_ref, m_s, acc_s, *, n_sel, ncp):
    t = pl.program_id(2)
    tq, tk, grp = NSA_TQ, NSA_TK, NSA_GROUP
    cols = grp * tq
    p0 = t * tq
    scale = jnp.asarray(ATTN_SCALE, BF16)
    qt = jnp.concatenate([qt_ref[0, p * HEAD_DIM:(p + 1) * HEAD_DIM, :] * scale for p in range(grp)], axis=1)
    qz = jnp.concatenate([qt, jnp.zeros_like(qt)], axis=0)
    qoff = jnp.bitwise_and(lax.broadcasted_iota(I32, (1, cols), 1), tq - 1)

    shift = lax.rem(t * (tq // CMP_STRIDE) - CMP_NEAR // 2 + ncp, ncp)
    bias_c = tc_ref[0, pl.ds(pl.multiple_of(ncp - shift, 8), ncp), :]
    c_end = lax.broadcasted_iota(I32, (ncp, cols), 0) * CMP_STRIDE + (CMP_BLOCK - 1)
    s_c = jnp.where(p0 + qoff >= c_end, _mm(kc_ref[0, 0], qz) + bias_c, -jnp.inf)
    m_c = jnp.max(s_c, axis=0, keepdims=True)
    e_c = jnp.exp(s_c - jnp.where(m_c == -jnp.inf, 0.0, m_c))
    p_c = e_c * (1.0 / jnp.maximum(jnp.sum(e_c, axis=0, keepdims=True), 1e-30))
    o_c = _mm(kct_ref[0, 0, HEAD_DIM:, :], p_c.astype(BF16))
    p_sum = p_c[:, 0:tq]
    for p in range(1, grp):
        p_sum = p_sum + p_c[:, p * tq:(p + 1) * tq]
    at = at_ref[...]
    imp = sum(_mm(at, part) for part in _split3(p_sum))

    blk = lax.broadcasted_iota(I32, (n_sel, tq), 0)
    cur = jnp.right_shift(p0 + lax.broadcasted_iota(I32, (n_sel, tq), 1), SEL_SHIFT)
    valid = blk <= cur
    forced = (blk == 0) | (blk >= cur - (N_LOCAL_SEL - 1))
    score = jnp.where(valid, jnp.where(forced, jnp.inf, imp), -jnp.inf)
    state = [jnp.full((n_sel, tq), MASK_NEG, F32)]

    def pick(hit, m):
        state[0] = jnp.where(hit, jnp.where(m > -jnp.inf, 0.0, state[0]), state[0])

    _topk_rows(score, SEL_TOPK, pick)
    negt = state[0]
    if n_sel < LANES:
        negt = jnp.concatenate([negt, jnp.zeros((LANES - n_sel, tq), F32)], axis=0)
    negt = negt.astype(BF16)
    qaug = jnp.concatenate([qz, jnp.concatenate([negt] * grp, axis=1)], axis=0)

    kj = lax.broadcasted_iota(I32, (tk, LANES), 0)
    kl = lax.broadcasted_iota(I32, (tk, LANES), 1)

    def sel_tile(kt, variant):
        at_kt = pl.ds(pl.multiple_of(kt * tk, tk), tk)
        onehot = jnp.where(kl == kt * (tk // SEL_BLOCK) + jnp.right_shift(kj, SEL_SHIFT), 1.0, 0.0).astype(BF16)
        s = _mm(jnp.concatenate([ks_ref[0, at_kt, :], onehot], axis=1), qaug) + tsel_ref[0, variant]
        return s, _with_sum_rows(kst_ref[0, HEAD_DIM:, at_kt])

    _softmax_state_init(m_s, acc_s)
    kt_d = t // (tk // tq)
    _online_softmax_step(m_s, acc_s, [sel_tile(kt_d, lax.rem(t, tk // tq))])

    def past_tiles(kts):
        _online_softmax_step(m_s, acc_s, [sel_tile(kt, jnp.minimum((p0 - kt * tk) // tq, 3)) for kt in kts])

    _unrolled_range(kt_d, NSA_UNROLL, past_tiles)
    o_s = _softmax_finish(acc_s[0])

    at_w = pl.ds(pl.multiple_of(p0, tq), WIN_KEYS)
    wj = lax.broadcasted_iota(I32, (WIN_KEYS, cols), 0)
    s_w = jnp.where(p0 + wj >= WINDOW, _mm(kw_ref[0, at_w, :], qz) + tw_ref[0], -jnp.inf)
    m_w = jnp.max(s_w, axis=0, keepdims=True)
    e_w = jnp.exp(s_w - jnp.where(m_w == -jnp.inf, 0.0, m_w))
    o_w = _softmax_finish(_mm(_with_sum_rows(kwt_ref[0, HEAD_DIM:, at_w]), e_w.astype(BF16)))

    gt = gt_ref[0]
    per_head = []
    for p in range(grp):
        c = slice(p * tq, (p + 1) * tq)
        g = [gt[N_BRANCH * p + k:N_BRANCH * p + k + 1, :] for k in range(N_BRANCH)]
        per_head.append(g[0] * o_c[:, c] + g[1] * o_s[:, c] + g[2] * o_w[:, c])
    o_ref[...] = jnp.concatenate(per_head, axis=0).T.astype(o_ref.dtype)


def _heads_on_lanes(tab, lead):
    n = len(lead)
    x = tab.reshape((NSA_KV_HEADS, NSA_GROUP) + tab.shape[1:])
    perm = (0,) + tuple(range(2, 3 + n)) + (1, 3 + n)
    x = x.transpose(perm)
    return x.reshape(x.shape[:2 + n] + (NSA_GROUP * tab.shape[-1],))


def _nsa_prompt(qt, gt, kcvc, kcvc_t, selb, selb_t, winb, winb_t, tabs, b, s):
    d = qt.shape[1]
    gw = NSA_KV_HEADS * LANES
    ncp = s // CMP_STRIDE
    n_sel = s // SEL_BLOCK
    assert s % NSA_TK == 0 and SEL_TOPK <= n_sel <= LANES and ncp >= CMP_NEAR
    nq = s // NSA_TQ
    at = jnp.asarray(_imp_matrix(n_sel, ncp, ncp - 1), BF16)
    winp = jnp.pad(winb.reshape(b, s, gw), ((0, 0), (WINDOW, 0), (0, 0)))
    winp_t = jnp.pad(winb_t, ((0, 0), (0, 0), (WINDOW, 0)))
    gq = NSA_GROUP
    cols = gq * NSA_TQ
    kr = np.arange(NSA_TK)[:, None]
    wr = np.arange(WIN_KEYS)[:, None]
    qc = np.arange(cols)[None, :] % NSA_TQ
    sel_mask = np.zeros((4, NSA_TK, cols), np.float32)
    for v in range(NSA_TK // NSA_TQ):
        sel_mask[v] = np.where(kr <= qc + v * NSA_TQ, 0.0, -np.inf)
    win_mask = np.where((wr > qc) & (wr <= qc + WINDOW), 0.0, -np.inf).astype(np.float32)
    tsel = _heads_on_lanes(tabs["sel"], (4,)) + jnp.asarray(sel_mask)
    tw = _heads_on_lanes(tabs["win"], ()) + jnp.asarray(win_mask)
    tc = jnp.tile(_heads_on_lanes(tabs["cmp"], ()), (1, 2, 1))
    return pl.pallas_call(
        functools.partial(_nsa_prompt_kernel, n_sel=n_sel, ncp=ncp),
        grid=(b, NSA_KV_HEADS, nq),
        in_specs=[pl.BlockSpec((1, gq * HEAD_DIM, NSA_TQ), lambda i, g, t: (i, g, t)),
                  pl.BlockSpec((1, LANES, NSA_TQ), lambda i, g, t: (i, g, t)),
                  pl.BlockSpec((1, 1, ncp, LANES), lambda i, g, t: (i, g, 0, 0)),
                  pl.BlockSpec((1, 1, LANES, ncp), lambda i, g, t: (i, g, 0, 0)),
                  pl.BlockSpec((1, s, LANES), lambda i, g, t: (i, 0, g)),
                  pl.BlockSpec((1, LANES, s), lambda i, g, t: (i, g, 0)),
                  pl.BlockSpec((1, s + WINDOW, LANES), lambda i, g, t: (i, 0, g)),
                  pl.BlockSpec((1, LANES, s + WINDOW), lambda i, g, t: (i, g, 0)),
                  pl.BlockSpec((1, 4, NSA_TK, cols), lambda i, g, t: (g, 0, 0, 0)),
                  pl.BlockSpec((1, WIN_KEYS, cols), lambda i, g, t: (g, 0, 0)),
                  pl.BlockSpec((1, 2 * ncp, cols), lambda i, g, t: (g, 0, 0)),
                  pl.BlockSpec(at.shape, lambda i, g, t: (0, 0))],
        out_specs=pl.BlockSpec((NSA_TQ, gq * HEAD_DIM), lambda i, g, t: (i * nq + t, g)),
        out_shape=jax.ShapeDtypeStruct((b * s, d), BF16),
        scratch_shapes=[pltpu.VMEM((1, 1, cols), F32), pltpu.VMEM((1, HEAD_DIM + SUM_ROWS, cols), F32)],
        compiler_params=_cparams("parallel", "parallel", "arbitrary"),
        name="nsa_prompt",
    )(qt, gt, kcvc, kcvc_t, selb.reshape(b, s, gw), selb_t, winp, winp_t, tsel, tw, tc, at)


def _topk_lanes(x, k):
    ax = x.ndim - 1
    lane = lax.broadcasted_iota(I32, x.shape, ax)
    oshape = x.shape[:-1] + (LANES,)
    out_lane = lax.broadcasted_iota(I32, oshape, ax)
    ids = jnp.zeros(oshape, I32)
    for r in range(k):
        m = jnp.max(x, axis=ax, keepdims=True)
        idx = jnp.min(jnp.where(x == m, lane, x.shape[ax]), axis=ax, keepdims=True)
        ids = jnp.where(out_lane == r, idx, ids)
        x = jnp.where(lane == idx, -jnp.inf, x)
    return ids


def _slot_minor(cache, n_kv):
    pages, slots = cache.shape[:2]
    return jnp.transpose(cache, (0, 2, 3, 4, 1)).reshape(pages, n_kv, LANES, slots)


MOBA_SELECT_PAGES = 16


def _moba_select_kernel(pt_ref, *refs, nb):
    x_refs = refs[:MOBA_SELECT_PAGES]
    q_ref, ids_ref, ksum_s = refs[MOBA_SELECT_PAGES:]
    j = pl.program_id(1)
    ppb = MOBA_BLOCK // PAGE_SIZE
    per_step = MOBA_SELECT_PAGES // ppb

    @pl.when(j == 0)
    def _():
        ksum_s[...] = jnp.zeros_like(ksum_s)

    lane = lax.broadcasted_iota(I32, ksum_s.shape, 2)
    acc = ksum_s[...]
    for bi in range(per_step):
        x = x_refs[bi * ppb][0]
        for pg in range(1, ppb):
            x = x + x_refs[bi * ppb + pg][0]
        col = jnp.sum(x, axis=2, keepdims=True)
        acc = acc + jnp.where(lane == j * per_step + bi, col, 0.0)
    ksum_s[...] = acc

    @pl.when(j == pl.num_programs(1) - 1)
    def _():
        km = ksum_s[...] * (1.0 / MOBA_BLOCK)
        gate = sum(jnp.einsum("hqd,hdn->hqn", a, b_, preferred_element_type=F32)
                   for a in _split3(q_ref[0]) for b_ in _split3(km))
        blk = lax.broadcasted_iota(I32, gate.shape, 2)
        ids_ref[0] = _topk_lanes(jnp.where(blk < nb, gate, -jnp.inf), MOBA_TOPK)


def _moba_select(cache_t, page_table, q, past):
    db, n_pages = page_table.shape
    nb = past // MOBA_BLOCK
    assert past % MOBA_BLOCK == 0 and MOBA_TOPK <= nb <= LANES and n_pages % MOBA_SELECT_PAGES == 0
    q8 = jnp.pad(q.reshape(db, N_HEADS, 1, HEAD_DIM), ((0, 0), (0, 0), (0, 7), (0, 0)))
    grid_spec = pltpu.PrefetchScalarGridSpec(
        num_scalar_prefetch=1,
        grid=(db, n_pages // MOBA_SELECT_PAGES),
        in_specs=[pl.BlockSpec((1, N_HEADS, HEAD_DIM, PAGE_SIZE),
                               functools.partial(lambda i, j, pt, pg: (pt[i * n_pages + MOBA_SELECT_PAGES * j + pg],
                                                                       0, 0, 0), pg=pg))
                  for pg in range(MOBA_SELECT_PAGES)]
                 + [pl.BlockSpec((1, N_HEADS, 8, HEAD_DIM), lambda i, j, pt: (i, 0, 0, 0))],
        out_specs=pl.BlockSpec((1, N_HEADS, 8, LANES), lambda i, j, pt: (i, 0, 0, 0)),
        scratch_shapes=[pltpu.VMEM((N_HEADS, HEAD_DIM, LANES), F32)],
    )
    ids = pl.pallas_call(
        functools.partial(_moba_select_kernel, nb=nb),
        grid_spec=grid_spec,
        out_shape=jax.ShapeDtypeStruct((db, N_HEADS, 8, LANES), I32),
        compiler_params=_cparams("parallel", "arbitrary"),
        name="moba_select",
    )(page_table.reshape(-1), *([cache_t] * MOBA_SELECT_PAGES), q8)
    return ids[:, :, 0, :MOBA_TOPK]


def _pad_rows(x, rows):
    return jnp.concatenate([x, jnp.zeros((rows - x.shape[0],) + x.shape[1:], x.dtype)], axis=0)


MOBA_SAMPLE_HEADS = 4


def _moba_sample_kernel(pt_ref, ids_ref, q_ref, new_ref, *rest, n_tiles, last_blk, pages_per_block):
    tiles = rest[:MOBA_SAMPLE_HEADS * n_tiles]
    tab_ref, o_ref = rest[MOBA_SAMPLE_HEADS * n_tiles:]
    i, hg = pl.program_id(0), pl.program_id(1)
    for hh in range(MOBA_SAMPLE_HEADS):
        h = hg * MOBA_SAMPLE_HEADS + hh
        q8 = q_ref[0, hh] * ATTN_SCALE
        q8b = q8.astype(BF16)
        new = new_ref[0, hh]
        k_new, v_new = new[:, :HEAD_DIM], new[:, HEAD_DIM:]
        scores, vts = [], []
        for ti in range(n_tiles):
            k, pp = divmod(ti, pages_per_block)
            blk = ids_ref[(i * N_HEADS + h) * MOBA_TOPK + k]
            near = jnp.logical_and(blk == last_blk, pp == pages_per_block - 1)
            tile = tiles[hh * n_tiles + ti][0, 0]
            vts.append(tile[HEAD_DIM:].astype(BF16))
            scores.append(_mm(q8b, tile[:HEAD_DIM].astype(BF16))
                          + jnp.where(near, tab_ref[hh, 0:1, :], tab_ref[hh, 1:2, :]))
        s = jnp.concatenate(scores, axis=1)
        s_self = jnp.sum(q8 * k_new, axis=1, keepdims=True) + tab_ref[hh, 2:3, 0:1]
        m = jnp.maximum(jnp.max(s, axis=1, keepdims=True), s_self)
        p = jnp.exp(s - m)
        p_self = jnp.exp(s_self - m)
        l = jnp.sum(p, axis=1, keepdims=True) + p_self
        o = p_self * v_new
        for ti in range(n_tiles):
            o = o + _nt(p[:, ti * PAGE_SIZE:(ti + 1) * PAGE_SIZE].astype(BF16), vts[ti])
        o_ref[0, hh] = o / jnp.maximum(l, 1e-30)


def _moba_sample(cache_t, page_table, ids, q, kv_new, tab_ms, past):
    db, n_pages = page_table.shape
    ppb = MOBA_BLOCK // PAGE_SIZE
    n_tiles = MOBA_TOPK * ppb

    hps = MOBA_SAMPLE_HEADS

    def tile_spec(hh, ti):
        k, pp = divmod(ti, ppb)

        def imap(i, hg, pt, sel):
            h = hg * hps + hh
            return (pt[i * n_pages + sel[(i * N_HEADS + h) * MOBA_TOPK + k] * ppb + pp], h, 0, 0)
        return pl.BlockSpec((1, 1, LANES, PAGE_SIZE), imap)

    q8 = jnp.pad(q.reshape(db, N_HEADS, 1, HEAD_DIM), ((0, 0), (0, 0), (0, 7), (0, 0)))
    grid_spec = pltpu.PrefetchScalarGridSpec(
        num_scalar_prefetch=2,
        grid=(db, N_HEADS // hps),
        in_specs=[pl.BlockSpec((1, hps, 8, HEAD_DIM), lambda i, h, pt, sel: (i, h, 0, 0)),
                  pl.BlockSpec((1, hps, 1, LANES), lambda i, h, pt, sel: (i, h, 0, 0))]
                 + [tile_spec(hh, ti) for hh in range(hps) for ti in range(n_tiles)]
                 + [pl.BlockSpec((hps, 3, LANES), lambda i, h, pt, sel: (h, 0, 0))],
        out_specs=pl.BlockSpec((1, hps, 8, HEAD_DIM), lambda i, h, pt, sel: (i, h, 0, 0)),
    )
    out = pl.pallas_call(
        functools.partial(_moba_sample_kernel, n_tiles=n_tiles, last_blk=past // MOBA_BLOCK - 1,
                          pages_per_block=ppb),
        grid_spec=grid_spec,
        out_shape=jax.ShapeDtypeStruct((db, N_HEADS, 8, HEAD_DIM), F32),
        compiler_params=_cparams("parallel", "arbitrary"),
        name="moba_sample",
    )(page_table.reshape(-1), ids.reshape(-1), q8, kv_new.reshape(db, N_HEADS, 1, LANES),
      *([cache_t] * (hps * n_tiles)), tab_ms)
    return out[:, :, 0, :].reshape(db, N_HEADS * HEAD_DIM)


def _nsa_select_kernel(q_ref, kc_ref, tc_ref, a_ref, oc_ref, ids_ref, *, n_valid, n_sel, cur):
    nrow = kc_ref.shape[2]
    lane = lax.broadcasted_iota(I32, (8, nrow), 1)
    psums = []
    for g in range(NSA_KV_HEADS):
        q8 = (q_ref[0, g] * ATTN_SCALE).astype(BF16)
        kc = kc_ref[0, g].astype(BF16)
        s = jnp.where(lane < n_valid, _nt(q8, kc) + tc_ref[g], -jnp.inf)
        p_c = _softmax_rows(s)
        oc_ref[0, g] = _mm(p_c.astype(BF16), kc)[:, HEAD_DIM:]
        psums.append(jnp.sum(p_c[0:NSA_GROUP], axis=0, keepdims=True))
    p_sum = _pad_rows(jnp.concatenate(psums, axis=0), 8)
    a = a_ref[...]
    imp = sum(_mm(part, a) for part in _split3(p_sum))
    blk = lax.broadcasted_iota(I32, imp.shape, 1)
    valid = blk <= min(cur, n_sel - 1)
    forced = (blk == 0) | (blk >= cur - (N_LOCAL_SEL - 1))
    score = jnp.where(valid, jnp.where(forced, jnp.inf, imp), -jnp.inf)
    ids_ref[0] = _topk_lanes(score, SEL_TOPK)


def _nsa_select(q4, kcvc, tab_cs, past):
    db = q4.shape[0]
    nrow = kcvc.shape[2]
    n_cmp_valid = (past - (CMP_BLOCK - 1)) // CMP_STRIDE + 1
    n_sel = max(-(-(past + 1) // SEL_BLOCK), SEL_TOPK)
    n_cmp = n_sel * SEL_BLOCK // CMP_STRIDE - CMP_BLOCK // CMP_STRIDE + 1
    cur = past // SEL_BLOCK
    assert n_sel <= 2 * LANES and cur >= SEL_TOPK - 1 and n_cmp <= nrow
    a = jnp.asarray(_imp_matrix(2 * LANES, nrow, n_cmp).T, BF16)
    tc = jnp.concatenate([tab_cs.reshape(NSA_KV_HEADS, NSA_GROUP, nrow)] * 2, axis=1)
    return pl.pallas_call(
        functools.partial(_nsa_select_kernel, n_valid=n_cmp_valid, n_sel=n_sel, cur=cur),
        grid=(db,),
        in_specs=[pl.BlockSpec((1, NSA_KV_HEADS, 8, LANES), lambda i: (i, 0, 0, 0)),
                  pl.BlockSpec((1, NSA_KV_HEADS, nrow, LANES), lambda i: (i, 0, 0, 0)),
                  pl.BlockSpec(tc.shape, lambda i: (0, 0, 0)),
                  pl.BlockSpec(a.shape, lambda i: (0, 0))],
        out_specs=[pl.BlockSpec((1, NSA_KV_HEADS, 8, HEAD_DIM), lambda i: (i, 0, 0, 0)),
                   pl.BlockSpec((1, 8, LANES), lambda i: (i, 0, 0))],
        out_shape=[jax.ShapeDtypeStruct((db, NSA_KV_HEADS, 8, HEAD_DIM), F32),
                   jax.ShapeDtypeStruct((db, 8, LANES), I32)],
        compiler_params=_cparams("parallel"),
        name="nsa_select",
    )(q4, kcvc, tc, a)


def _nsa_sample_kernel(pt_ref, ids_ref, q_ref, oc_ref, gate_ref, snew_ref, wnew_ref, st_ref, *rest,
                       n_cache_blk):
    tiles = rest[:SEL_TOPK]
    tms_ref, tws_ref, eye_ref, o_ref, st_out = rest[SEL_TOPK:]
    i, g = pl.program_id(0), pl.program_id(1)
    q8 = q_ref[0, 0] * ATTN_SCALE
    q8b = q8.astype(BF16)
    halves = PAGE_SIZE // SEL_BLOCK
    last_page = n_cache_blk // halves - 1
    half_of_lane = jnp.right_shift(lax.broadcasted_iota(I32, (8, PAGE_SIZE), 1), SEL_SHIFT)
    t0 = tms_ref[0, 2][:, 0:1]

    scores, vts = [], []
    for k in range(SEL_TOPK):
        blk = ids_ref[(i * NSA_KV_HEADS + g) * SEL_TOPK + k]
        in_cache = blk < n_cache_blk
        half = jnp.where(in_cache, lax.rem(blk, halves), -1)
        near = jnp.logical_and(in_cache, blk // halves == last_page)
        tile = tiles[k][0, 0]
        vts.append(tile[HEAD_DIM:].astype(BF16))
        s = _mm(q8b, tile[:HEAD_DIM].astype(BF16)) + jnp.where(near, tms_ref[0, 0], tms_ref[0, 1])
        scores.append(jnp.where(half_of_lane == half, s, -jnp.inf))
    s = jnp.concatenate(scores, axis=1)
    snew = snew_ref[0]
    s_self = jnp.sum(q8 * snew[:, :HEAD_DIM], axis=1, keepdims=True) + t0
    m = jnp.maximum(jnp.max(s, axis=1, keepdims=True), s_self)
    p = jnp.exp(s - m)
    p_self = jnp.exp(s_self - m)
    l = jnp.sum(p, axis=1, keepdims=True) + p_self
    o_s = p_self * snew[:, HEAD_DIM:]
    for k in range(SEL_TOPK):
        o_s = o_s + _nt(p[:, k * PAGE_SIZE:(k + 1) * PAGE_SIZE].astype(BF16), vts[k])
    o_s = o_s / jnp.maximum(l, 1e-30)

    ws = st_ref[0, 0]
    wl = lax.broadcasted_iota(I32, (8, WINDOW), 1)
    s_w = jnp.where(wl >= 1, _mm(q8b, ws[:HEAD_DIM].astype(BF16)) + tws_ref[0], -jnp.inf)
    wnew = wnew_ref[0]
    w_self = jnp.sum(q8 * wnew[:, :HEAD_DIM], axis=1, keepdims=True) + t0
    m = jnp.maximum(jnp.max(s_w, axis=1, keepdims=True), w_self)
    p = jnp.exp(s_w - m)
    p_self = jnp.exp(w_self - m)
    l = jnp.sum(p, axis=1, keepdims=True) + p_self
    o_w = (_nt(p.astype(BF16), ws[HEAD_DIM:].astype(BF16)) + p_self * wnew[:, HEAD_DIM:]) / jnp.maximum(l, 1e-30)

    gate = gate_ref[0, 0]
    o_ref[0, 0] = gate[:, 0:1] * oc_ref[0, 0] + gate[:, 1:2] * o_s + gate[:, 2:3] * o_w
    new_col = sum(_nt(eye_ref[...], _pad_rows(part, 8)) for part in _split3(wnew))[:, 0:1]
    wcol = lax.broadcasted_iota(I32, ws.shape, 1)
    st_out[0, 0] = jnp.where(wcol == WINDOW - 1, new_col, pltpu.roll(ws, WINDOW - 1, 1))


def _nsa_sample(cache_t, state_t, page_table, ids, q4, o_c, gates, sel_new, win_new, tabs, past):
    db, n_pages = page_table.shape
    gw = NSA_KV_HEADS * LANES
    halves = PAGE_SIZE // SEL_BLOCK
    n_cache_blk = past // SEL_BLOCK
    assert state_t.shape[-1] == WINDOW and past >= WINDOW and past % PAGE_SIZE == 0

    def tile_spec(k):
        def imap(i, g, pt, sel):
            blk = jnp.minimum(sel[(i * NSA_KV_HEADS + g) * SEL_TOPK + k], n_cache_blk - 1)
            return (pt[i * n_pages + blk // halves], g, 0, 0)
        return pl.BlockSpec((1, 1, LANES, PAGE_SIZE), imap)

    grp = lambda n: pl.BlockSpec((1, 1, 8, n), lambda i, g, pt, sel: (i, g, 0, 0))
    new = pl.BlockSpec((1, 1, LANES), lambda i, g, pt, sel: (i, 0, g))
    st = pl.BlockSpec((1, 1, LANES, WINDOW), lambda i, g, pt, sel: (i, g, 0, 0))
    pad8 = lambda x: jnp.concatenate([x, jnp.zeros_like(x)], axis=-2)
    tms = pad8(tabs["ms"].reshape(NSA_KV_HEADS, NSA_GROUP, 3, LANES).transpose(0, 2, 1, 3))
    tws = pad8(tabs["ws"].reshape(NSA_KV_HEADS, NSA_GROUP, WINDOW))
    eye = jnp.asarray(np.eye(LANES), BF16)
    grid_spec = pltpu.PrefetchScalarGridSpec(
        num_scalar_prefetch=2,
        grid=(db, NSA_KV_HEADS),
        in_specs=[grp(HEAD_DIM), grp(HEAD_DIM), grp(LANES), new, new, st] + [tile_spec(k) for k in range(SEL_TOPK)]
                 + [pl.BlockSpec((1, 3, 8, LANES), lambda i, g, pt, sel: (g, 0, 0, 0)),
                    pl.BlockSpec((1, 8, WINDOW), lambda i, g, pt, sel: (g, 0, 0)),
                    pl.BlockSpec(eye.shape, lambda i, g, pt, sel: (0, 0))],
        out_specs=[grp(HEAD_DIM), st],
    )
    o, st_new = pl.pallas_call(
        functools.partial(_nsa_sample_kernel, n_cache_blk=n_cache_blk),
        grid_spec=grid_spec,
        out_shape=[jax.ShapeDtypeStruct((db, NSA_KV_HEADS, 8, HEAD_DIM), F32),
                   jax.ShapeDtypeStruct(state_t.shape, F32)],
        compiler_params=_cparams("parallel", "arbitrary"),
        name="nsa_sample",
    )(page_table.reshape(-1), ids[:, :NSA_KV_HEADS, :SEL_TOPK].reshape(-1), q4, o_c, gates,
      sel_new.reshape(db, 1, gw), win_new.reshape(db, 1, gw), state_t,
      *([cache_t] * SEL_TOPK), tms, tws, eye)
    return o[:, :, :NSA_GROUP, :].reshape(db, N_HEADS * HEAD_DIM), st_new


def kernel(x_prompt, x_sample, cache_moba_kv, cache_nsa_cmp_kv, cache_nsa_sel_kv, state_nsa_win_kv, page_table,
           rel_bias, attn_norm, ffn_norm, moba_w_qkv, moba_q_norm, moba_k_norm, moba_w_o, nsa_w_in,
           nsa_gate_bias, nsa_q_norm, nsa_k_norm, nsa_cmp_pos, nsa_cmp_w1, nsa_cmp_b1, nsa_cmp_w2, nsa_w_o,
           ffn_w_up, ffn_w_down):
    b, s, d = x_prompt.shape
    db = x_sample.shape[0]
    assert x_sample.shape[1] == 1 and d == N_HEADS * HEAD_DIM
    past = page_table.shape[1] * PAGE_SIZE
    depth = attn_norm.shape[0]
    tabs = _bias_tables(rel_bias, s, past)
    hp = x_prompt.reshape(b * s, d)
    hs = x_sample.reshape(db, d)
    outs = {k: [] for k in ("moba_p", "moba_s", "cmp_p", "cmp_s", "sel_p", "sel_s", "win_p", "win_s")}
    kvshape = lambda n, g: (n, -1, g, 2, HEAD_DIM)
    for i in range(depth):
        j = i // 2
        if i % 2 == 0:
            qp_t, kvp_t, kvpb, kmean, kvpb_t = _proj_moba(hp, attn_norm[i], moba_w_qkv[j], moba_q_norm[j],
                                                          moba_k_norm[j], qdt=BF16, seq=s)
            qs, kvs, _ = _proj_moba(hs, attn_norm[i], moba_w_qkv[j], moba_q_norm[j], moba_k_norm[j], qdt=F32)
            op = _moba_prompt(qp_t, kvpb, kvpb_t, kmean, tabs["moba"], b, s)
            cache_t = _slot_minor(cache_moba_kv[j], N_HEADS)
            ids = _moba_select(cache_t, page_table, qs, past)
            osm = _moba_sample(cache_t, page_table, ids, qs, kvs, tabs["ms"], past)
            outs["moba_p"].append(_token_major(kvp_t, b, s, N_HEADS))
            outs["moba_s"].append(kvs.reshape(kvshape(db, N_HEADS)))
            w_o = moba_w_o[j]
        else:
            qp_t, cp, sp_t, wp_t, spb, wpb, gp_t, cp_t, spb_t, wpb_t = _proj_nsa(
                hp, attn_norm[i], nsa_w_in[j], nsa_gate_bias[j], nsa_q_norm[j], nsa_k_norm[j], qdt=BF16, seq=s)
            qs, cs, ss, ws, _, _, gs = _proj_nsa(hs, attn_norm[i], nsa_w_in[j], nsa_gate_bias[j],
                                                 nsa_q_norm[j], nsa_k_norm[j], qdt=F32)
            cw = _compress_weights(nsa_cmp_pos[j], nsa_cmp_w1[j], nsa_cmp_b1[j], nsa_cmp_w2[j], nsa_k_norm[j, 0])
            kcvc_p, kcvc_pt = _compress_prompt(cp, cw, b, s)
            op = _nsa_prompt(qp_t, gp_t, kcvc_p, kcvc_pt, spb, spb_t, wpb, wpb_t, tabs, b, s)
            kcvc_s = _compress_sample(cache_nsa_cmp_kv[j], page_table, cs, cw, past)
            pad8 = lambda x: jnp.concatenate([x, jnp.zeros_like(x)], axis=2)
            q4 = pad8(qs.reshape(db, NSA_KV_HEADS, NSA_GROUP, HEAD_DIM))
            q4w = jnp.pad(q4, ((0, 0), (0, 0), (0, 0), (0, LANES - HEAD_DIM)))
            g4 = pad8(gs.reshape(db, NSA_KV_HEADS, LANES)[:, :, :NSA_GROUP * N_BRANCH]
                      .reshape(db, NSA_KV_HEADS, NSA_GROUP, N_BRANCH))
            g4 = jnp.pad(g4, ((0, 0), (0, 0), (0, 0), (0, LANES - N_BRANCH)))
            o_c, sel_ids = _nsa_select(q4w, kcvc_s, tabs["cs"], past)
            osm, st_t = _nsa_sample(_slot_minor(cache_nsa_sel_kv[j], NSA_KV_HEADS),
                                    _slot_minor(state_nsa_win_kv[j], NSA_KV_HEADS), page_table, sel_ids, q4, o_c,
                                    g4, ss, ws, tabs, past)
            ws_new = st_t.reshape(db, NSA_KV_HEADS, 2, HEAD_DIM, WINDOW).transpose(0, 4, 1, 2, 3)
            gsh = (NSA_KV_HEADS, 2, HEAD_DIM)
            outs["cmp_p"].append(_token_major(cp_t, b, s, NSA_KV_HEADS))
            outs["cmp_s"].append(cs.reshape((db, 1) + gsh))
            outs["sel_p"].append(_token_major(sp_t, b, s, NSA_KV_HEADS))
            outs["sel_s"].append(ss.reshape((db, 1) + gsh))
            outs["win_p"].append(_token_major(wp_t, b, s, NSA_KV_HEADS)[:, s - min(WINDOW, s):])
            outs["win_s"].append(ws_new.reshape((db, WINDOW) + gsh))
            w_o = nsa_w_o[j]
        hp = _attn_out_mlp(hp, op, w_o, ffn_norm[i], ffn_w_up[i], ffn_w_down[i])
        hs = _attn_out_mlp(hs, osm, w_o, ffn_norm[i], ffn_w_up[i], ffn_w_down[i])
    stack = lambda k: jnp.stack(outs[k])
    return (hp.reshape(b, s, d), hs.reshape(db, 1, d), stack("moba_p"), stack("moba_s"), stack("cmp_p"),
            stack("cmp_s"), stack("sel_p"), stack("sel_s"), stack("win_p"), stack("win_s"))
```
